```python
import math
import jax, jax.numpy as jnp
from jax import lax
import numpy as np

D_MODEL = 1024
BATCH = 4
SEQ = 4096
DEPTH = 2
DEC_BATCH = 4
DEC_SEQ = 8192
PAST_LEN = 128

M_HEADS = 8
M_V_DIM = D_MODEL // M_HEADS
M_QK_DIM = M_V_DIM // 2
M_CHUNK = 128
M_Q = M_HEADS * M_QK_DIM
M_K = M_HEADS * M_QK_DIM
M_V = M_HEADS * M_V_DIM
M_O = M_HEADS * M_V_DIM
M_G = 4 * M_HEADS
M_IN = M_Q + M_K + M_V + M_O + M_G

A_Q_HEADS = 16
A_KV_HEADS = 4
A_HEAD_DIM = D_MODEL // A_Q_HEADS
A_GROUP = A_Q_HEADS // A_KV_HEADS
WINDOW = 128
A_BLOCK = 128
A_Q = A_Q_HEADS * A_HEAD_DIM
A_KV = A_KV_HEADS * A_HEAD_DIM
A_IN = A_Q + 2 * A_KV
ROPE_THETA = 500000.0
ROPE_DIM = A_HEAD_DIM // 4

D_FF = 3584
N_EXPERTS = 8
TOP_K = 2
EPS = 1e-6

kernel_name = "hybrid_mlstm_swa_moe_encoder"


def rms_norm(x, g):
    xf = x.astype(jnp.float32)
    y = xf * lax.rsqrt(jnp.mean(xf * xf, axis=-1, keepdims=True) + EPS)
    return (y * g.astype(jnp.float32)).astype(x.dtype)


def mlstm_chunkwise(q, k, v, ig, lf):
    B, H, S, dk = q.shape
    dv = v.shape[-1]
    L = M_CHUNK
    nc = S // L
    f32 = jnp.float32
    qc = q.reshape(B, H, nc, L, dk).astype(f32) * (dk ** -0.5)
    kc = k.reshape(B, H, nc, L, dk).astype(f32)
    vc = v.reshape(B, H, nc, L, dv).astype(f32)
    ic = ig.reshape(B, H, nc, L).astype(f32)
    b = jnp.cumsum(lf.reshape(B, H, nc, L).astype(f32), axis=-1)
    g = b[..., -1]
    a = g[..., None] - b + ic
    a_max = jnp.max(a, axis=-1)

    def step(carry, xs):
        C, n, m = carry
        k_j, v_j, a_j, amax_j, g_j = xs
        m_new = jnp.maximum(g_j + m, amax_j)
        w = jnp.exp(a_j - m_new[..., None])
        decay = jnp.exp(g_j + m - m_new)
        kw = k_j * w[..., None]
        C_new = decay[..., None, None] * C + jnp.einsum('bhld,bhle->bhde', kw, v_j)
        n_new = decay[..., None] * n + jnp.sum(kw, axis=2)
        return (C_new, n_new, m_new), (C, n, m)

    init = (jnp.zeros((B, H, dk, dv), f32), jnp.zeros((B, H, dk), f32), jnp.zeros((B, H), f32))
    xs = (jnp.moveaxis(kc, 2, 0), jnp.moveaxis(vc, 2, 0), jnp.moveaxis(a, 2, 0),
          jnp.moveaxis(a_max, 2, 0), jnp.moveaxis(g, 2, 0))
    _, (C_prev, n_prev, m_prev) = lax.scan(step, init, xs)
    C_prev = jnp.moveaxis(C_prev, 0, 2)
    n_prev = jnp.moveaxis(n_prev, 0, 2)
    m_prev = jnp.moveaxis(m_prev, 0, 2)

    tri = jnp.tril(jnp.ones((L, L), dtype=bool))
    Dm = jnp.where(tri, b[..., :, None] - b[..., None, :] + ic[..., None, :], -jnp.inf)
    inter = b + m_prev[..., None]
    m_t = jnp.maximum(jnp.max(Dm, axis=-1), inter)
    s = jnp.einsum('bhctd,bhcsd->bhcts', qc, kc) * jnp.exp(Dm - m_t[..., None])
    e = jnp.exp(inter - m_t)
    num = jnp.einsum('bhcts,bhcse->bhcte', s, vc) + e[..., None] * jnp.einsum('bhctd,bhcde->bhcte', qc, C_prev)
    den = jnp.sum(s, axis=-1) + e * jnp.einsum('bhctd,bhcd->bhct', qc, n_prev)
    h = num / jnp.maximum(jnp.abs(den), jnp.exp(-m_t))[..., None]
    return h.reshape(B, H, S, dv)


def mlstm_mixer(x, w_in, gate_bias, head_gain, w_out):
    B, S, _ = x.shape
    f32 = jnp.float32
    p = x @ w_in
    q, k, v, o, gates = jnp.split(p, [M_Q, M_Q + M_K, M_Q + M_K + M_V, M_Q + M_K + M_V + M_O], axis=-1)
    to_heads = lambda t, d: t.reshape(B, S, M_HEADS, d).transpose(0, 2, 1, 3)
    q, k, v = to_heads(q, M_QK_DIM), to_heads(k, M_QK_DIM), to_heads(v, M_V_DIM)
    gates = (gates.astype(f32) + gate_bias.astype(f32)).reshape(B, S, 4, M_HEADS).transpose(2, 0, 3, 1)
    i_f, f_f, i_b, f_b = gates[0], gates[1], gates[2], gates[3]
    h_fwd = mlstm_chunkwise(q, k, v, i_f, jax.nn.log_sigmoid(f_f))
    flip = lambda t: jnp.flip(t, axis=2)
    h_bwd = flip(mlstm_chunkwise(flip(q), flip(k), flip(v), flip(i_b), flip(jax.nn.log_sigmoid(f_b))))
    h = h_fwd + h_bwd
    h = h * lax.rsqrt(jnp.mean(h * h, axis=-1, keepdims=True) + EPS)
    h = h.transpose(0, 2, 1, 3).reshape(B, S, M_V) * head_gain.astype(f32)
    h = jax.nn.sigmoid(o.astype(f32)) * h
    return h.astype(x.dtype) @ w_out


def rope_partial(x, pos):
    half = ROPE_DIM // 2
    inv = ROPE_THETA ** (-(jnp.arange(half, dtype=jnp.float32) * 2.0) / ROPE_DIM)
    ang = pos.astype(jnp.float32)[:, None] * inv[None, :]
    cos = jnp.cos(ang)[None, :, None, :]
    sin = jnp.sin(ang)[None, :, None, :]
    xr = x[..., :ROPE_DIM].astype(jnp.float32)
    x1, x2 = xr[..., :half], xr[..., half:]
    rot = jnp.concatenate([x1 * cos - x2 * sin, x2 * cos + x1 * sin], axis=-1)
    return jnp.concatenate([rot.astype(x.dtype), x[..., ROPE_DIM:]], axis=-1)


def window_attention(x, w_in, q_gain, k_gain, sink, w_out):
    B, S, _ = x.shape
    nb = S // A_BLOCK
    f32 = jnp.float32
    p = x @ w_in
    q, k, v = jnp.split(p, [A_Q, A_Q + A_KV], axis=-1)
    q = q.reshape(B, S, A_Q_HEADS, A_HEAD_DIM)
    k = k.reshape(B, S, A_KV_HEADS, A_HEAD_DIM)
    v = v.reshape(B, S, A_KV_HEADS, A_HEAD_DIM)
    q = rms_norm(q, q_gain)
    k = rms_norm(k, k_gain)
    pos = jnp.arange(S)
    q = rope_partial(q, pos)
    k = rope_partial(k, pos)
    qb = q.reshape(B, nb, A_BLOCK, A_KV_HEADS, A_GROUP, A_HEAD_DIM)
    padw = ((0, 0), (A_BLOCK, A_BLOCK), (0, 0), (0, 0))
    kp = jnp.pad(k, padw).reshape(B, nb + 2, A_BLOCK, A_KV_HEADS, A_HEAD_DIM)
    vp = jnp.pad(v, padw).reshape(B, nb + 2, A_BLOCK, A_KV_HEADS, A_HEAD_DIM)
    kwin = jnp.concatenate([kp[:, :-2], kp[:, 1:-1], kp[:, 2:]], axis=2)
    vwin = jnp.concatenate([vp[:, :-2], vp[:, 1:-1], vp[:, 2:]], axis=2)
    qpos = jnp.arange(S).reshape(nb, A_BLOCK)
    kpos = (jnp.arange(nb)[:, None] - 1) * A_BLOCK + jnp.arange(3 * A_BLOCK)[None, :]
    mask = (jnp.abs(kpos[:, None, :] - qpos[:, :, None]) <= WINDOW) & (kpos[:, None, :] >= 0) & (kpos[:, None, :] < S)
    mask = mask[None, :, None, None]
    s = jnp.einsum('bnqkgd,bnskd->bnkgqs', qb, kwin).astype(f32) * (A_HEAD_DIM ** -0.5)
    s = jnp.where(mask, s, -jnp.inf)
    sink_b = sink.astype(f32).reshape(1, 1, A_KV_HEADS, A_GROUP, 1, 1)
    m = jnp.maximum(jnp.max(s, axis=-1, keepdims=True), sink_b)
    pr = jnp.exp(s - m)
    pr = pr / (jnp.sum(pr, axis=-1, keepdims=True) + jnp.exp(sink_b - m))
    out = jnp.einsum('bnkgqs,bnskd->bnqkgd', pr.astype(v.dtype), vwin)
    return out.reshape(B, S, A_Q) @ w_out


def swiglu(x, w_gate, w_up, w_down):
    return (jax.nn.silu(x @ w_gate) * (x @ w_up)) @ w_down


def moe_swiglu(x, w_router, w_gate, w_up, w_down):
    B, S, D = x.shape
    t = x.reshape(B * S, D)
    logits = (t @ w_router).astype(jnp.float32)
    top_v, top_i = lax.top_k(logits, TOP_K)
    probs = jax.nn.softmax(top_v, axis=-1)
    gate = jnp.sum(jax.nn.one_hot(top_i, N_EXPERTS, dtype=jnp.float32) * probs[..., None], axis=1)
    y = jnp.zeros((B * S, D), jnp.float32)
    for e in range(N_EXPERTS):
        y = y + gate[:, e:e + 1] * swiglu(t, w_gate[e], w_up[e], w_down[e]).astype(jnp.float32)
    return y.astype(x.dtype).reshape(B, S, D)


def trunk(x, norm_mix, norm_ffn, mlstm_w_in, mlstm_gate_bias, mlstm_head_gain, mlstm_w_out,
          attn_w_in, attn_q_gain, attn_k_gain, attn_sink, attn_w_out,
          ffn_w_gate, ffn_w_up, ffn_w_down, moe_w_router, moe_w_gate, moe_w_up, moe_w_down):
    for i in range(DEPTH):
        j = i // 2
        h = rms_norm(x, norm_mix[i])
        if i % 2 == 0:
            x = x + mlstm_mixer(h, mlstm_w_in[j], mlstm_gate_bias[j], mlstm_head_gain[j], mlstm_w_out[j])
        else:
            x = x + window_attention(h, attn_w_in[j], attn_q_gain[j], attn_k_gain[j], attn_sink[j], attn_w_out[j])
        h = rms_norm(x, norm_ffn[i])
        if i % 2 == 0:
            x = x + swiglu(h, ffn_w_gate[j], ffn_w_up[j], ffn_w_down[j])
        else:
            x = x + moe_swiglu(h, moe_w_router[j], moe_w_gate[j], moe_w_up[j], moe_w_down[j])
    return x


def setup_inputs(seed: int = 0) -> dict:
    key = jax.random.key(seed)
    ks = jax.random.split(key, 24)
    f32 = jnp.float32
    n_a = (DEPTH + 1) // 2
    n_b = DEPTH // 2
    w = lambda k, shape, fan_in: jax.random.normal(k, shape, f32) * (fan_in ** -0.5)
    gain = lambda k, shape: 1.0 + 0.05 * jax.random.normal(k, shape, f32)
    i_bias = 0.1 * jax.random.normal(ks[4], (n_a, 2, M_HEADS), f32)
    f_bias = jnp.linspace(3.0, 6.0, M_HEADS, dtype=f32)[None, None, :] + 0.1 * jax.random.normal(ks[5], (n_a, 2, M_HEADS), f32)
    gate_bias = jnp.stack([i_bias[:, 0], f_bias[:, 0], i_bias[:, 1], f_bias[:, 1]], axis=1).reshape(n_a, M_G)
    return {
        "x_prompt": jax.random.normal(ks[0], (BATCH, SEQ, D_MODEL), f32),
        "x_sample": jax.random.normal(ks[1], (DEC_BATCH, DEC_SEQ, D_MODEL), f32),
        "norm_mix": gain(ks[2], (DEPTH, D_MODEL)),
        "norm_ffn": gain(ks[3], (DEPTH, D_MODEL)),
        "mlstm_w_in": w(ks[6], (n_a, D_MODEL, M_IN), D_MODEL),
        "mlstm_gate_bias": gate_bias,
        "mlstm_head_gain": gain(ks[7], (n_a, M_V)),
        "mlstm_w_out": w(ks[8], (n_a, M_V, D_MODEL), M_V),
        "attn_w_in": w(ks[9], (n_b, D_MODEL, A_IN), D_MODEL),
        "attn_q_gain": gain(ks[10], (n_b, A_HEAD_DIM)),
        "attn_k_gain": gain(ks[11], (n_b, A_HEAD_DIM)),
        "attn_sink": 0.5 * jax.random.normal(ks[12], (n_b, A_Q_HEADS), f32),
        "attn_w_out": w(ks[13], (n_b, A_Q, D_MODEL), A_Q),
        "ffn_w_gate": w(ks[14], (n_a, D_MODEL, D_FF), D_MODEL),
        "ffn_w_up": w(ks[15], (n_a, D_MODEL, D_FF), D_MODEL),
        "ffn_w_down": w(ks[16], (n_a, D_FF, D_MODEL), D_FF),
        "moe_w_router": w(ks[17], (n_b, D_MODEL, N_EXPERTS), D_MODEL),
        "moe_w_gate": w(ks[18], (n_b, N_EXPERTS, D_MODEL, D_FF), D_MODEL),
        "moe_w_up": w(ks[19], (n_b, N_EXPERTS, D_MODEL, D_FF), D_MODEL),
        "moe_w_down": w(ks[20], (n_b, N_EXPERTS, D_FF, D_MODEL), D_FF),
    }


def reference(x_prompt, x_sample, norm_mix, norm_ffn, mlstm_w_in, mlstm_gate_bias, mlstm_head_gain, mlstm_w_out,
              attn_w_in, attn_q_gain, attn_k_gain, attn_sink, attn_w_out,
              ffn_w_gate, ffn_w_up, ffn_w_down, moe_w_router, moe_w_gate, moe_w_up, moe_w_down):
    y_prompt = trunk(x_prompt, norm_mix, norm_ffn, mlstm_w_in, mlstm_gate_bias, mlstm_head_gain, mlstm_w_out,
                     attn_w_in, attn_q_gain, attn_k_gain, attn_sink, attn_w_out,
                     ffn_w_gate, ffn_w_up, ffn_w_down, moe_w_router, moe_w_gate, moe_w_up, moe_w_down)
    y_sample = trunk(x_sample, norm_mix, norm_ffn, mlstm_w_in, mlstm_gate_bias, mlstm_head_gain, mlstm_w_out,
                     attn_w_in, attn_q_gain, attn_k_gain, attn_sink, attn_w_out,
                     ffn_w_gate, ffn_w_up, ffn_w_down, moe_w_router, moe_w_gate, moe_w_up, moe_w_down)
    return (y_prompt, y_sample)
```

```python
import functools
import math

import jax
import jax.numpy as jnp
from jax import lax
from jax.experimental import pallas as pl
from jax.experimental.pallas import tpu as pltpu

F32 = jnp.float32
BF16 = jnp.bfloat16

D_MODEL = 1024
EPS = 1e-6
M_HEADS = 8
M_QK = 64
M_V = 128
M_CHUNK = 128
A_Q_HEADS = 16
A_KV_HEADS = 4
A_HD = 64
A_GROUP = A_Q_HEADS // A_KV_HEADS
A_BLOCK = 128
WINDOW = 128
ROPE_THETA = 500000.0
ROPE_DIM = 16
D_FF = 3584
N_EXPERTS = 8

LANES = 128
VMEM_BYTES_V7X = 64 * 1024 * 1024

_NT = (((1,), (1,)), ((), ()))


def _params(semantics, vmem_mb):
    assert vmem_mb * 1024 * 1024 < VMEM_BYTES_V7X
    return pltpu.CompilerParams(dimension_semantics=semantics,
                                vmem_limit_bytes=vmem_mb * 1024 * 1024)


def _tile(total, pref):
    t = min(pref, total)
    t -= t % LANES
    while total % t:
        t -= LANES
    return t


def _rms(x, g):
    ms = jnp.mean(x * x, axis=-1, keepdims=True)
    return x * lax.rsqrt(ms + EPS) * g


def _sigmoid(x):
    return 1.0 / (1.0 + jnp.exp(-x))


def _dot(a, b):
    return jnp.dot(a, b, preferred_element_type=F32)


def _seq_pos(row0, m1, s1, s2):
    first = row0 < m1
    pos = jnp.where(first, row0 % s1, (row0 - m1) % s2)
    return pos, jnp.where(first, s1, s2)


def _mlstm_in_kernel(x_ref, g_ref, wq_ref, wkT_ref, wv_ref, wo_ref, wgT_ref, bgT_ref,
                     q_ref, kT_ref, v_ref, o_ref, gT_ref):
    hn = _rms(x_ref[...], g_ref[...]).astype(BF16)
    q_ref[...] = (_dot(hn, wq_ref[...]) * (M_QK ** -0.5)).astype(BF16)
    v_ref[...] = _dot(hn, wv_ref[...]).astype(BF16)
    o_ref[...] = _dot(hn, wo_ref[...]).astype(BF16)
    kT = lax.dot_general(wkT_ref[...], hn, _NT, preferred_element_type=F32).astype(BF16)
    gT = lax.dot_general(wgT_ref[...], hn, _NT, preferred_element_type=F32) + bgT_ref[...]
    log_sig = jnp.minimum(gT, 0.0) - jnp.log1p(jnp.exp(-jnp.abs(gT)))
    row = lax.broadcasted_iota(jnp.int32, gT.shape, 0)
    gT = jnp.where(row % 2 == 1, log_sig, gT)
    for j in range(kT_ref.shape[0]):
        kT_ref[j] = kT[:, j * M_CHUNK:(j + 1) * M_CHUNK]
        gT_ref[j] = gT[:, j * M_CHUNK:(j + 1) * M_CHUNK]


def _mlstm_in(x, g, wq, wkT, wv, wo, wgT, bgT, tm):
    m = x.shape[0]
    nq = M_HEADS * M_QK
    nv = M_HEADS * M_V
    ng = 4 * M_HEADS
    row = lambda i: (i, 0)
    fix = lambda i: (0, 0)
    return pl.pallas_call(
        _mlstm_in_kernel,
        grid=(m // tm,),
        in_specs=[
            pl.BlockSpec((tm, D_MODEL), row),
            pl.BlockSpec((1, D_MODEL), fix),
            pl.BlockSpec((D_MODEL, nq), fix),
            pl.BlockSpec((nq, D_MODEL), fix),
            pl.BlockSpec((D_MODEL, nv), fix),
            pl.BlockSpec((D_MODEL, nv), fix),
            pl.BlockSpec((ng, D_MODEL), fix),
            pl.BlockSpec((ng, 1), fix),
        ],
        out_specs=[
            pl.BlockSpec((tm, nq), row),
            pl.BlockSpec((tm // M_CHUNK, nq, M_CHUNK), lambda i: (i, 0, 0)),
            pl.BlockSpec((tm, nv), row),
            pl.BlockSpec((tm, nv), row),
            pl.BlockSpec((tm // M_CHUNK, ng, M_CHUNK), lambda i: (i, 0, 0)),
        ],
        out_shape=[
            jax.ShapeDtypeStruct((m, nq), BF16),
            jax.ShapeDtypeStruct((m // M_CHUNK, nq, M_CHUNK), BF16),
            jax.ShapeDtypeStruct((m, nv), BF16),
            jax.ShapeDtypeStruct((m, nv), BF16),
            jax.ShapeDtypeStruct((m // M_CHUNK, ng, M_CHUNK), F32),
        ],
        compiler_params=_params(("parallel",), 40),
        name="mlstm_in",
    )(x, g, wq, wkT, wv, wo, wgT, bgT)


def _mlstm_unit(q_m, kT, v_aug, ig, lf, c_ref, m_ref, chain, reverse):
    L = M_CHUNK
    lane = lax.broadcasted_iota(jnp.int32, (8, L), 1)
    lf8 = jnp.broadcast_to(lf, (8, L))
    ig8 = jnp.broadcast_to(ig, (8, L))
    b = lf8
    sh = 1
    while sh < L:
        if reverse:
            b = b + jnp.where(lane < L - sh, pltpu.roll(b, L - sh, 1), 0.0)
        else:
            b = b + jnp.where(lane >= sh, pltpu.roll(b, sh, 1), 0.0)
        sh *= 2
    g = jnp.sum(lf8, axis=1, keepdims=True)
    a = g - b + ig8
    a_max = jnp.max(a, axis=1, keepdims=True)
    m_prev = jnp.max(m_ref[chain], axis=1, keepdims=True)
    m_new = jnp.maximum(g + m_prev, a_max)
    w = jnp.exp(a - a_max)[0:1]
    decay = jnp.exp(g + m_prev - m_new)[0:1]
    scale = jnp.exp(a_max - m_new)[0:1]

    b_col = jnp.broadcast_to(b[0:1], (L, L)).T
    r_row = (ig8 - b)[0:1]
    t_idx = lax.broadcasted_iota(jnp.int32, (L, L), 0)
    s_idx = lax.broadcasted_iota(jnp.int32, (L, L), 1)
    keep = (s_idx >= t_idx) if reverse else (s_idx <= t_idx)
    d_mat = jnp.where(keep, b_col + r_row, -jnp.inf)
    inter = jnp.max(b_col, axis=1, keepdims=True) + m_prev[0:1]
    m_t = jnp.maximum(jnp.max(d_mat, axis=1, keepdims=True), inter)
    s = _dot(q_m, kT) * jnp.exp(d_mat - m_t)
    e = jnp.exp(inter - m_t)
    c_prev = c_ref[chain]
    tot = _dot(s.astype(BF16), v_aug) + e * _dot(q_m, c_prev.astype(BF16))
    num = tot[:, :M_V]
    den = tot[:, M_V:]
    h = num / jnp.maximum(jnp.abs(den), jnp.exp(-m_t))

    kTw = (kT.astype(F32) * w).astype(BF16)
    c_ref[chain] = decay * c_prev + scale * _dot(kTw, v_aug)
    m_ref[chain] = jnp.broadcast_to(m_new, (8, L))
    return h


def _mlstm_core_kernel(qf_ref, kTf_ref, vf_ref, gf_ref, qb_ref, kTb_ref, vb_ref, gb_ref,
                       hf_ref, hb_ref, c_ref, m_ref, *, nchunk, rb, m1, s1, s2):
    i = pl.program_id(1)
    nblk = pl.num_programs(1)
    pos_f, _ = _seq_pos(i * rb, m1, s1, s2)
    pos_b, len_b = _seq_pos((nblk - 1 - i) * rb, m1, s1, s2)

    @pl.when(pos_f == 0)
    def _():
        for chain in (0, 2):
            c_ref[chain] = jnp.zeros(c_ref.shape[1:], F32)
            m_ref[chain] = jnp.zeros(m_ref.shape[1:], F32)

    @pl.when(pos_b + rb == len_b)
    def _():
        for chain in (1, 3):
            c_ref[chain] = jnp.zeros(c_ref.shape[1:], F32)
            m_ref[chain] = jnp.zeros(m_ref.shape[1:], F32)

    lane = lax.broadcasted_iota(jnp.int32, (M_CHUNK, LANES), 1)
    head_lanes = (lane < M_QK, lane >= M_QK)
    ones = jnp.ones((M_CHUNK, M_V), BF16)

    def body(c, carry):
        dirs = ((qf_ref, kTf_ref, vf_ref, gf_ref, hf_ref, c),
                (qb_ref, kTb_ref, vb_ref, gb_ref, hb_ref, nchunk - 1 - c))
        for dirn, (q_ref, kT_ref, v_ref, g_ref, out_ref, cc) in enumerate(dirs):
            rows = pl.ds(pl.multiple_of(cc * M_CHUNK, M_CHUNK), M_CHUNK)
            q_pair = q_ref[rows, :]
            kT = kT_ref[cc]
            gates = g_ref[cc]
            for hh in range(2):
                q_m = jnp.where(head_lanes[hh], q_pair, jnp.zeros_like(q_pair))
                v = v_ref[rows, hh * M_V:(hh + 1) * M_V]
                v_aug = jnp.concatenate([v, ones], axis=1)
                r = hh * 4 + dirn * 2
                h = _mlstm_unit(q_m, kT, v_aug, gates[r:r + 1], gates[r + 1:r + 2],
                                c_ref, m_ref, hh * 2 + dirn, reverse=bool(dirn))
                out_ref[rows, hh * M_V:(hh + 1) * M_V] = h
        return carry

    lax.fori_loop(0, nchunk, body, 0)


def _mlstm_core(q, kT, v, gT, rb, m1, s1, s2):
    m = q.shape[0]
    nblk = m // rb
    nchunk = rb // M_CHUNK
    npair = M_HEADS // 2
    fwd = lambda p, i: (i, p)
    bwd = lambda p, i: (nblk - 1 - i, p)
    fwd3 = lambda p, i: (i, p, 0)
    bwd3 = lambda p, i: (nblk - 1 - i, p, 0)

    def specs(im2, im3):
        return [
            pl.BlockSpec((rb, LANES), im2),
            pl.BlockSpec((nchunk, LANES, M_CHUNK), im3),
            pl.BlockSpec((rb, 2 * M_V), im2),
            pl.BlockSpec((nchunk, 8, M_CHUNK), im3),
        ]

    kern = functools.partial(_mlstm_core_kernel, nchunk=nchunk, rb=rb, m1=m1, s1=s1, s2=s2)
    return pl.pallas_call(
        kern,
        grid=(npair, nblk),
        in_specs=specs(fwd, fwd3) + specs(bwd, bwd3),
        out_specs=[pl.BlockSpec((rb, 2 * M_V), fwd), pl.BlockSpec((rb, 2 * M_V), bwd)],
        out_shape=[jax.ShapeDtypeStruct((m, M_HEADS * M_V), F32)] * 2,
        scratch_shapes=[pltpu.VMEM((4, LANES, 2 * M_V), F32), pltpu.VMEM((4, 8, M_CHUNK), F32)],
        compiler_params=_params(("parallel", "arbitrary"), 32),
        name="mlstm_core",
    )(q, kT, v, gT, q, kT, v, gT)


def _mlstm_out_kernel(hf_ref, hb_ref, o_ref, gain_ref, w_ref, x_ref, out_ref):
    h = hf_ref[...] + hb_ref[...]
    parts = []
    for hd in range(M_HEADS):
        hh = h[:, hd * M_V:(hd + 1) * M_V]
        ms = jnp.mean(hh * hh, axis=-1, keepdims=True)
        parts.append(hh * lax.rsqrt(ms + EPS))
    hn = jnp.concatenate(parts, axis=1) * gain_ref[...]
    hg = (_sigmoid(o_ref[...].astype(F32)) * hn).astype(BF16)
    out_ref[...] = x_ref[...] + _dot(hg, w_ref[...])


def _mlstm_out(hf, hb, o, gain, w, x, tm):
    m = x.shape[0]
    row = lambda i: (i, 0)
    fix = lambda i: (0, 0)
    blk = pl.BlockSpec((tm, D_MODEL), row)
    return pl.pallas_call(
        _mlstm_out_kernel,
        grid=(m // tm,),
        in_specs=[blk, blk, blk, pl.BlockSpec((1, D_MODEL), fix),
                  pl.BlockSpec((D_MODEL, D_MODEL), fix), blk],
        out_specs=blk,
        out_shape=jax.ShapeDtypeStruct((m, D_MODEL), F32),
        compiler_params=_params(("parallel",), 40),
        name="mlstm_out",
    )(hf, hb, o, gain, w, x)


def _ffn_kernel(x_ref, g_ref, wg_ref, wu_ref, wd_ref, out_ref, hn_ref, acc_ref):
    f = pl.program_id(1)

    @pl.when(f == 0)
    def _():
        hn_ref[...] = _rms(x_ref[...], g_ref[...]).astype(BF16)
        acc_ref[...] = jnp.zeros_like(acc_ref)

    hn = hn_ref[...]
    gate = _dot(hn, wg_ref[...])
    up = _dot(hn, wu_ref[...])
    act = (gate * _sigmoid(gate) * up).astype(BF16)
    acc_ref[...] += _dot(act, wd_ref[...])

    @pl.when(f == pl.num_programs(1) - 1)
    def _():
        out_ref[...] = x_ref[...] + acc_ref[...]


def _ffn(x, g, wg, wu, wd, tm, tf):
    m = x.shape[0]
    row = lambda i, f: (i, 0)
    return pl.pallas_call(
        _ffn_kernel,
        grid=(m // tm, D_FF // tf),
        in_specs=[
            pl.BlockSpec((tm, D_MODEL), row),
            pl.BlockSpec((1, D_MODEL), lambda i, f: (0, 0)),
            pl.BlockSpec((D_MODEL, tf), lambda i, f: (0, f)),
            pl.BlockSpec((D_MODEL, tf), lambda i, f: (0, f)),
            pl.BlockSpec((tf, D_MODEL), lambda i, f: (f, 0)),
        ],
        out_specs=pl.BlockSpec((tm, D_MODEL), row),
        out_shape=jax.ShapeDtypeStruct((m, D_MODEL), F32),
        scratch_shapes=[pltpu.VMEM((tm, D_MODEL), BF16), pltpu.VMEM((tm, D_MODEL), F32)],
        compiler_params=_params(("parallel", "arbitrary"), 48),
        name="ffn",
    )(x, g, wg, wu, wd)


def _rope_table_kernel(inv_ref, ma_ref, mb_ref, cos_ref, sa_ref, sb_ref):
    rows = cos_ref.shape[0]
    pos = pl.program_id(0) * rows + lax.broadcasted_iota(jnp.int32, (rows, LANES), 0)
    ang = pos.astype(F32) * inv_ref[...]
    sin = jnp.sin(ang)
    cos_ref[...] = jnp.cos(ang)
    sa_ref[...] = sin * ma_ref[...]
    sb_ref[...] = sin * mb_ref[...]


def _rope_tables(smax, rows):
    half = ROPE_DIM // 2
    d = jnp.arange(LANES) % A_HD
    inv = ROPE_THETA ** (-(jnp.arange(half, dtype=F32) * 2.0) / ROPE_DIM)
    inv_lane = jnp.where(d < ROPE_DIM, inv[d % half], 0.0).astype(F32)[None, :]
    ma = jnp.where(d < half, -1.0, 0.0).astype(F32)[None, :]
    mb = jnp.where((d >= half) & (d < ROPE_DIM), 1.0, 0.0).astype(F32)[None, :]
    fix = lambda i: (0, 0)
    vec = pl.BlockSpec((1, LANES), fix)
    tab = pl.BlockSpec((rows, LANES), lambda i: (i, 0))
    return pl.pallas_call(
        _rope_table_kernel,
        grid=(smax // rows,),
        in_specs=[vec, vec, vec],
        out_specs=[tab, tab, tab],
        out_shape=[jax.ShapeDtypeStruct((smax, LANES), F32)] * 3,
        compiler_params=_params(("parallel",), 16),
        name="rope_tables",
    )(inv_lane, ma, mb)


def _attn_in_kernel(x_ref, g_ref, wq_ref, wk_ref, wv_ref, qg_ref, kg_ref, gm_ref,
                    cos_ref, sa_ref, sb_ref, q_ref, k_ref, v_ref):
    hn = _rms(x_ref[...], g_ref[...]).astype(BF16)
    cos = cos_ref[...]
    sa = sa_ref[...]
    sb = sb_ref[...]
    half = ROPE_DIM // 2

    def norm_rope(x, gain, scale, out_ref):
        for j in range(x.shape[1] // LANES):
            xj = x[:, j * LANES:(j + 1) * LANES]
            ms = _dot((xj * xj).astype(BF16), gm_ref[...])
            y = xj * lax.rsqrt(ms + EPS) * gain
            y = y * cos + pltpu.roll(y, LANES - half, 1) * sa + pltpu.roll(y, half, 1) * sb
            out_ref[:, j * LANES:(j + 1) * LANES] = (y * scale).astype(BF16)

    norm_rope(_dot(hn, wq_ref[...]), qg_ref[...], A_HD ** -0.5, q_ref)
    norm_rope(_dot(hn, wk_ref[...]), kg_ref[...], 1.0, k_ref)
    v_ref[...] = _dot(hn, wv_ref[...]).astype(BF16)


def _attn_in(x, g, wq, wk, wv, qg, kg, gm, tabs, tm, m1, s1, s2):
    m = x.shape[0]
    nq = A_Q_HEADS * A_HD
    nkv = A_KV_HEADS * A_HD
    row = lambda i: (i, 0)
    fix = lambda i: (0, 0)

    def tab_map(i):
        pos, _ = _seq_pos(i * tm, m1, s1, s2)
        return (pos // tm, 0)

    tab = pl.BlockSpec((tm, LANES), tab_map)
    vec = pl.BlockSpec((1, LANES), fix)
    return pl.pallas_call(
        _attn_in_kernel,
        grid=(m // tm,),
        in_specs=[
            pl.BlockSpec((tm, D_MODEL), row),
            pl.BlockSpec((1, D_MODEL), fix),
            pl.BlockSpec((D_MODEL, nq), fix),
            pl.BlockSpec((D_MODEL, nkv), fix),
            pl.BlockSpec((D_MODEL, nkv), fix),
            vec, vec,
            pl.BlockSpec((LANES, LANES), fix),
            tab, tab, tab,
        ],
        out_specs=[pl.BlockSpec((tm, nq), row), pl.BlockSpec((tm, nkv), row),
                   pl.BlockSpec((tm, nkv), row)],
        out_shape=[jax.ShapeDtypeStruct((m, nq), BF16), jax.ShapeDtypeStruct((m, nkv), BF16),
                   jax.ShapeDtypeStruct((m, nkv), BF16)],
        compiler_params=_params(("parallel",), 40),
        name="attn_in",
    )(x, g, wq, wk, wv, qg, kg, gm, *tabs)


def _attn_kernel(sink_ref, q_ref, kp_ref, kc_ref, kn_ref, vp_ref, vc_ref, vn_ref, out_ref,
                 *, m1, s1, s2):
    blk = A_BLOCK
    pos0, slen = _seq_pos(pl.program_id(0) * blk, m1, s1, s2)
    prev_ok = pos0 > 0
    next_ok = pos0 + blk < slen
    t = lax.broadcasted_iota(jnp.int32, (blk, 3 * blk), 0)
    c = lax.broadcasted_iota(jnp.int32, (blk, 3 * blk), 1)
    valid = (jnp.abs(c - blk - t) <= WINDOW) & ((c >= blk) | prev_ok) & ((c < 2 * blk) | next_ok)

    kcat = jnp.concatenate([kp_ref[...], kc_ref[...], kn_ref[...]], axis=0).astype(F32)
    vcat = jnp.concatenate([vp_ref[...], vc_ref[...], vn_ref[...]], axis=0).astype(F32)
    lane_kv = lax.broadcasted_iota(jnp.int32, (3 * blk, LANES), 1)
    lane_q = lax.broadcasted_iota(jnp.int32, (blk, LANES), 1)

    def softmax_parts(s, sink):
        s = jnp.where(valid, s, -jnp.inf)
        m = jnp.maximum(jnp.max(s, axis=1, keepdims=True), sink)
        p = jnp.exp(s - m)
        denom = jnp.sum(p, axis=1, keepdims=True) + jnp.exp(sink - m)
        return p.astype(BF16), denom

    for h in range(A_KV_HEADS):
        tile, half = divmod(h, 2)
        own = (lane_kv < A_HD) if half == 0 else (lane_kv >= A_HD)

        def lo_hi(cat):
            mine = jnp.where(own, cat[:, tile * LANES:(tile + 1) * LANES], 0.0)
            other = pltpu.roll(mine, A_HD, 1)
            lo, hi = (mine, other) if half == 0 else (other, mine)
            return lo.astype(BF16), hi.astype(BF16)

        k_lo, k_hi = lo_hi(kcat)
        v_lo, v_hi = lo_hi(vcat)
        for j in range(A_GROUP // 2):
            grp = h * (A_GROUP // 2) + j
            q_pair = q_ref[:, grp * LANES:(grp + 1) * LANES]
            s_even = lax.dot_general(q_pair, k_lo, _NT, preferred_element_type=F32)
            s_odd = lax.dot_general(q_pair, k_hi, _NT, preferred_element_type=F32)
            p_even, d_even = softmax_parts(s_even, sink_ref[2 * grp])
            p_odd, d_odd = softmax_parts(s_odd, sink_ref[2 * grp + 1])
            o = _dot(p_even, v_lo) + _dot(p_odd, v_hi)
            o = o * jnp.where(lane_q < A_HD, 1.0 / d_even, 1.0 / d_odd)
            out_ref[:, grp * LANES:(grp + 1) * LANES] = o.astype(BF16)


def _attn(sink, q, k, v, m1, s1, s2):
    m = q.shape[0]
    blk = A_BLOCK
    nblk = m // blk
    nkv = A_KV_HEADS * A_HD
    cur = lambda i: (i, 0)
    prev = lambda i: (jnp.maximum(i - 1, 0), 0)
    nxt = lambda i: (jnp.minimum(i + 1, nblk - 1), 0)
    kv = lambda im: pl.BlockSpec((blk, nkv), im)
    kern = functools.partial(_attn_kernel, m1=m1, s1=s1, s2=s2)
    return pl.pallas_call(
        kern,
        grid=(nblk,),
        in_specs=[pl.BlockSpec(memory_space=pltpu.SMEM),
                  pl.BlockSpec((blk, D_MODEL), cur), kv(prev), kv(cur), kv(nxt),
                  kv(prev), kv(cur), kv(nxt)],
        out_specs=pl.BlockSpec((blk, D_MODEL), cur),
        out_shape=jax.ShapeDtypeStruct((m, D_MODEL), BF16),
        compiler_params=_params(("parallel",), 32),
        name="attn",
    )(sink, q, k, k, k, v, v, v)


def _proj_res_kernel(a_ref, w_ref, x_ref, out_ref):
    out_ref[...] = x_ref[...] + _dot(a_ref[...], w_ref[...])


def _proj_res(a, w, x, tm):
    m = x.shape[0]
    row = lambda i: (i, 0)
    blk = pl.BlockSpec((tm, D_MODEL), row)
    return pl.pallas_call(
        _proj_res_kernel,
        grid=(m // tm,),
        in_specs=[blk, pl.BlockSpec((D_MODEL, D_MODEL), lambda i: (0, 0)), blk],
        out_specs=blk,
        out_shape=jax.ShapeDtypeStruct((m, D_MODEL), F32),
        compiler_params=_params(("parallel",), 32),
        name="attn_out",
    )(a, w, x)


def _router_kernel(x_ref, g_ref, wr_ref, hn_ref, gate_ref):
    hn = _rms(x_ref[...], g_ref[...]).astype(BF16)
    hn_ref[...] = hn
    logits = _dot(hn, wr_ref[...])
    lane = lax.broadcasted_iota(jnp.int32, logits.shape, 1).astype(F32)
    logits = jnp.where(lane < N_EXPERTS, logits, -jnp.inf)
    v1 = jnp.max(logits, axis=1, keepdims=True)
    i1 = jnp.min(jnp.where(logits == v1, lane, float(LANES)), axis=1, keepdims=True)
    rest = jnp.where(lane == i1, -jnp.inf, logits)
    v2 = jnp.max(rest, axis=1, keepdims=True)
    i2 = jnp.min(jnp.where(rest == v2, lane, float(LANES)), axis=1, keepdims=True)
    e2 = jnp.exp(v2 - v1)
    p1 = 1.0 / (1.0 + e2)
    gate_ref[...] = jnp.where(lane == i1, p1, 0.0) + jnp.where(lane == i2, e2 * p1, 0.0)


def _router(x, g, wr, tm):
    m = x.shape[0]
    row = lambda i: (i, 0)
    fix = lambda i: (0, 0)
    return pl.pallas_call(
        _router_kernel,
        grid=(m // tm,),
        in_specs=[pl.BlockSpec((tm, D_MODEL), row), pl.BlockSpec((1, D_MODEL), fix),
                  pl.BlockSpec((D_MODEL, LANES), fix)],
        out_specs=[pl.BlockSpec((tm, D_MODEL), row), pl.BlockSpec((tm, LANES), row)],
        out_shape=[jax.ShapeDtypeStruct((m, D_MODEL), BF16), jax.ShapeDtypeStruct((m, LANES), F32)],
        compiler_params=_params(("parallel",), 32),
        name="router",
    )(x, g, wr)


def _moe_kernel(x_ref, hn_ref, gate_ref, wg_ref, wu_ref, wd_ref, out_ref, acc_ref):
    e = pl.program_id(1)
    f = pl.program_id(2)

    @pl.when((e == 0) & (f == 0))
    def _():
        acc_ref[...] = jnp.zeros_like(acc_ref)

    gates = gate_ref[...]
    lane = lax.broadcasted_iota(jnp.int32, gates.shape, 1)
    ge = jnp.sum(jnp.where(lane == e, gates, 0.0), axis=1, keepdims=True)
    hn = hn_ref[...]
    gate = _dot(hn, wg_ref[...])
    up = _dot(hn, wu_ref[...])
    act = (gate * _sigmoid(gate) * up * ge).astype(BF16)
    acc_ref[...] += _dot(act, wd_ref[...])

    @pl.when((e == pl.num_programs(1) - 1) & (f == pl.num_programs(2) - 1))
    def _():
        out_ref[...] = x_ref[...] + acc_ref[...]


def _moe(x, hn, gate, wg, wu, wd, tm, tf):
    m = x.shape[0]
    row = lambda i, e, f: (i, 0)
    return pl.pallas_call(
        _moe_kernel,
        grid=(m // tm, N_EXPERTS, D_FF // tf),
        in_specs=[
            pl.BlockSpec((tm, D_MODEL), row),
            pl.BlockSpec((tm, D_MODEL), row),
            pl.BlockSpec((tm, LANES), row),
            pl.BlockSpec((None, D_MODEL, tf), lambda i, e, f: (e, 0, f)),
            pl.BlockSpec((None, D_MODEL, tf), lambda i, e, f: (e, 0, f)),
            pl.BlockSpec((None, tf, D_MODEL), lambda i, e, f: (e, f, 0)),
        ],
        out_specs=pl.BlockSpec((tm, D_MODEL), row),
        out_shape=jax.ShapeDtypeStruct((m, D_MODEL), F32),
        scratch_shapes=[pltpu.VMEM((tm, D_MODEL), F32)],
        compiler_params=_params(("parallel", "arbitrary", "arbitrary"), 48),
        name="moe",
    )(x, hn, gate, wg, wu, wd)


def _mlstm_gate_perm():
    perm = []
    for pair in range(M_HEADS // 2):
        for hh in range(2):
            for gtype in range(4):
                perm.append(gtype * M_HEADS + 2 * pair + hh)
    return jnp.array(perm, jnp.int32)


def kernel(x_prompt, x_sample, norm_mix, norm_ffn, mlstm_w_in, mlstm_gate_bias, mlstm_head_gain,
           mlstm_w_out, attn_w_in, attn_q_gain, attn_k_gain, attn_sink, attn_w_out, ffn_w_gate,
           ffn_w_up, ffn_w_down, moe_w_router, moe_w_gate, moe_w_up, moe_w_down):
    b1, s1, _ = x_prompt.shape
    b2, s2, _ = x_sample.shape
    m1 = b1 * s1
    x = jnp.concatenate([x_prompt.reshape(m1, D_MODEL), x_sample.reshape(b2 * s2, D_MODEL)], axis=0)
    m = x.shape[0]
    seq_gcd = math.gcd(s1, s2)
    tm = _tile(seq_gcd, 512)
    rb = _tile(seq_gcd, 1024)
    tm_ffn = _tile(m, 1024)
    tf = 512
    depth = norm_mix.shape[0]
    vec = lambda a: a.astype(F32).reshape(1, -1)
    nq = M_HEADS * M_QK
    nv = M_HEADS * M_V

    tabs = _rope_tables(max(s1, s2), tm)
    group_mean = jnp.where((jnp.arange(LANES)[:, None] // A_HD) == (jnp.arange(LANES)[None, :] // A_HD),
                           1.0 / A_HD, 0.0).astype(BF16)
    gate_perm = _mlstm_gate_perm()

    for i in range(depth):
        j = i // 2
        if i % 2 == 0:
            w_in = mlstm_w_in[j]
            wq = w_in[:, :nq].astype(BF16)
            wkT = w_in[:, nq:2 * nq].T.astype(BF16)
            wv = w_in[:, 2 * nq:2 * nq + nv].astype(BF16)
            wo = w_in[:, 2 * nq + nv:2 * nq + 2 * nv].astype(BF16)
            wgT = w_in[:, 2 * nq + 2 * nv:][:, gate_perm].T.astype(BF16)
            bgT = mlstm_gate_bias[j].astype(F32)[gate_perm].reshape(-1, 1)
            q, kT, v, o, gT = _mlstm_in(x, vec(norm_mix[i]), wq, wkT, wv, wo, wgT, bgT, tm)
            hf, hb = _mlstm_core(q, kT, v, gT, rb, m1, s1, s2)
            x = _mlstm_out(hf, hb, o, vec(mlstm_head_gain[j]), mlstm_w_out[j].astype(BF16), x, tm)
            x = _ffn(x, vec(norm_ffn[i]), ffn_w_gate[j].astype(BF16), ffn_w_up[j].astype(BF16),
                     ffn_w_down[j].astype(BF16), tm_ffn, tf)
        else:
            w_in = attn_w_in[j]
            a_q = A_Q_HEADS * A_HD
            a_kv = A_KV_HEADS * A_HD
            qg = jnp.tile(attn_q_gain[j].astype(F32), LANES // A_HD).reshape(1, LANES)
            kg = jnp.tile(attn_k_gain[j].astype(F32), LANES // A_HD).reshape(1, LANES)
            q, k, v = _attn_in(x, vec(norm_mix[i]), w_in[:, :a_q].astype(BF16),
                               w_in[:, a_q:a_q + a_kv].astype(BF16), w_in[:, a_q + a_kv:].astype(BF16),
                               qg, kg, group_mean, tabs, tm, m1, s1, s2)
            att = _attn(attn_sink[j].astype(F32), q, k, v, m1, s1, s2)
            x = _proj_res(att, attn_w_out[j].astype(BF16), x, tm)
            wr = jnp.pad(moe_w_router[j], ((0, 0), (0, LANES - N_EXPERTS))).astype(BF16)
            hn, gate = _router(x, vec(norm_ffn[i]), wr, tm)
            x = _moe(x, hn, gate, moe_w_gate[j].astype(BF16), moe_w_up[j].astype(BF16),
                     moe_w_down[j].astype(BF16), tm_ffn, tf)

    return x[:m1].reshape(b1, s1, D_MODEL), x[m1:].reshape(b2, s2, D_MODEL)
```

```python
import functools
import math

import jax
import jax.numpy as jnp
from jax import lax
from jax.experimental import pallas as pl
from jax.experimental.pallas import tpu as pltpu

F32 = jnp.float32
BF16 = jnp.bfloat16

D_MODEL = 1024
EPS = 1e-6
M_HEADS = 8
M_QK = 64
M_V = 128
M_CHUNK = 128
A_Q_HEADS = 16
A_KV_HEADS = 4
A_HD = 64
A_GROUP = A_Q_HEADS // A_KV_HEADS
A_BLOCK = 128
WINDOW = 128
ROPE_THETA = 500000.0
ROPE_DIM = 16
D_FF = 3584
N_EXPERTS = 8

LANES = 128
VMEM_BYTES_V7X = 64 * 1024 * 1024

_NT = (((1,), (1,)), ((), ()))


def _params(semantics, vmem_mb):
    assert vmem_mb * 1024 * 1024 < VMEM_BYTES_V7X
    return pltpu.CompilerParams(dimension_semantics=semantics,
                                vmem_limit_bytes=vmem_mb * 1024 * 1024)


def _tile(total, pref):
    t = min(pref, total)
    t -= t % LANES
    while total % t:
        t -= LANES
    return t


def _rms(x, g):
    ms = jnp.mean(x * x, axis=-1, keepdims=True)
    return x * lax.rsqrt(ms + EPS) * g


def _sigmoid(x):
    return 1.0 / (1.0 + jnp.exp(-x))


def _dot(a, b):
    return jnp.dot(a, b, preferred_element_type=F32)


def _seq_pos(row0, m1, s1, s2):
    first = row0 < m1
    pos = jnp.where(first, row0 % s1, (row0 - m1) % s2)
    return pos, jnp.where(first, s1, s2)


def _mlstm_in_kernel(x_ref, g_ref, wq_ref, wkT_ref, wv_ref, wo_ref, wgT_ref, bgT_ref,
                     q_ref, kT_ref, v_ref, o_ref, gT_ref):
    hn = _rms(x_ref[...], g_ref[...]).astype(BF16)
    q_ref[...] = (_dot(hn, wq_ref[...]) * (M_QK ** -0.5)).astype(BF16)
    v_ref[...] = _dot(hn, wv_ref[...]).astype(BF16)
    o_ref[...] = _dot(hn, wo_ref[...]).astype(BF16)
    kT = lax.dot_general(wkT_ref[...], hn, _NT, preferred_element_type=F32).astype(BF16)
    gT = lax.dot_general(wgT_ref[...], hn, _NT, preferred_element_type=F32) + bgT_ref[...]
    log_sig = jnp.minimum(gT, 0.0) - jnp.log1p(jnp.exp(-jnp.abs(gT)))
    row = lax.broadcasted_iota(jnp.int32, gT.shape, 0)
    gT = jnp.where(row % 2 == 1, log_sig, gT)
    for j in range(kT_ref.shape[0]):
        kT_ref[j] = kT[:, j * M_CHUNK:(j + 1) * M_CHUNK]
        gT_ref[j] = gT[:, j * M_CHUNK:(j + 1) * M_CHUNK]


def _mlstm_in(x, g, wq, wkT, wv, wo, wgT, bgT, tm):
    m = x.shape[0]
    nq = M_HEADS * M_QK
    nv = M_HEADS * M_V
    ng = 4 * M_HEADS
    row = lambda i: (i, 0)
    fix = lambda i: (0, 0)
    return pl.pallas_call(
        _mlstm_in_kernel,
        grid=(m // tm,),
        in_specs=[
            pl.BlockSpec((tm, D_MODEL), row),
            pl.BlockSpec((1, D_MODEL), fix),
            pl.BlockSpec((D_MODEL, nq), fix),
            pl.BlockSpec((nq, D_MODEL), fix),
            pl.BlockSpec((D_MODEL, nv), fix),
            pl.BlockSpec((D_MODEL, nv), fix),
            pl.BlockSpec((ng, D_MODEL), fix),
            pl.BlockSpec((ng, 1), fix),
        ],
        out_specs=[
            pl.BlockSpec((tm, nq), row),
            pl.BlockSpec((tm // M_CHUNK, nq, M_CHUNK), lambda i: (i, 0, 0)),
            pl.BlockSpec((tm, nv), row),
            pl.BlockSpec((tm, nv), row),
            pl.BlockSpec((tm // M_CHUNK, ng, M_CHUNK), lambda i: (i, 0, 0)),
        ],
        out_shape=[
            jax.ShapeDtypeStruct((m, nq), BF16),
            jax.ShapeDtypeStruct((m // M_CHUNK, nq, M_CHUNK), BF16),
            jax.ShapeDtypeStruct((m, nv), BF16),
            jax.ShapeDtypeStruct((m, nv), BF16),
            jax.ShapeDtypeStruct((m // M_CHUNK, ng, M_CHUNK), F32),
        ],
        compiler_params=_params(("parallel",), 40),
        name="mlstm_in",
    )(x, g, wq, wkT, wv, wo, wgT, bgT)


def _mlstm_unit(q_m, kT, v_aug, ig, lf, c_ref, m_ref, chain, reverse):
    L = M_CHUNK
    lane = lax.broadcasted_iota(jnp.int32, (8, L), 1)
    lf8 = jnp.broadcast_to(lf, (8, L))
    ig8 = jnp.broadcast_to(ig, (8, L))
    b = lf8
    sh = 1
    while sh < L:
        if reverse:
            b = b + jnp.where(lane < L - sh, pltpu.roll(b, L - sh, 1), 0.0)
        else:
            b = b + jnp.where(lane >= sh, pltpu.roll(b, sh, 1), 0.0)
        sh *= 2
    g = jnp.sum(lf8, axis=1, keepdims=True)
    a = g - b + ig8
    a_max = jnp.max(a, axis=1, keepdims=True)
    m_prev = jnp.max(m_ref[chain], axis=1, keepdims=True)
    m_new = jnp.maximum(g + m_prev, a_max)
    w = jnp.exp(a - a_max)[0:1]
    decay = jnp.exp(g + m_prev - m_new)[0:1]
    scale = jnp.exp(a_max - m_new)[0:1]

    b_col = jnp.broadcast_to(b[0:1], (L, L)).T
    r_row = (ig8 - b)[0:1]
    t_idx = lax.broadcasted_iota(jnp.int32, (L, L), 0)
    s_idx = lax.broadcasted_iota(jnp.int32, (L, L), 1)
    keep = (s_idx >= t_idx) if reverse else (s_idx <= t_idx)
    d_mat = jnp.where(keep, b_col + r_row, -jnp.inf)
    inter = jnp.max(b_col, axis=1, keepdims=True) + m_prev[0:1]
    m_t = jnp.maximum(jnp.max(d_mat, axis=1, keepdims=True), inter)
    s = _dot(q_m, kT) * jnp.exp(d_mat - m_t)
    e = jnp.exp(inter - m_t)
    c_prev = c_ref[chain]
    tot = _dot(s.astype(BF16), v_aug) + e * _dot(q_m, c_prev.astype(BF16))
    num = tot[:, :M_V]
    den = tot[:, M_V:]
    h = num / jnp.maximum(jnp.abs(den), jnp.exp(-m_t))

    kTw = (kT.astype(F32) * w).astype(BF16)
    c_ref[chain] = decay * c_prev + scale * _dot(kTw, v_aug)
    m_ref[chain] = jnp.broadcast_to(m_new, (8, L))
    return h


def _mlstm_core_kernel(qf_ref, kTf_ref, vf_ref, gf_ref, qb_ref, kTb_ref, vb_ref, gb_ref,
                       hf_ref, hb_ref, c_ref, m_ref, *, nchunk, rb, m1, s1, s2):
    i = pl.program_id(1)
    nblk = pl.num_programs(1)
    pos_f, _ = _seq_pos(i * rb, m1, s1, s2)
    pos_b, len_b = _seq_pos((nblk - 1 - i) * rb, m1, s1, s2)

    @pl.when(pos_f == 0)
    def _():
        for chain in (0, 2):
            c_ref[chain] = jnp.zeros(c_ref.shape[1:], F32)
            m_ref[chain] = jnp.zeros(m_ref.shape[1:], F32)

    @pl.when(pos_b + rb == len_b)
    def _():
        for chain in (1, 3):
            c_ref[chain] = jnp.zeros(c_ref.shape[1:], F32)
            m_ref[chain] = jnp.zeros(m_ref.shape[1:], F32)

    lane = lax.broadcasted_iota(jnp.int32, (M_CHUNK, LANES), 1)
    head_lanes = (lane < M_QK, lane >= M_QK)
    ones = jnp.ones((M_CHUNK, M_V), BF16)

    def body(c, carry):
        dirs = ((qf_ref, kTf_ref, vf_ref, gf_ref, hf_ref, c),
                (qb_ref, kTb_ref, vb_ref, gb_ref, hb_ref, nchunk - 1 - c))
        for dirn, (q_ref, kT_ref, v_ref, g_ref, out_ref, cc) in enumerate(dirs):
            rows = pl.ds(pl.multiple_of(cc * M_CHUNK, M_CHUNK), M_CHUNK)
            q_pair = q_ref[rows, :]
            kT = kT_ref[cc]
            gates = g_ref[cc]
            for hh in range(2):
                q_m = jnp.where(head_lanes[hh], q_pair, jnp.zeros_like(q_pair))
                v = v_ref[rows, hh * M_V:(hh + 1) * M_V]
                v_aug = jnp.concatenate([v, ones], axis=1)
                r = hh * 4 + dirn * 2
                h = _mlstm_unit(q_m, kT, v_aug, gates[r:r + 1], gates[r + 1:r + 2],
                                c_ref, m_ref, hh * 2 + dirn, reverse=bool(dirn))
                out_ref[rows, hh * M_V:(hh + 1) * M_V] = h
        return carry

    lax.fori_loop(0, nchunk, body, 0)


def _mlstm_core(q, kT, v, gT, rb, m1, s1, s2):
    m = q.shape[0]
    nblk = m // rb
    nchunk = rb // M_CHUNK
    npair = M_HEADS // 2
    fwd = lambda p, i: (i, p)
    bwd = lambda p, i: (nblk - 1 - i, p)
    fwd3 = lambda p, i: (i, p, 0)
    bwd3 = lambda p, i: (nblk - 1 - i, p, 0)

    def specs(im2, im3):
        return [
            pl.BlockSpec((rb, LANES), im2),
            pl.BlockSpec((nchunk, LANES, M_CHUNK), im3),
            pl.BlockSpec((rb, 2 * M_V), im2),
            pl.BlockSpec((nchunk, 8, M_CHUNK), im3),
        ]

    kern = functools.partial(_mlstm_core_kernel, nchunk=nchunk, rb=rb, m1=m1, s1=s1, s2=s2)
    return pl.pallas_call(
        kern,
        grid=(npair, nblk),
        in_specs=specs(fwd, fwd3) + specs(bwd, bwd3),
        out_specs=[pl.BlockSpec((rb, 2 * M_V), fwd), pl.BlockSpec((rb, 2 * M_V), bwd)],
        out_shape=[jax.ShapeDtypeStruct((m, M_HEADS * M_V), F32)] * 2,
        scratch_shapes=[pltpu.VMEM((4, LANES, 2 * M_V), F32), pltpu.VMEM((4, 8, M_CHUNK), F32)],
        compiler_params=_params(("parallel", "arbitrary"), 32),
        name="mlstm_core",
    )(q, kT, v, gT, q, kT, v, gT)


def _mlstm_out_kernel(hf_ref, hb_ref, o_ref, gain_ref, w_ref, x_ref, out_ref):
    h = hf_ref[...] + hb_ref[...]
    parts = []
    for hd in range(M_HEADS):
        hh = h[:, hd * M_V:(hd + 1) * M_V]
        ms = jnp.mean(hh * hh, axis=-1, keepdims=True)
        parts.append(hh * lax.rsqrt(ms + EPS))
    hn = jnp.concatenate(parts, axis=1) * gain_ref[...]
    hg = (_sigmoid(o_ref[...].astype(F32)) * hn).astype(BF16)
    out_ref[...] = x_ref[...] + _dot(hg, w_ref[...])


def _mlstm_out(hf, hb, o, gain, w, x, tm):
    m = x.shape[0]
    row = lambda i: (i, 0)
    fix = lambda i: (0, 0)
    blk = pl.BlockSpec((tm, D_MODEL), row)
    return pl.pallas_call(
        _mlstm_out_kernel,
        grid=(m // tm,),
        in_specs=[blk, blk, blk, pl.BlockSpec((1, D_MODEL), fix),
                  pl.BlockSpec((D_MODEL, D_MODEL), fix), blk],
        out_specs=blk,
        out_shape=jax.ShapeDtypeStruct((m, D_MODEL), F32),
        compiler_params=_params(("parallel",), 40),
        name="mlstm_out",
    )(hf, hb, o, gain, w, x)


def _ffn_kernel(x_ref, g_ref, wg_ref, wu_ref, wd_ref, out_ref, hn_ref, acc_ref):
    f = pl.program_id(1)

    @pl.when(f == 0)
    def _():
        hn_ref[...] = _rms(x_ref[...], g_ref[...]).astype(BF16)
        acc_ref[...] = jnp.zeros_like(acc_ref)

    hn = hn_ref[...]
    gate = _dot(hn, wg_ref[...])
    up = _dot(hn, wu_ref[...])
    act = (gate * _sigmoid(gate) * up).astype(BF16)
    acc_ref[...] += _dot(act, wd_ref[...])

    @pl.when(f == pl.num_programs(1) - 1)
    def _():
        out_ref[...] = x_ref[...] + acc_ref[...]


def _ffn(x, g, wg, wu, wd, tm, tf):
    m = x.shape[0]
    row = lambda i, f: (i, 0)
    return pl.pallas_call(
        _ffn_kernel,
        grid=(m // tm, D_FF // tf),
        in_specs=[
            pl.BlockSpec((tm, D_MODEL), row),
            pl.BlockSpec((1, D_MODEL), lambda i, f: (0, 0)),
            pl.BlockSpec((D_MODEL, tf), lambda i, f: (0, f)),
            pl.BlockSpec((D_MODEL, tf), lambda i, f: (0, f)),
            pl.BlockSpec((tf, D_MODEL), lambda i, f: (f, 0)),
        ],
        out_specs=pl.BlockSpec((tm, D_MODEL), row),
        out_shape=jax.ShapeDtypeStruct((m, D_MODEL), F32),
        scratch_shapes=[pltpu.VMEM((tm, D_MODEL), BF16), pltpu.VMEM((tm, D_MODEL), F32)],
        compiler_params=_params(("parallel", "arbitrary"), 48),
        name="ffn",
    )(x, g, wg, wu, wd)


def _rope_table_kernel(inv_ref, ma_ref, mb_ref, cos_ref, sa_ref, sb_ref):
    rows = cos_ref.shape[0]
    pos = pl.program_id(0) * rows + lax.broadcasted_iota(jnp.int32, (rows, LANES), 0)
    ang = pos.astype(F32) * inv_ref[...]
    sin = jnp.sin(ang)
    cos_ref[...] = jnp.cos(ang)
    sa_ref[...] = sin * ma_ref[...]
    sb_ref[...] = sin * mb_ref[...]


def _rope_tables(smax, rows):
    half = ROPE_DIM // 2
    d = jnp.arange(LANES) % A_HD
    inv = ROPE_THETA ** (-(jnp.arange(half, dtype=F32) * 2.0) / ROPE_DIM)
    inv_lane = jnp.where(d < ROPE_DIM, inv[d % half], 0.0).astype(F32)[None, :]
    ma = jnp.where(d < half, -1.0, 0.0).astype(F32)[None, :]
    mb = jnp.where((d >= half) & (d < ROPE_DIM), 1.0, 0.0).astype(F32)[None, :]
    fix = lambda i: (0, 0)
    vec = pl.BlockSpec((1, LANES), fix)
    tab = pl.BlockSpec((rows, LANES), lambda i: (i, 0))
    return pl.pallas_call(
        _rope_table_kernel,
        grid=(smax // rows,),
        in_specs=[vec, vec, vec],
        out_specs=[tab, tab, tab],
        out_shape=[jax.ShapeDtypeStruct((smax, LANES), F32)] * 3,
        compiler_params=_params(("parallel",), 16),
        name="rope_tables",
    )(inv_lane, ma, mb)


def _attn_in_kernel(x_ref, g_ref, wq_ref, wk_ref, wv_ref, qg_ref, kg_ref, gm_ref,
                    cos_ref, sa_ref, sb_ref, q_ref, k_ref, v_ref):
    hn = _rms(x_ref[...], g_ref[...]).astype(BF16)
    cos = cos_ref[...]
    sa = sa_ref[...]
    sb = sb_ref[...]
    half = ROPE_DIM // 2

    def norm_rope(x, gain, scale, out_ref):
        for j in range(x.shape[1] // LANES):
            xj = x[:, j * LANES:(j + 1) * LANES]
            ms = _dot((xj * xj).astype(BF16), gm_ref[...])
            y = xj * lax.rsqrt(ms + EPS) * gain
            y = y * cos + pltpu.roll(y, LANES - half, 1) * sa + pltpu.roll(y, half, 1) * sb
            out_ref[:, j * LANES:(j + 1) * LANES] = (y * scale).astype(BF16)

    norm_rope(_dot(hn, wq_ref[...]), qg_ref[...], A_HD ** -0.5, q_ref)
    norm_rope(_dot(hn, wk_ref[...]), kg_ref[...], 1.0, k_ref)
    v_ref[...] = _dot(hn, wv_ref[...]).astype(BF16)


def _attn_in(x, g, wq, wk, wv, qg, kg, gm, tabs, tm, m1, s1, s2):
    m = x.shape[0]
    nq = A_Q_HEADS * A_HD
    nkv = A_KV_HEADS * A_HD
    row = lambda i: (i, 0)
    fix = lambda i: (0, 0)

    def tab_map(i):
        pos, _ = _seq_pos(i * tm, m1, s1, s2)
        return (pos // tm, 0)

    tab = pl.BlockSpec((tm, LANES), tab_map)
    vec = pl.BlockSpec((1, LANES), fix)
    return pl.pallas_call(
        _attn_in_kernel,
        grid=(m // tm,),
        in_specs=[
            pl.BlockSpec((tm, D_MODEL), row),
            pl.BlockSpec((1, D_MODEL), fix),
            pl.BlockSpec((D_MODEL, nq), fix),
            pl.BlockSpec((D_MODEL, nkv), fix),
            pl.BlockSpec((D_MODEL, nkv), fix),
            vec, vec,
            pl.BlockSpec((LANES, LANES), fix),
            tab, tab, tab,
        ],
        out_specs=[pl.BlockSpec((tm, nq), row), pl.BlockSpec((tm, nkv), row),
                   pl.BlockSpec((tm, nkv), row)],
        out_shape=[jax.ShapeDtypeStruct((m, nq), BF16), jax.ShapeDtypeStruct((m, nkv), BF16),
                   jax.ShapeDtypeStruct((m, nkv), BF16)],
        compiler_params=_params(("parallel",), 40),
        name="attn_in",
    )(x, g, wq, wk, wv, qg, kg, gm, *tabs)


def _attn_kernel(sink_ref, q_ref, kp_ref, kc_ref, kn_ref, vp_ref, vc_ref, vn_ref, out_ref,
                 *, m1, s1, s2):
    blk = A_BLOCK
    pos0, slen = _seq_pos(pl.program_id(0) * blk, m1, s1, s2)
    prev_ok = pos0 > 0
    next_ok = pos0 + blk < slen
    t = lax.broadcasted_iota(jnp.int32, (blk, 3 * blk), 0)
    c = lax.broadcasted_iota(jnp.int32, (blk, 3 * blk), 1)
    valid = (jnp.abs(c - blk - t) <= WINDOW) & ((c >= blk) | prev_ok) & ((c < 2 * blk) | next_ok)

    kcat = jnp.concatenate([kp_ref[...], kc_ref[...], kn_ref[...]], axis=0).astype(F32)
    vcat = jnp.concatenate([vp_ref[...], vc_ref[...], vn_ref[...]], axis=0).astype(F32)
    lane_kv = lax.broadcasted_iota(jnp.int32, (3 * blk, LANES), 1)
    lane_q = lax.broadcasted_iota(jnp.int32, (blk, LANES), 1)

    def softmax_parts(s, sink):
        s = jnp.where(valid, s, -jnp.inf)
        m = jnp.maximum(jnp.max(s, axis=1, keepdims=True), sink)
        p = jnp.exp(s - m)
        denom = jnp.sum(p, axis=1, keepdims=True) + jnp.exp(sink - m)
        return p.astype(BF16), denom

    for h in range(A_KV_HEADS):
        tile, half = divmod(h, 2)
        own = (lane_kv < A_HD) if half == 0 else (lane_kv >= A_HD)

        def lo_hi(cat):
            mine = jnp.where(own, cat[:, tile * LANES:(tile + 1) * LANES], 0.0)
            other = pltpu.roll(mine, A_HD, 1)
            lo, hi = (mine, other) if half == 0 else (other, mine)
            return lo.astype(BF16), hi.astype(BF16)

        k_lo, k_hi = lo_hi(kcat)
        v_lo, v_hi = lo_hi(vcat)
        for j in range(A_GROUP // 2):
            grp = h * (A_GROUP // 2) + j
            q_pair = q_ref[:, grp * LANES:(grp + 1) * LANES]
            s_even = lax.dot_general(q_pair, k_lo, _NT, preferred_element_type=F32)
            s_odd = lax.dot_general(q_pair, k_hi, _NT, preferred_element_type=F32)
            p_even, d_even = softmax_parts(s_even, sink_ref[2 * grp])
            p_odd, d_odd = softmax_parts(s_odd, sink_ref[2 * grp + 1])
            o = _dot(p_even, v_lo) + _dot(p_odd, v_hi)
            o = o * jnp.where(lane_q < A_HD, 1.0 / d_even, 1.0 / d_odd)
            out_ref[:, grp * LANES:(grp + 1) * LANES] = o.astype(BF16)


def _attn(sink, q, k, v, m1, s1, s2):
    m = q.shape[0]
    blk = A_BLOCK
    nblk = m // blk
    nkv = A_KV_HEADS * A_HD
    cur = lambda i: (i, 0)
    prev = lambda i: (jnp.maximum(i - 1, 0), 0)
    nxt = lambda i: (jnp.minimum(i + 1, nblk - 1), 0)
    kv = lambda im: pl.BlockSpec((blk, nkv), im)
    kern = functools.partial(_attn_kernel, m1=m1, s1=s1, s2=s2)
    return pl.pallas_call(
        kern,
        grid=(nblk,),
        in_specs=[pl.BlockSpec(memory_space=pltpu.SMEM),
                  pl.BlockSpec((blk, D_MODEL), cur), kv(prev), kv(cur), kv(nxt),
                  kv(prev), kv(cur), kv(nxt)],
        out_specs=pl.BlockSpec((blk, D_MODEL), cur),
        out_shape=jax.ShapeDtypeStruct((m, D_MODEL), BF16),
        compiler_params=_params(("parallel",), 32),
        name="attn",
    )(sink, q, k, k, k, v, v, v)


def _proj_res_kernel(a_ref, w_ref, x_ref, out_ref):
    out_ref[...] = x_ref[...] + _dot(a_ref[...], w_ref[...])


def _proj_res(a, w, x, tm):
    m = x.shape[0]
    row = lambda i: (i, 0)
    blk = pl.BlockSpec((tm, D_MODEL), row)
    return pl.pallas_call(
        _proj_res_kernel,
        grid=(m // tm,),
        in_specs=[blk, pl.BlockSpec((D_MODEL, D_MODEL), lambda i: (0, 0)), blk],
        out_specs=blk,
        out_shape=jax.ShapeDtypeStruct((m, D_MODEL), F32),
        compiler_params=_params(("parallel",), 32),
        name="attn_out",
    )(a, w, x)


_R_E1, _R_E2, _R_P1, _R_P2, _R_RANK1, _R_RANK2 = range(6)


def _router_kernel(x_ref, g_ref, wr_ref, route_ref, count_ref, carry_ref):
    @pl.when(pl.program_id(0) == 0)
    def _():
        carry_ref[...] = jnp.zeros_like(carry_ref)

    hn = _rms(x_ref[...], g_ref[...]).astype(BF16)
    logits = _dot(hn, wr_ref[...])
    tm = logits.shape[0]
    lane = lax.broadcasted_iota(jnp.int32, logits.shape, 1).astype(F32)
    logits = jnp.where(lane < N_EXPERTS, logits, -jnp.inf)
    v1 = jnp.max(logits, axis=1, keepdims=True)
    i1 = jnp.min(jnp.where(logits == v1, lane, float(LANES)), axis=1, keepdims=True)
    rest = jnp.where(lane == i1, -jnp.inf, logits)
    v2 = jnp.max(rest, axis=1, keepdims=True)
    i2 = jnp.min(jnp.where(rest == v2, lane, float(LANES)), axis=1, keepdims=True)
    e2 = jnp.exp(v2 - v1)
    p1 = 1.0 / (1.0 + e2)
    p2 = e2 * p1

    sel = jnp.where((lane == i1) | (lane == i2), 1.0, 0.0)
    t_row = lax.broadcasted_iota(jnp.int32, (tm, tm), 0)
    t_col = lax.broadcasted_iota(jnp.int32, (tm, tm), 1)
    earlier = jnp.where(t_col < t_row, 1.0, 0.0).astype(BF16)
    before = _dot(earlier, sel.astype(BF16)) + carry_ref[0:1, :]
    rank1 = jnp.sum(jnp.where(lane == i1, before, 0.0), axis=1, keepdims=True)
    rank2 = jnp.sum(jnp.where(lane == i2, before, 0.0), axis=1, keepdims=True)
    total = carry_ref[0:1, :] + jnp.sum(sel, axis=0, keepdims=True)
    carry_ref[...] = jnp.broadcast_to(total, carry_ref.shape)
    count_ref[...] = jnp.broadcast_to(total, count_ref.shape)

    route = jnp.zeros_like(logits)
    for col, val in ((_R_E1, i1), (_R_E2, i2), (_R_P1, p1), (_R_P2, p2),
                     (_R_RANK1, rank1), (_R_RANK2, rank2)):
        route = jnp.where(lane == float(col), val, route)
    route_ref[...] = route


def _router(x, g, wr, tm):
    m = x.shape[0]
    row = lambda i: (i, 0)
    fix = lambda i: (0, 0)
    return pl.pallas_call(
        _router_kernel,
        grid=(m // tm,),
        in_specs=[pl.BlockSpec((tm, D_MODEL), row), pl.BlockSpec((1, D_MODEL), fix),
                  pl.BlockSpec((D_MODEL, LANES), fix)],
        out_specs=[pl.BlockSpec((tm, LANES), row), pl.BlockSpec((8, LANES), fix)],
        out_shape=[jax.ShapeDtypeStruct((m, LANES), F32), jax.ShapeDtypeStruct((8, LANES), F32)],
        scratch_shapes=[pltpu.VMEM((8, LANES), F32)],
        compiler_params=_params(("arbitrary",), 32),
        name="router",
    )(x, g, wr)


def _row_copies(n, src_of, dst_of, sem):
    unroll = 8

    def body(jj, carry):
        for u in range(unroll):
            j = jj * unroll + u
            pltpu.make_async_copy(src_of(j), dst_of(j), sem).start()
        return carry

    lax.fori_loop(0, n // unroll, body, 0)


def _dispatch_kernel(pos1_ref, pos2_ref, x_hbm, zeros_hbm, xs_hbm, sem):
    del zeros_hbm
    td = pos1_ref.shape[-1]
    base = pl.program_id(0) * td
    for pos_ref in (pos1_ref, pos2_ref):
        _row_copies(td, lambda j: x_hbm.at[pl.ds(base + j, 1)],
                    lambda j: xs_hbm.at[pl.ds(pos_ref[0, 0, j], 1)], sem)
    pltpu.make_async_copy(x_hbm.at[pl.ds(0, 2 * td)], xs_hbm.at[pl.ds(0, 2 * td)], sem).wait()


def _dispatch(pos1, pos2, x, rows, td):
    m = x.shape[0]
    idx = pl.BlockSpec((1, 1, td), lambda i: (i, 0, 0), memory_space=pltpu.SMEM)
    any_spec = pl.BlockSpec(memory_space=pl.ANY)
    return pl.pallas_call(
        _dispatch_kernel,
        grid=(m // td,),
        in_specs=[idx, idx, any_spec, any_spec],
        out_specs=any_spec,
        out_shape=jax.ShapeDtypeStruct((rows, D_MODEL), F32),
        scratch_shapes=[pltpu.SemaphoreType.DMA(())],
        input_output_aliases={3: 0},
        compiler_params=_params(("arbitrary",), 16),
        name="moe_dispatch",
    )(pos1.reshape(m // td, 1, td), pos2.reshape(m // td, 1, td), x,
      jnp.zeros((rows, D_MODEL), F32))


def _experts_kernel(te_ref, nused_ref, xs_ref, g_ref, wg_ref, wu_ref, wd_ref, ys_ref, hn_ref, acc_ref):
    del te_ref
    t = pl.program_id(0)
    f = pl.program_id(1)

    @pl.when(t < nused_ref[0])
    def _():
        @pl.when(f == 0)
        def _():
            hn_ref[...] = _rms(xs_ref[...], g_ref[...]).astype(BF16)
            acc_ref[...] = jnp.zeros_like(acc_ref)

        hn = hn_ref[...]
        gate = _dot(hn, wg_ref[...])
        up = _dot(hn, wu_ref[...])
        act = (gate * _sigmoid(gate) * up).astype(BF16)
        acc_ref[...] += _dot(act, wd_ref[...])

        @pl.when(f == pl.num_programs(1) - 1)
        def _():
            ys_ref[...] = acc_ref[...]

    @pl.when((t >= nused_ref[0]) & (f == pl.num_programs(1) - 1))
    def _():
        ys_ref[...] = jnp.zeros_like(ys_ref)


def _experts(tile_expert, nused, xs, g, wg, wu, wd, tm, tf):
    rows = xs.shape[0]
    row = lambda t, f, te, nu: (jnp.minimum(t, nu[0] - 1), 0)
    out_row = lambda t, f, te, nu: (t, 0)
    return pl.pallas_call(
        _experts_kernel,
        grid_spec=pltpu.PrefetchScalarGridSpec(
            num_scalar_prefetch=2,
            grid=(rows // tm, D_FF // tf),
            in_specs=[
                pl.BlockSpec((tm, D_MODEL), row),
                pl.BlockSpec((1, D_MODEL), lambda t, f, te, nu: (0, 0)),
                pl.BlockSpec((None, D_MODEL, tf), lambda t, f, te, nu: (te[t], 0, f)),
                pl.BlockSpec((None, D_MODEL, tf), lambda t, f, te, nu: (te[t], 0, f)),
                pl.BlockSpec((None, tf, D_MODEL), lambda t, f, te, nu: (te[t], f, 0)),
            ],
            out_specs=pl.BlockSpec((tm, D_MODEL), out_row),
            scratch_shapes=[pltpu.VMEM((tm, D_MODEL), BF16), pltpu.VMEM((tm, D_MODEL), F32)],
        ),
        out_shape=jax.ShapeDtypeStruct((rows, D_MODEL), F32),
        compiler_params=_params(("arbitrary", "arbitrary"), 48),
        name="moe_experts",
    )(tile_expert, nused, xs, g, wg, wu, wd)


def _combine_kernel(pos1_ref, pos2_ref, x_ref, route_ref, ys_hbm, out_ref, y1_ref, y2_ref, sem):
    tc = x_ref.shape[0]
    for pos_ref, y_ref in ((pos1_ref, y1_ref), (pos2_ref, y2_ref)):
        _row_copies(tc, lambda j: ys_hbm.at[pl.ds(pos_ref[0, 0, j], 1)],
                    lambda j: y_ref.at[pl.ds(j, 1)], sem)
    for y_ref in (y1_ref, y2_ref):
        pltpu.make_async_copy(ys_hbm.at[pl.ds(0, tc)], y_ref, sem).wait()
    route = route_ref[...]
    p1 = route[:, _R_P1:_R_P1 + 1]
    p2 = route[:, _R_P2:_R_P2 + 1]
    out_ref[...] = x_ref[...] + (p1 * y1_ref[...] + p2 * y2_ref[...])


def _combine(pos1, pos2, x, route, ys, tc):
    m = x.shape[0]
    row = lambda i: (i, 0)
    idx = pl.BlockSpec((1, 1, tc), lambda i: (i, 0, 0), memory_space=pltpu.SMEM)
    return pl.pallas_call(
        _combine_kernel,
        grid=(m // tc,),
        in_specs=[idx, idx, pl.BlockSpec((tc, D_MODEL), row), pl.BlockSpec((tc, LANES), row),
                  pl.BlockSpec(memory_space=pl.ANY)],
        out_specs=pl.BlockSpec((tc, D_MODEL), row),
        out_shape=jax.ShapeDtypeStruct((m, D_MODEL), F32),
        scratch_shapes=[pltpu.VMEM((tc, D_MODEL), F32), pltpu.VMEM((tc, D_MODEL), F32),
                        pltpu.SemaphoreType.DMA(())],
        compiler_params=_params(("arbitrary",), 32),
        name="moe_combine",
    )(pos1.reshape(m // tc, 1, tc), pos2.reshape(m // tc, 1, tc), x, route, ys)


def _moe(x, g, wr, wg, wu, wd, tm_route, tm, tf):
    m = x.shape[0]
    route, counts = _router(x, g, wr, tm_route)
    e1 = route[:, _R_E1].astype(jnp.int32)
    e2 = route[:, _R_E2].astype(jnp.int32)
    n_e = counts[0, :N_EXPERTS].astype(jnp.int32)
    tiles_e = (n_e + tm - 1) // tm
    tile_end = jnp.cumsum(tiles_e)
    row_start = (tile_end - tiles_e) * tm
    pos1 = row_start[e1] + route[:, _R_RANK1].astype(jnp.int32)
    pos2 = row_start[e2] + route[:, _R_RANK2].astype(jnp.int32)
    ntile = 2 * m // tm + N_EXPERTS
    nused = tile_end[-1:]
    tile_ids = jnp.minimum(jnp.arange(ntile, dtype=jnp.int32), nused - 1)
    tile_expert = jnp.sum(tile_ids[:, None] >= tile_end[None, :], axis=1).astype(jnp.int32)
    xs = _dispatch(pos1, pos2, x, ntile * tm, tm_route)
    ys = _experts(tile_expert, nused.astype(jnp.int32), xs, g, wg, wu, wd, tm, tf)
    return _combine(pos1, pos2, x, route, ys, tm_route)


def _mlstm_gate_perm():
    perm = []
    for pair in range(M_HEADS // 2):
        for hh in range(2):
            for gtype in range(4):
                perm.append(gtype * M_HEADS + 2 * pair + hh)
    return jnp.array(perm, jnp.int32)


def kernel(x_prompt, x_sample, norm_mix, norm_ffn, mlstm_w_in, mlstm_gate_bias, mlstm_head_gain,
           mlstm_w_out, attn_w_in, attn_q_gain, attn_k_gain, attn_sink, attn_w_out, ffn_w_gate,
           ffn_w_up, ffn_w_down, moe_w_router, moe_w_gate, moe_w_up, moe_w_down):
    b1, s1, _ = x_prompt.shape
    b2, s2, _ = x_sample.shape
    m1 = b1 * s1
    x = jnp.concatenate([x_prompt.reshape(m1, D_MODEL), x_sample.reshape(b2 * s2, D_MODEL)], axis=0)
    m = x.shape[0]
    seq_gcd = math.gcd(s1, s2)
    tm = _tile(seq_gcd, 512)
    rb = _tile(seq_gcd, 1024)
    tm_ffn = _tile(m, 1024)
    tm_moe = _tile(2 * m, 512)
    tf = 512
    depth = norm_mix.shape[0]
    vec = lambda a: a.astype(F32).reshape(1, -1)
    nq = M_HEADS * M_QK
    nv = M_HEADS * M_V

    tabs = _rope_tables(max(s1, s2), tm)
    group_mean = jnp.where((jnp.arange(LANES)[:, None] // A_HD) == (jnp.arange(LANES)[None, :] // A_HD),
                           1.0 / A_HD, 0.0).astype(BF16)
    gate_perm = _mlstm_gate_perm()

    for i in range(depth):
        j = i // 2
        if i % 2 == 0:
            w_in = mlstm_w_in[j]
            wq = w_in[:, :nq].astype(BF16)
            wkT = w_in[:, nq:2 * nq].T.astype(BF16)
            wv = w_in[:, 2 * nq:2 * nq + nv].astype(BF16)
            wo = w_in[:, 2 * nq + nv:2 * nq + 2 * nv].astype(BF16)
            wgT = w_in[:, 2 * nq + 2 * nv:][:, gate_perm].T.astype(BF16)
            bgT = mlstm_gate_bias[j].astype(F32)[gate_perm].reshape(-1, 1)
            q, kT, v, o, gT = _mlstm_in(x, vec(norm_mix[i]), wq, wkT, wv, wo, wgT, bgT, tm)
            hf, hb = _mlstm_core(q, kT, v, gT, rb, m1, s1, s2)
            x = _mlstm_out(hf, hb, o, vec(mlstm_head_gain[j]), mlstm_w_out[j].astype(BF16), x, tm)
            x = _ffn(x, vec(norm_ffn[i]), ffn_w_gate[j].astype(BF16), ffn_w_up[j].astype(BF16),
                     ffn_w_down[j].astype(BF16), tm_ffn, tf)
        else:
            w_in = attn_w_in[j]
            a_q = A_Q_HEADS * A_HD
            a_kv = A_KV_HEADS * A_HD
            qg = jnp.tile(attn_q_gain[j].astype(F32), LANES // A_HD).reshape(1, LANES)
            kg = jnp.tile(attn_k_gain[j].astype(F32), LANES // A_HD).reshape(1, LANES)
            q, k, v = _attn_in(x, vec(norm_mix[i]), w_in[:, :a_q].astype(BF16),
                               w_in[:, a_q:a_q + a_kv].astype(BF16), w_in[:, a_q + a_kv:].astype(BF16),
                               qg, kg, group_mean, tabs, tm, m1, s1, s2)
            att = _attn(attn_sink[j].astype(F32), q, k, v, m1, s1, s2)
            x = _proj_res(att, attn_w_out[j].astype(BF16), x, tm)
            wr = jnp.pad(moe_w_router[j], ((0, 0), (0, LANES - N_EXPERTS))).astype(BF16)
            x = _moe(x, vec(norm_ffn[i]), wr, moe_w_gate[j].astype(BF16), moe_w_up[j].astype(BF16),
                     moe_w_down[j].astype(BF16), tm, tm_moe, tf)

    return x[:m1].reshape(b1, s1, D_MODEL), x[m1:].reshape(b2, s2, D_MODEL)
```

```python
import functools
import math

import jax
import jax.numpy as jnp
from jax import lax
from jax.experimental import pallas as pl
from jax.experimental.pallas import tpu as pltpu

F32 = jnp.float32
BF16 = jnp.bfloat16

D_MODEL = 1024
EPS = 1e-6
M_HEADS = 8
M_QK = 64
M_V = 128
M_CHUNK = 128
A_Q_HEADS = 16
A_KV_HEADS = 4
A_HD = 64
A_GROUP = A_Q_HEADS // A_KV_HEADS
A_BLOCK = 128
WINDOW = 128
ROPE_THETA = 500000.0
ROPE_DIM = 16
D_FF = 3584
N_EXPERTS = 8

LANES = 128
VMEM_BYTES_V7X = 64 * 1024 * 1024

_NT = (((1,), (1,)), ((), ()))


def _params(semantics, vmem_mb):
    assert vmem_mb * 1024 * 1024 < VMEM_BYTES_V7X
    return pltpu.CompilerParams(dimension_semantics=semantics,
                                vmem_limit_bytes=vmem_mb * 1024 * 1024)


def _tile(total, pref):
    t = min(pref, total)
    t -= t % LANES
    while total % t:
        t -= LANES
    return t


def _rms(x, g):
    ms = jnp.mean(x * x, axis=-1, keepdims=True)
    return x * lax.rsqrt(ms + EPS) * g


def _sigmoid(x):
    return 1.0 / (1.0 + jnp.exp(-x))


def _dot(a, b):
    return jnp.dot(a, b, preferred_element_type=F32)


def _seq_pos(row0, m1, s1, s2):
    first = row0 < m1
    pos = jnp.where(first, row0 % s1, (row0 - m1) % s2)
    return pos, jnp.where(first, s1, s2)


def _mlstm_in_kernel(x_ref, g_ref, wq_ref, wkT_ref, wv_ref, wo_ref, wgT_ref, bgT_ref,
                     q_ref, kT_ref, v_ref, o_ref, gT_ref):
    hn = _rms(x_ref[...], g_ref[...]).astype(BF16)
    q_ref[...] = (_dot(hn, wq_ref[...]) * (M_QK ** -0.5)).astype(BF16)
    v_ref[...] = _dot(hn, wv_ref[...]).astype(BF16)
    o_ref[...] = _dot(hn, wo_ref[...]).astype(BF16)
    kT = lax.dot_general(wkT_ref[...], hn, _NT, preferred_element_type=F32).astype(BF16)
    gT = lax.dot_general(wgT_ref[...], hn, _NT, preferred_element_type=F32) + bgT_ref[...]
    log_sig = jnp.minimum(gT, 0.0) - jnp.log1p(jnp.exp(-jnp.abs(gT)))
    row = lax.broadcasted_iota(jnp.int32, gT.shape, 0)
    gT = jnp.where(row % 2 == 1, log_sig, gT)
    for j in range(kT_ref.shape[0]):
        kT_ref[j] = kT[:, j * M_CHUNK:(j + 1) * M_CHUNK]
        gT_ref[j] = gT[:, j * M_CHUNK:(j + 1) * M_CHUNK]


def _mlstm_in(x, g, wq, wkT, wv, wo, wgT, bgT, tm):
    m = x.shape[0]
    nq = M_HEADS * M_QK
    nv = M_HEADS * M_V
    ng = 4 * M_HEADS
    row = lambda i: (i, 0)
    fix = lambda i: (0, 0)
    return pl.pallas_call(
        _mlstm_in_kernel,
        grid=(m // tm,),
        in_specs=[
            pl.BlockSpec((tm, D_MODEL), row),
            pl.BlockSpec((1, D_MODEL), fix),
            pl.BlockSpec((D_MODEL, nq), fix),
            pl.BlockSpec((nq, D_MODEL), fix),
            pl.BlockSpec((D_MODEL, nv), fix),
            pl.BlockSpec((D_MODEL, nv), fix),
            pl.BlockSpec((ng, D_MODEL), fix),
            pl.BlockSpec((ng, 1), fix),
        ],
        out_specs=[
            pl.BlockSpec((tm, nq), row),
            pl.BlockSpec((tm // M_CHUNK, nq, M_CHUNK), lambda i: (i, 0, 0)),
            pl.BlockSpec((tm, nv), row),
            pl.BlockSpec((tm, nv), row),
            pl.BlockSpec((tm // M_CHUNK, ng, M_CHUNK), lambda i: (i, 0, 0)),
        ],
        out_shape=[
            jax.ShapeDtypeStruct((m, nq), BF16),
            jax.ShapeDtypeStruct((m // M_CHUNK, nq, M_CHUNK), BF16),
            jax.ShapeDtypeStruct((m, nv), BF16),
            jax.ShapeDtypeStruct((m, nv), BF16),
            jax.ShapeDtypeStruct((m // M_CHUNK, ng, M_CHUNK), F32),
        ],
        compiler_params=_params(("parallel",), 40),
        name="mlstm_in",
    )(x, g, wq, wkT, wv, wo, wgT, bgT)


def _mlstm_unit(q_m, kT, v_aug, ig, lf, c_ref, m_ref, chain, reverse):
    L = M_CHUNK
    lane = lax.broadcasted_iota(jnp.int32, (8, L), 1)
    lf8 = jnp.broadcast_to(lf, (8, L))
    ig8 = jnp.broadcast_to(ig, (8, L))
    b = lf8
    sh = 1
    while sh < L:
        if reverse:
            b = b + jnp.where(lane < L - sh, pltpu.roll(b, L - sh, 1), 0.0)
        else:
            b = b + jnp.where(lane >= sh, pltpu.roll(b, sh, 1), 0.0)
        sh *= 2
    g = jnp.sum(lf8, axis=1, keepdims=True)
    a = g - b + ig8
    a_max = jnp.max(a, axis=1, keepdims=True)
    m_prev = jnp.max(m_ref[chain], axis=1, keepdims=True)
    m_new = jnp.maximum(g + m_prev, a_max)
    w = jnp.exp(a - a_max)[0:1]
    decay = jnp.exp(g + m_prev - m_new)[0:1]
    scale = jnp.exp(a_max - m_new)[0:1]

    b_col = jnp.broadcast_to(b[0:1], (L, L)).T
    r_row = (ig8 - b)[0:1]
    t_idx = lax.broadcasted_iota(jnp.int32, (L, L), 0)
    s_idx = lax.broadcasted_iota(jnp.int32, (L, L), 1)
    keep = (s_idx >= t_idx) if reverse else (s_idx <= t_idx)
    d_mat = jnp.where(keep, b_col + r_row, -jnp.inf)
    inter = jnp.max(b_col, axis=1, keepdims=True) + m_prev[0:1]
    m_t = jnp.maximum(jnp.max(d_mat, axis=1, keepdims=True), inter)
    s = _dot(q_m, kT) * jnp.exp(d_mat - m_t)
    e = jnp.exp(inter - m_t)
    c_prev = c_ref[chain]
    tot = _dot(s.astype(BF16), v_aug) + e * _dot(q_m, c_prev.astype(BF16))
    num = tot[:, :M_V]
    den = tot[:, M_V:]
    h = num / jnp.maximum(jnp.abs(den), jnp.exp(-m_t))

    kTw = (kT.astype(F32) * w).astype(BF16)
    c_ref[chain] = decay * c_prev + scale * _dot(kTw, v_aug)
    m_ref[chain] = jnp.broadcast_to(m_new, (8, L))
    return h


def _mlstm_core_kernel(qf_ref, kTf_ref, vf_ref, gf_ref, qb_ref, kTb_ref, vb_ref, gb_ref,
                       hf_ref, hb_ref, c_ref, m_ref, *, nchunk, rb, m1, s1, s2):
    i = pl.program_id(1)
    nblk = pl.num_programs(1)
    pos_f, _ = _seq_pos(i * rb, m1, s1, s2)
    pos_b, len_b = _seq_pos((nblk - 1 - i) * rb, m1, s1, s2)

    @pl.when(pos_f == 0)
    def _():
        for chain in (0, 2):
            c_ref[chain] = jnp.zeros(c_ref.shape[1:], F32)
            m_ref[chain] = jnp.zeros(m_ref.shape[1:], F32)

    @pl.when(pos_b + rb == len_b)
    def _():
        for chain in (1, 3):
            c_ref[chain] = jnp.zeros(c_ref.shape[1:], F32)
            m_ref[chain] = jnp.zeros(m_ref.shape[1:], F32)

    lane = lax.broadcasted_iota(jnp.int32, (M_CHUNK, LANES), 1)
    head_lanes = (lane < M_QK, lane >= M_QK)
    ones = jnp.ones((M_CHUNK, M_V), BF16)

    def body(c, carry):
        dirs = ((qf_ref, kTf_ref, vf_ref, gf_ref, hf_ref, c),
                (qb_ref, kTb_ref, vb_ref, gb_ref, hb_ref, nchunk - 1 - c))
        for dirn, (q_ref, kT_ref, v_ref, g_ref, out_ref, cc) in enumerate(dirs):
            rows = pl.ds(pl.multiple_of(cc * M_CHUNK, M_CHUNK), M_CHUNK)
            q_pair = q_ref[rows, :]
            kT = kT_ref[cc]
            gates = g_ref[cc]
            for hh in range(2):
                q_m = jnp.where(head_lanes[hh], q_pair, jnp.zeros_like(q_pair))
                v = v_ref[rows, hh * M_V:(hh + 1) * M_V]
                v_aug = jnp.concatenate([v, ones], axis=1)
                r = hh * 4 + dirn * 2
                h = _mlstm_unit(q_m, kT, v_aug, gates[r:r + 1], gates[r + 1:r + 2],
                                c_ref, m_ref, hh * 2 + dirn, reverse=bool(dirn))
                out_ref[rows, hh * M_V:(hh + 1) * M_V] = h
        return carry

    lax.fori_loop(0, nchunk, body, 0)


def _mlstm_core(q, kT, v, gT, rb, m1, s1, s2):
    m = q.shape[0]
    nblk = m // rb
    nchunk = rb // M_CHUNK
    npair = M_HEADS // 2
    fwd = lambda p, i: (i, p)
    bwd = lambda p, i: (nblk - 1 - i, p)
    fwd3 = lambda p, i: (i, p, 0)
    bwd3 = lambda p, i: (nblk - 1 - i, p, 0)

    def specs(im2, im3):
        return [
            pl.BlockSpec((rb, LANES), im2),
            pl.BlockSpec((nchunk, LANES, M_CHUNK), im3),
            pl.BlockSpec((rb, 2 * M_V), im2),
            pl.BlockSpec((nchunk, 8, M_CHUNK), im3),
        ]

    kern = functools.partial(_mlstm_core_kernel, nchunk=nchunk, rb=rb, m1=m1, s1=s1, s2=s2)
    return pl.pallas_call(
        kern,
        grid=(npair, nblk),
        in_specs=specs(fwd, fwd3) + specs(bwd, bwd3),
        out_specs=[pl.BlockSpec((rb, 2 * M_V), fwd), pl.BlockSpec((rb, 2 * M_V), bwd)],
        out_shape=[jax.ShapeDtypeStruct((m, M_HEADS * M_V), F32)] * 2,
        scratch_shapes=[pltpu.VMEM((4, LANES, 2 * M_V), F32), pltpu.VMEM((4, 8, M_CHUNK), F32)],
        compiler_params=_params(("parallel", "arbitrary"), 32),
        name="mlstm_core",
    )(q, kT, v, gT, q, kT, v, gT)


def _mlstm_out_kernel(hf_ref, hb_ref, o_ref, gain_ref, w_ref, x_ref, out_ref):
    h = hf_ref[...] + hb_ref[...]
    parts = []
    for hd in range(M_HEADS):
        hh = h[:, hd * M_V:(hd + 1) * M_V]
        ms = jnp.mean(hh * hh, axis=-1, keepdims=True)
        parts.append(hh * lax.rsqrt(ms + EPS))
    hn = jnp.concatenate(parts, axis=1) * gain_ref[...]
    hg = (_sigmoid(o_ref[...].astype(F32)) * hn).astype(BF16)
    out_ref[...] = x_ref[...] + _dot(hg, w_ref[...])


def _mlstm_out(hf, hb, o, gain, w, x, tm):
    m = x.shape[0]
    row = lambda i: (i, 0)
    fix = lambda i: (0, 0)
    blk = pl.BlockSpec((tm, D_MODEL), row)
    return pl.pallas_call(
        _mlstm_out_kernel,
        grid=(m // tm,),
        in_specs=[blk, blk, blk, pl.BlockSpec((1, D_MODEL), fix),
                  pl.BlockSpec((D_MODEL, D_MODEL), fix), blk],
        out_specs=blk,
        out_shape=jax.ShapeDtypeStruct((m, D_MODEL), F32),
        compiler_params=_params(("parallel",), 40),
        name="mlstm_out",
    )(hf, hb, o, gain, w, x)


def _ffn_kernel(x_ref, g_ref, wg_ref, wu_ref, wd_ref, out_ref, hn_ref, acc_ref):
    f = pl.program_id(1)

    @pl.when(f == 0)
    def _():
        hn_ref[...] = _rms(x_ref[...], g_ref[...]).astype(BF16)
        acc_ref[...] = jnp.zeros_like(acc_ref)

    hn = hn_ref[...]
    gate = _dot(hn, wg_ref[...])
    up = _dot(hn, wu_ref[...])
    act = (gate * _sigmoid(gate) * up).astype(BF16)
    acc_ref[...] += _dot(act, wd_ref[...])

    @pl.when(f == pl.num_programs(1) - 1)
    def _():
        out_ref[...] = x_ref[...] + acc_ref[...]


def _ffn(x, g, wg, wu, wd, tm, tf):
    m = x.shape[0]
    row = lambda i, f: (i, 0)
    return pl.pallas_call(
        _ffn_kernel,
        grid=(m // tm, D_FF // tf),
        in_specs=[
            pl.BlockSpec((tm, D_MODEL), row),
            pl.BlockSpec((1, D_MODEL), lambda i, f: (0, 0)),
            pl.BlockSpec((D_MODEL, tf), lambda i, f: (0, f)),
            pl.BlockSpec((D_MODEL, tf), lambda i, f: (0, f)),
            pl.BlockSpec((tf, D_MODEL), lambda i, f: (f, 0)),
        ],
        out_specs=pl.BlockSpec((tm, D_MODEL), row),
        out_shape=jax.ShapeDtypeStruct((m, D_MODEL), F32),
        scratch_shapes=[pltpu.VMEM((tm, D_MODEL), BF16), pltpu.VMEM((tm, D_MODEL), F32)],
        compiler_params=_params(("parallel", "arbitrary"), 48),
        name="ffn",
    )(x, g, wg, wu, wd)


def _rope_table_kernel(inv_ref, ma_ref, mb_ref, cos_ref, sa_ref, sb_ref):
    rows = cos_ref.shape[0]
    pos = pl.program_id(0) * rows + lax.broadcasted_iota(jnp.int32, (rows, LANES), 0)
    ang = pos.astype(F32) * inv_ref[...]
    sin = jnp.sin(ang)
    cos_ref[...] = jnp.cos(ang)
    sa_ref[...] = sin * ma_ref[...]
    sb_ref[...] = sin * mb_ref[...]


def _rope_tables(smax, rows):
    half = ROPE_DIM // 2
    d = jnp.arange(LANES) % A_HD
    inv = ROPE_THETA ** (-(jnp.arange(half, dtype=F32) * 2.0) / ROPE_DIM)
    inv_lane = jnp.where(d < ROPE_DIM, inv[d % half], 0.0).astype(F32)[None, :]
    ma = jnp.where(d < half, -1.0, 0.0).astype(F32)[None, :]
    mb = jnp.where((d >= half) & (d < ROPE_DIM), 1.0, 0.0).astype(F32)[None, :]
    fix = lambda i: (0, 0)
    vec = pl.BlockSpec((1, LANES), fix)
    tab = pl.BlockSpec((rows, LANES), lambda i: (i, 0))
    return pl.pallas_call(
        _rope_table_kernel,
        grid=(smax // rows,),
        in_specs=[vec, vec, vec],
        out_specs=[tab, tab, tab],
        out_shape=[jax.ShapeDtypeStruct((smax, LANES), F32)] * 3,
        compiler_params=_params(("parallel",), 16),
        name="rope_tables",
    )(inv_lane, ma, mb)


def _attn_in_kernel(x_ref, g_ref, wq_ref, wk_ref, wv_ref, qg_ref, kg_ref, gm_ref,
                    cos_ref, sa_ref, sb_ref, q_ref, k_ref, v_ref):
    hn = _rms(x_ref[...], g_ref[...]).astype(BF16)
    cos = cos_ref[...]
    sa = sa_ref[...]
    sb = sb_ref[...]
    half = ROPE_DIM // 2

    def norm_rope(x, gain, scale, out_ref):
        for j in range(x.shape[1] // LANES):
            xj = x[:, j * LANES:(j + 1) * LANES]
            ms = _dot((xj * xj).astype(BF16), gm_ref[...])
            y = xj * lax.rsqrt(ms + EPS) * gain
            y = y * cos + pltpu.roll(y, LANES - half, 1) * sa + pltpu.roll(y, half, 1) * sb
            out_ref[:, j * LANES:(j + 1) * LANES] = (y * scale).astype(BF16)

    norm_rope(_dot(hn, wq_ref[...]), qg_ref[...], A_HD ** -0.5, q_ref)
    norm_rope(_dot(hn, wk_ref[...]), kg_ref[...], 1.0, k_ref)
    v_ref[...] = _dot(hn, wv_ref[...]).astype(BF16)


def _attn_in(x, g, wq, wk, wv, qg, kg, gm, tabs, tm, m1, s1, s2):
    m = x.shape[0]
    nq = A_Q_HEADS * A_HD
    nkv = A_KV_HEADS * A_HD
    row = lambda i: (i, 0)
    fix = lambda i: (0, 0)

    def tab_map(i):
        pos, _ = _seq_pos(i * tm, m1, s1, s2)
        return (pos // tm, 0)

    tab = pl.BlockSpec((tm, LANES), tab_map)
    vec = pl.BlockSpec((1, LANES), fix)
    return pl.pallas_call(
        _attn_in_kernel,
        grid=(m // tm,),
        in_specs=[
            pl.BlockSpec((tm, D_MODEL), row),
            pl.BlockSpec((1, D_MODEL), fix),
            pl.BlockSpec((D_MODEL, nq), fix),
            pl.BlockSpec((D_MODEL, nkv), fix),
            pl.BlockSpec((D_MODEL, nkv), fix),
            vec, vec,
            pl.BlockSpec((LANES, LANES), fix),
            tab, tab, tab,
        ],
        out_specs=[pl.BlockSpec((tm, nq), row), pl.BlockSpec((tm, nkv), row),
                   pl.BlockSpec((tm, nkv), row)],
        out_shape=[jax.ShapeDtypeStruct((m, nq), BF16), jax.ShapeDtypeStruct((m, nkv), BF16),
                   jax.ShapeDtypeStruct((m, nkv), BF16)],
        compiler_params=_params(("parallel",), 40),
        name="attn_in",
    )(x, g, wq, wk, wv, qg, kg, gm, *tabs)


def _attn_kernel(sink_ref, q_ref, kp_ref, kc_ref, kn_ref, vp_ref, vc_ref, vn_ref, out_ref,
                 *, m1, s1, s2):
    blk = A_BLOCK
    pos0, slen = _seq_pos(pl.program_id(0) * blk, m1, s1, s2)
    prev_ok = pos0 > 0
    next_ok = pos0 + blk < slen
    t = lax.broadcasted_iota(jnp.int32, (blk, 3 * blk), 0)
    c = lax.broadcasted_iota(jnp.int32, (blk, 3 * blk), 1)
    valid = (jnp.abs(c - blk - t) <= WINDOW) & ((c >= blk) | prev_ok) & ((c < 2 * blk) | next_ok)

    kcat = jnp.concatenate([kp_ref[...], kc_ref[...], kn_ref[...]], axis=0).astype(F32)
    vcat = jnp.concatenate([vp_ref[...], vc_ref[...], vn_ref[...]], axis=0).astype(F32)
    lane_kv = lax.broadcasted_iota(jnp.int32, (3 * blk, LANES), 1)
    lane_q = lax.broadcasted_iota(jnp.int32, (blk, LANES), 1)

    def softmax_parts(s, sink):
        s = jnp.where(valid, s, -jnp.inf)
        m = jnp.maximum(jnp.max(s, axis=1, keepdims=True), sink)
        p = jnp.exp(s - m)
        denom = jnp.sum(p, axis=1, keepdims=True) + jnp.exp(sink - m)
        return p.astype(BF16), denom

    for h in range(A_KV_HEADS):
        tile, half = divmod(h, 2)
        own = (lane_kv < A_HD) if half == 0 else (lane_kv >= A_HD)

        def lo_hi(cat):
            mine = jnp.where(own, cat[:, tile * LANES:(tile + 1) * LANES], 0.0)
            other = pltpu.roll(mine, A_HD, 1)
            lo, hi = (mine, other) if half == 0 else (other, mine)
            return lo.astype(BF16), hi.astype(BF16)

        k_lo, k_hi = lo_hi(kcat)
        v_lo, v_hi = lo_hi(vcat)
        for j in range(A_GROUP // 2):
            grp = h * (A_GROUP // 2) + j
            q_pair = q_ref[:, grp * LANES:(grp + 1) * LANES]
            s_even = lax.dot_general(q_pair, k_lo, _NT, preferred_element_type=F32)
            s_odd = lax.dot_general(q_pair, k_hi, _NT, preferred_element_type=F32)
            p_even, d_even = softmax_parts(s_even, sink_ref[2 * grp])
            p_odd, d_odd = softmax_parts(s_odd, sink_ref[2 * grp + 1])
            o = _dot(p_even, v_lo) + _dot(p_odd, v_hi)
            o = o * jnp.where(lane_q < A_HD, 1.0 / d_even, 1.0 / d_odd)
            out_ref[:, grp * LANES:(grp + 1) * LANES] = o.astype(BF16)


def _attn(sink, q, k, v, m1, s1, s2):
    m = q.shape[0]
    blk = A_BLOCK
    nblk = m // blk
    nkv = A_KV_HEADS * A_HD
    cur = lambda i: (i, 0)
    prev = lambda i: (jnp.maximum(i - 1, 0), 0)
    nxt = lambda i: (jnp.minimum(i + 1, nblk - 1), 0)
    kv = lambda im: pl.BlockSpec((blk, nkv), im)
    kern = functools.partial(_attn_kernel, m1=m1, s1=s1, s2=s2)
    return pl.pallas_call(
        kern,
        grid=(nblk,),
        in_specs=[pl.BlockSpec(memory_space=pltpu.SMEM),
                  pl.BlockSpec((blk, D_MODEL), cur), kv(prev), kv(cur), kv(nxt),
                  kv(prev), kv(cur), kv(nxt)],
        out_specs=pl.BlockSpec((blk, D_MODEL), cur),
        out_shape=jax.ShapeDtypeStruct((m, D_MODEL), BF16),
        compiler_params=_params(("parallel",), 32),
        name="attn",
    )(sink, q, k, k, k, v, v, v)


def _proj_res_kernel(a_ref, w_ref, x_ref, out_ref):
    out_ref[...] = x_ref[...] + _dot(a_ref[...], w_ref[...])


def _proj_res(a, w, x, tm):
    m = x.shape[0]
    row = lambda i: (i, 0)
    blk = pl.BlockSpec((tm, D_MODEL), row)
    return pl.pallas_call(
        _proj_res_kernel,
        grid=(m // tm,),
        in_specs=[blk, pl.BlockSpec((D_MODEL, D_MODEL), lambda i: (0, 0)), blk],
        out_specs=blk,
        out_shape=jax.ShapeDtypeStruct((m, D_MODEL), F32),
        compiler_params=_params(("parallel",), 32),
        name="attn_out",
    )(a, w, x)


_R_E1, _R_E2, _R_P1, _R_P2, _R_RANK1, _R_RANK2 = range(6)


def _router_kernel(x_ref, g_ref, wr_ref, route_ref, count_ref, carry_ref):
    @pl.when(pl.program_id(0) == 0)
    def _():
        carry_ref[...] = jnp.zeros_like(carry_ref)

    hn = _rms(x_ref[...], g_ref[...]).astype(BF16)
    logits = _dot(hn, wr_ref[...])
    tm = logits.shape[0]
    lane = lax.broadcasted_iota(jnp.int32, logits.shape, 1).astype(F32)
    logits = jnp.where(lane < N_EXPERTS, logits, -jnp.inf)
    v1 = jnp.max(logits, axis=1, keepdims=True)
    i1 = jnp.min(jnp.where(logits == v1, lane, float(LANES)), axis=1, keepdims=True)
    rest = jnp.where(lane == i1, -jnp.inf, logits)
    v2 = jnp.max(rest, axis=1, keepdims=True)
    i2 = jnp.min(jnp.where(rest == v2, lane, float(LANES)), axis=1, keepdims=True)
    e2 = jnp.exp(v2 - v1)
    p1 = 1.0 / (1.0 + e2)
    p2 = e2 * p1

    sel = jnp.where((lane == i1) | (lane == i2), 1.0, 0.0)
    t_row = lax.broadcasted_iota(jnp.int32, (tm, tm), 0)
    t_col = lax.broadcasted_iota(jnp.int32, (tm, tm), 1)
    earlier = jnp.where(t_col < t_row, 1.0, 0.0).astype(BF16)
    before = _dot(earlier, sel.astype(BF16)) + carry_ref[0:1, :]
    rank1 = jnp.sum(jnp.where(lane == i1, before, 0.0), axis=1, keepdims=True)
    rank2 = jnp.sum(jnp.where(lane == i2, before, 0.0), axis=1, keepdims=True)
    total = carry_ref[0:1, :] + jnp.sum(sel, axis=0, keepdims=True)
    carry_ref[...] = jnp.broadcast_to(total, carry_ref.shape)
    count_ref[...] = jnp.broadcast_to(total, count_ref.shape)

    route = jnp.zeros_like(logits)
    for col, val in ((_R_E1, i1), (_R_E2, i2), (_R_P1, p1), (_R_P2, p2),
                     (_R_RANK1, rank1), (_R_RANK2, rank2)):
        route = jnp.where(lane == float(col), val, route)
    route_ref[...] = route


def _router(x, g, wr, tm):
    m = x.shape[0]
    row = lambda i: (i, 0)
    fix = lambda i: (0, 0)
    return pl.pallas_call(
        _router_kernel,
        grid=(m // tm,),
        in_specs=[pl.BlockSpec((tm, D_MODEL), row), pl.BlockSpec((1, D_MODEL), fix),
                  pl.BlockSpec((D_MODEL, LANES), fix)],
        out_specs=[pl.BlockSpec((tm, LANES), row), pl.BlockSpec((8, LANES), fix)],
        out_shape=[jax.ShapeDtypeStruct((m, LANES), F32), jax.ShapeDtypeStruct((8, LANES), F32)],
        scratch_shapes=[pltpu.VMEM((8, LANES), F32)],
        compiler_params=_params(("arbitrary",), 32),
        name="router",
    )(x, g, wr)


def _experts_kernel(te_ref, nused_ref, src0_ref, src_next_ref, dst_prev_ref, x_hbm, g_ref,
                    wg_ref, wu_ref, wd_ref, y_hbm, xbuf, acc_ref, hn_ref, gsem, ssem, zsem,
                    *, n_real, n_trash):
    del te_ref
    t = pl.program_id(0)
    f = pl.program_id(1)
    nused = nused_ref[0]
    tm = hn_ref.shape[0]
    chunk = tm // (D_FF // wg_ref.shape[1])
    slot = t % 2
    other = 1 - slot
    row0 = f * chunk

    def gather_row(idx_ref, j, s):
        return pltpu.make_async_copy(x_hbm.at[pl.ds(idx_ref[0, 0, j], 1)],
                                     xbuf.at[s, pl.ds(j, 1)], gsem.at[s])

    def scatter_row(j, s):
        return pltpu.make_async_copy(acc_ref.at[s, pl.ds(j, 1)],
                                     y_hbm.at[pl.ds(dst_prev_ref[0, 0, j], 1)], ssem.at[s])

    def issue_gather():
        for j in range(chunk):
            gather_row(src_next_ref, row0 + j, other).start()

    def issue_scatter():
        for j in range(chunk):
            scatter_row(row0 + j, other).start()

    def compute():
        hn = hn_ref[...]
        gate = _dot(hn, wg_ref[...])
        up = _dot(hn, wu_ref[...])
        act = (gate * _sigmoid(gate) * up).astype(BF16)
        acc_ref[slot] += _dot(act, wd_ref[...])

    @pl.when(f == 0)
    def _():
        @pl.when(t == 0)
        def _():
            acc_ref[1] = jnp.zeros(acc_ref.shape[1:], F32)
            fills = [pltpu.make_async_copy(acc_ref.at[1], y_hbm.at[pl.ds(n_real + k * tm, tm)], zsem)
                     for k in range(n_trash // tm)]
            for c in fills:
                c.start()
            for c in fills:
                c.wait()

            def body(jj, carry):
                for u in range(8):
                    gather_row(src0_ref, jj * 8 + u, 0).start()
                return carry

            lax.fori_loop(0, tm // 8, body, 0)

        @pl.when(t <= nused)
        def _():
            pltpu.make_async_copy(x_hbm.at[pl.ds(0, tm)], xbuf.at[slot], gsem.at[slot]).wait()

        @pl.when((t >= 2) & (t - 2 < nused))
        def _():
            pltpu.make_async_copy(acc_ref.at[slot], y_hbm.at[pl.ds(0, tm)], ssem.at[slot]).wait()

        @pl.when(t < nused)
        def _():
            hn_ref[...] = _rms(xbuf[slot], g_ref[...]).astype(BF16)
            acc_ref[slot] = jnp.zeros(acc_ref.shape[1:], F32)

    @pl.when(t == 0)
    def _():
        issue_gather()
        compute()

    @pl.when((t >= 1) & (t < nused))
    def _():
        issue_gather()
        issue_scatter()
        compute()

    @pl.when(t == nused)
    def _():
        issue_scatter()


def _experts(tile_expert, nused, src, dst, x, g, wg, wu, wd, tm, tf):
    m = x.shape[0]
    ntile = src.shape[0]
    n_trash = N_EXPERTS * tm
    nf = D_FF // tf
    assert tm % nf == 0 and D_FF % tf == 0
    idx = lambda im: pl.BlockSpec((1, 1, tm), im, memory_space=pltpu.SMEM)
    any_spec = pl.BlockSpec(memory_space=pl.ANY)
    kern = functools.partial(_experts_kernel, n_real=2 * m, n_trash=n_trash)
    return pl.pallas_call(
        kern,
        grid_spec=pltpu.PrefetchScalarGridSpec(
            num_scalar_prefetch=2,
            grid=(ntile, nf),
            in_specs=[
                idx(lambda t, f, te, nu: (0, 0, 0)),
                idx(lambda t, f, te, nu: (jnp.minimum(t + 1, ntile - 1), 0, 0)),
                idx(lambda t, f, te, nu: (jnp.maximum(t - 1, 0), 0, 0)),
                any_spec,
                pl.BlockSpec((1, D_MODEL), lambda t, f, te, nu: (0, 0)),
                pl.BlockSpec((None, D_MODEL, tf), lambda t, f, te, nu: (te[t], 0, f)),
                pl.BlockSpec((None, D_MODEL, tf), lambda t, f, te, nu: (te[t], 0, f)),
                pl.BlockSpec((None, tf, D_MODEL), lambda t, f, te, nu: (te[t], f, 0)),
            ],
            out_specs=any_spec,
            scratch_shapes=[pltpu.VMEM((2, tm, D_MODEL), F32), pltpu.VMEM((2, tm, D_MODEL), F32),
                            pltpu.VMEM((tm, D_MODEL), BF16), pltpu.SemaphoreType.DMA((2,)),
                            pltpu.SemaphoreType.DMA((2,)), pltpu.SemaphoreType.DMA(())],
        ),
        out_shape=jax.ShapeDtypeStruct((2 * m + n_trash, D_MODEL), F32),
        compiler_params=_params(("arbitrary", "arbitrary"), 48),
        name="moe_experts",
    )(tile_expert, nused, src, src, dst, x, g, wg, wu, wd)


def _combine_kernel(x_ref, route_ref, y1_ref, y2_ref, out_ref):
    route = route_ref[...]
    p1 = route[:, _R_P1:_R_P1 + 1]
    p2 = route[:, _R_P2:_R_P2 + 1]
    out_ref[...] = x_ref[...] + (p1 * y1_ref[...] + p2 * y2_ref[...])


def _combine(x, route, y, tc):
    m = x.shape[0]
    row = lambda i: (i, 0)
    blk = pl.BlockSpec((tc, D_MODEL), row)
    return pl.pallas_call(
        _combine_kernel,
        grid=(m // tc,),
        in_specs=[blk, pl.BlockSpec((tc, LANES), row), blk,
                  pl.BlockSpec((tc, D_MODEL), lambda i: (i + m // tc, 0))],
        out_specs=blk,
        out_shape=jax.ShapeDtypeStruct((m, D_MODEL), F32),
        compiler_params=_params(("parallel",), 32),
        name="moe_combine",
    )(x, route, y, y)


def _moe(x, g, wr, wg, wu, wd, tm_route, tm, tf):
    m = x.shape[0]
    route, counts = _router(x, g, wr, tm_route)
    e1 = route[:, _R_E1].astype(jnp.int32)
    e2 = route[:, _R_E2].astype(jnp.int32)
    n_e = counts[0, :N_EXPERTS].astype(jnp.int32)
    tiles_e = (n_e + tm - 1) // tm
    tile_end = jnp.cumsum(tiles_e)
    row_start = (tile_end - tiles_e) * tm
    pos = jnp.concatenate([row_start[e1] + route[:, _R_RANK1].astype(jnp.int32),
                           row_start[e2] + route[:, _R_RANK2].astype(jnp.int32)])
    ntile = 2 * m // tm + N_EXPERTS + 2
    nused = tile_end[-1:]
    tile_ids = jnp.minimum(jnp.arange(ntile, dtype=jnp.int32), nused - 1)
    tile_expert = jnp.sum(tile_ids[:, None] >= tile_end[None, :], axis=1).astype(jnp.int32)
    rows = jnp.arange(ntile * tm, dtype=jnp.int32)
    spare = 2 * m + rows - jnp.cumsum(n_e)[jnp.repeat(tile_expert, tm)]
    spare = jnp.clip(spare, 2 * m, 2 * m + N_EXPERTS * tm - 1)
    tok = jnp.arange(m, dtype=jnp.int32)
    src = jnp.zeros_like(rows).at[pos].set(jnp.concatenate([tok, tok]))
    dst = spare.at[pos].set(jnp.concatenate([tok, m + tok]))
    y = _experts(tile_expert, nused.astype(jnp.int32), src.reshape(ntile, 1, tm),
                 dst.reshape(ntile, 1, tm), x, g, wg, wu, wd, tm, tf)
    return _combine(x, route, y, tm_route)


def _mlstm_gate_perm():
    perm = []
    for pair in range(M_HEADS // 2):
        for hh in range(2):
            for gtype in range(4):
                perm.append(gtype * M_HEADS + 2 * pair + hh)
    return jnp.array(perm, jnp.int32)


def kernel(x_prompt, x_sample, norm_mix, norm_ffn, mlstm_w_in, mlstm_gate_bias, mlstm_head_gain,
           mlstm_w_out, attn_w_in, attn_q_gain, attn_k_gain, attn_sink, attn_w_out, ffn_w_gate,
           ffn_w_up, ffn_w_down, moe_w_router, moe_w_gate, moe_w_up, moe_w_down):
    b1, s1, _ = x_prompt.shape
    b2, s2, _ = x_sample.shape
    m1 = b1 * s1
    x = jnp.concatenate([x_prompt.reshape(m1, D_MODEL), x_sample.reshape(b2 * s2, D_MODEL)], axis=0)
    m = x.shape[0]
    seq_gcd = math.gcd(s1, s2)
    tm = _tile(seq_gcd, 512)
    rb = _tile(seq_gcd, 1024)
    tm_ffn = _tile(m, 1024)
    tm_moe = _tile(2 * m, 512)
    tf = 512
    tf_moe = 896
    depth = norm_mix.shape[0]
    vec = lambda a: a.astype(F32).reshape(1, -1)
    nq = M_HEADS * M_QK
    nv = M_HEADS * M_V

    tabs = _rope_tables(max(s1, s2), tm)
    group_mean = jnp.where((jnp.arange(LANES)[:, None] // A_HD) == (jnp.arange(LANES)[None, :] // A_HD),
                           1.0 / A_HD, 0.0).astype(BF16)
    gate_perm = _mlstm_gate_perm()

    for i in range(depth):
        j = i // 2
        if i % 2 == 0:
            w_in = mlstm_w_in[j]
            wq = w_in[:, :nq].astype(BF16)
            wkT = w_in[:, nq:2 * nq].T.astype(BF16)
            wv = w_in[:, 2 * nq:2 * nq + nv].astype(BF16)
            wo = w_in[:, 2 * nq + nv:2 * nq + 2 * nv].astype(BF16)
            wgT = w_in[:, 2 * nq + 2 * nv:][:, gate_perm].T.astype(BF16)
            bgT = mlstm_gate_bias[j].astype(F32)[gate_perm].reshape(-1, 1)
            q, kT, v, o, gT = _mlstm_in(x, vec(norm_mix[i]), wq, wkT, wv, wo, wgT, bgT, tm)
            hf, hb = _mlstm_core(q, kT, v, gT, rb, m1, s1, s2)
            x = _mlstm_out(hf, hb, o, vec(mlstm_head_gain[j]), mlstm_w_out[j].astype(BF16), x, tm)
            x = _ffn(x, vec(norm_ffn[i]), ffn_w_gate[j].astype(BF16), ffn_w_up[j].astype(BF16),
                     ffn_w_down[j].astype(BF16), tm_ffn, tf)
        else:
            w_in = attn_w_in[j]
            a_q = A_Q_HEADS * A_HD
            a_kv = A_KV_HEADS * A_HD
            qg = jnp.tile(attn_q_gain[j].astype(F32), LANES // A_HD).reshape(1, LANES)
            kg = jnp.tile(attn_k_gain[j].astype(F32), LANES // A_HD).reshape(1, LANES)
            q, k, v = _attn_in(x, vec(norm_mix[i]), w_in[:, :a_q].astype(BF16),
                               w_in[:, a_q:a_q + a_kv].astype(BF16), w_in[:, a_q + a_kv:].astype(BF16),
                               qg, kg, group_mean, tabs, tm, m1, s1, s2)
            att = _attn(attn_sink[j].astype(F32), q, k, v, m1, s1, s2)
            x = _proj_res(att, attn_w_out[j].astype(BF16), x, tm)
            wr = jnp.pad(moe_w_router[j], ((0, 0), (0, LANES - N_EXPERTS))).astype(BF16)
            x = _moe(x, vec(norm_ffn[i]), wr, moe_w_gate[j].astype(BF16), moe_w_up[j].astype(BF16),
                     moe_w_down[j].astype(BF16), tm, tm_moe, tf_moe)

    return x[:m1].reshape(b1, s1, D_MODEL), x[m1:].reshape(b2, s2, D_MODEL)
```

```python
import functools
import math

import jax
import jax.numpy as jnp
from jax import lax
from jax.experimental import pallas as pl
from jax.experimental.pallas import tpu as pltpu

F32 = jnp.float32
BF16 = jnp.bfloat16

D_MODEL = 1024
EPS = 1e-6
M_HEADS = 8
M_QK = 64
M_V = 128
M_CHUNK = 128
A_Q_HEADS = 16
A_KV_HEADS = 4
A_HD = 64
A_GROUP = A_Q_HEADS // A_KV_HEADS
A_BLOCK = 128
WINDOW = 128
ROPE_THETA = 500000.0
ROPE_DIM = 16
D_FF = 3584
N_EXPERTS = 8

LANES = 128
VMEM_BYTES_V7X = 64 * 1024 * 1024

_NT = (((1,), (1,)), ((), ()))


def _params(semantics, vmem_mb):
    assert vmem_mb * 1024 * 1024 < VMEM_BYTES_V7X
    return pltpu.CompilerParams(dimension_semantics=semantics,
                                vmem_limit_bytes=vmem_mb * 1024 * 1024)


def _tile(total, pref):
    t = min(pref, total)
    t -= t % LANES
    while total % t:
        t -= LANES
    return t


def _rms(x, g):
    ms = jnp.mean(x * x, axis=-1, keepdims=True)
    return x * lax.rsqrt(ms + EPS) * g


def _sigmoid(x):
    return 1.0 / (1.0 + jnp.exp(-x))


def _dot(a, b):
    return jnp.dot(a, b, preferred_element_type=F32)


def _seq_pos(row0, m1, s1, s2):
    first = row0 < m1
    pos = jnp.where(first, row0 % s1, (row0 - m1) % s2)
    return pos, jnp.where(first, s1, s2)


def _mlstm_in_kernel(x_ref, g_ref, wqT_ref, wkT_ref, wk_ref, wv_ref, wo_ref, wgT_ref, bgT_ref,
                     qT_ref, kT_ref, k_ref, v_ref, o_ref, st_ref):
    hn = _rms(x_ref[...], g_ref[...]).astype(BF16)
    k_ref[...] = _dot(hn, wk_ref[...]).astype(BF16)
    v_ref[...] = _dot(hn, wv_ref[...]).astype(BF16)
    o_ref[...] = _dot(hn, wo_ref[...]).astype(BF16)
    qT = lax.dot_general(wqT_ref[...], hn, _NT, preferred_element_type=F32) * (M_QK ** -0.5)
    qT = qT.astype(BF16)
    kT = lax.dot_general(wkT_ref[...], hn, _NT, preferred_element_type=F32).astype(BF16)
    gT = lax.dot_general(wgT_ref[...], hn, _NT, preferred_element_type=F32) + bgT_ref[...]
    log_sig = jnp.minimum(gT, 0.0) - jnp.log1p(jnp.exp(-jnp.abs(gT)))
    row = lax.broadcasted_iota(jnp.int32, gT.shape, 0)
    gT = jnp.where(row % 2 == 1, log_sig, gT)
    for j in range(kT_ref.shape[0]):
        cols = slice(j * M_CHUNK, (j + 1) * M_CHUNK)
        qT_ref[j] = qT[:, cols]
        kT_ref[j] = kT[:, cols]
        for pair in range(M_HEADS // 2):
            stats = _mlstm_gate_stats(gT[pair * 8:(pair + 1) * 8, cols])
            st_ref[j, pair * _N_STATS * 8:(pair + 1) * _N_STATS * 8, :] = jnp.concatenate(stats, axis=0)


def _mlstm_in(x, g, wqT, wkT, wk, wv, wo, wgT, bgT, tm):
    m = x.shape[0]
    nq = M_HEADS * M_QK
    nv = M_HEADS * M_V
    ng = 4 * M_HEADS
    row = lambda i: (i, 0)
    fix = lambda i: (0, 0)
    chunked = lambda n: pl.BlockSpec((tm // M_CHUNK, n, M_CHUNK), lambda i: (i, 0, 0))
    chunked_shape = lambda n, dt: jax.ShapeDtypeStruct((m // M_CHUNK, n, M_CHUNK), dt)
    return pl.pallas_call(
        _mlstm_in_kernel,
        grid=(m // tm,),
        in_specs=[
            pl.BlockSpec((tm, D_MODEL), row),
            pl.BlockSpec((1, D_MODEL), fix),
            pl.BlockSpec((nq, D_MODEL), fix),
            pl.BlockSpec((nq, D_MODEL), fix),
            pl.BlockSpec((D_MODEL, nq), fix),
            pl.BlockSpec((D_MODEL, nv), fix),
            pl.BlockSpec((D_MODEL, nv), fix),
            pl.BlockSpec((ng, D_MODEL), fix),
            pl.BlockSpec((ng, 1), fix),
        ],
        out_specs=[chunked(nq), chunked(nq), pl.BlockSpec((tm, nq), row),
                   pl.BlockSpec((tm, nv), row), pl.BlockSpec((tm, nv), row), chunked(_N_STATS * ng)],
        out_shape=[chunked_shape(nq, BF16), chunked_shape(nq, BF16),
                   jax.ShapeDtypeStruct((m, nq), BF16), jax.ShapeDtypeStruct((m, nv), BF16),
                   jax.ShapeDtypeStruct((m, nv), BF16), chunked_shape(_N_STATS * ng, F32)],
        compiler_params=_params(("parallel",), 40),
        name="mlstm_in",
    )(x, g, wqT, wkT, wk, wv, wo, wgT, bgT)


_N_STATS = 6


def _lane_scan(x, combine, identity, reverse):
    n = x.shape[-1]
    lane = lax.broadcasted_iota(jnp.int32, x.shape, 1)
    sh = 1
    while sh < n:
        if reverse:
            moved = jnp.where(lane < n - sh, pltpu.roll(x, n - sh, 1), identity)
        else:
            moved = jnp.where(lane >= sh, pltpu.roll(x, sh, 1), identity)
        x = combine(x, moved)
        sh *= 2
    return x


def _mlstm_gate_stats(gates):
    row = lax.broadcasted_iota(jnp.int32, gates.shape, 0)
    bwd_row = (row % 4) >= 2
    add = lambda x, y: x + y
    cum = jnp.where(bwd_row, _lane_scan(gates, add, 0.0, True), _lane_scan(gates, add, 0.0, False))
    b = pltpu.roll(cum, 7, 0)
    total = jnp.broadcast_to(jnp.sum(gates, axis=1, keepdims=True), gates.shape)
    g = pltpu.roll(total, 7, 0)
    r = gates - b
    pm = jnp.where(bwd_row, _lane_scan(r, jnp.maximum, -jnp.inf, True),
                   _lane_scan(r, jnp.maximum, -jnp.inf, False))
    a = g + r
    a_max = jnp.broadcast_to(jnp.max(a, axis=1, keepdims=True), gates.shape)
    return b, r, pm, g, a_max, jnp.exp(a - a_max)


def _mlstm_unit_pre(qT, kT, k, v2, stats, reverse):
    L = M_CHUNK
    s_idx = lax.broadcasted_iota(jnp.int32, (L, L), 0)
    t_idx = lax.broadcasted_iota(jnp.int32, (L, L), 1)
    keep = (s_idx >= t_idx) if reverse else (s_idx <= t_idx)
    head_a_lanes = lax.broadcasted_iota(jnp.int32, (L, LANES), 1) < M_QK
    head_a_rows = lax.broadcasted_iota(jnp.int32, (LANES, L), 0) < M_QK
    zeros_k = jnp.zeros_like(k)
    k2 = jnp.concatenate([jnp.where(head_a_lanes, k, zeros_k), jnp.where(head_a_lanes, zeros_k, k)],
                         axis=0)
    s2 = _dot(k2, qT)
    sT, den0 = [], []
    for hd in range(2):
        _, r, pm, _, _, _ = stats[hd]
        r_col = jnp.broadcast_to(r, (L, L)).T
        sT.append(s2[hd * L:(hd + 1) * L] * jnp.exp(jnp.where(keep, r_col - pm, -jnp.inf)))
        den0.append(jnp.sum(sT[hd], axis=0, keepdims=True))
    w = jnp.where(head_a_rows, stats[0][5], stats[1][5])
    ones = jnp.ones((L, M_V), BF16)
    kv = _dot((kT.astype(F32) * w).astype(BF16), jnp.concatenate([v2, ones], axis=1))
    col = lax.broadcasted_iota(jnp.int32, kv.shape, 1)
    row_a = lax.broadcasted_iota(jnp.int32, kv.shape, 0) < M_QK
    other_v = (row_a & (col >= M_V) & (col < 2 * M_V)) | (~row_a & (col < M_V))
    return sT, den0, jnp.where(other_v, 0.0, kv)


def _mlstm_unit_post(qT, v2, stats, pre, c_pair, m_rows):
    L = M_CHUNK
    sT, den0, kv = pre
    head_a_rows = lax.broadcasted_iota(jnp.int32, (LANES, L), 0) < M_QK
    row_a = lax.broadcasted_iota(jnp.int32, kv.shape, 0) < M_QK
    qT_f = qT.astype(F32)
    qn_all = qT_f * c_pair[:, 2 * M_V:]
    s_scale, q_scale, m_new, decay, scale = [], [], [], [], []
    for hd in range(2):
        b, _, pm, g, a_max, _ = stats[hd]
        m_row = m_rows[hd]
        u = jnp.maximum(pm, m_row)
        c1 = jnp.exp(pm - u)
        e = jnp.exp(m_row - u)
        qn = jnp.sum(qn_all[hd * M_QK:(hd + 1) * M_QK], axis=0, keepdims=True)
        den = c1 * den0[hd] + e * qn
        inv = 1.0 / jnp.maximum(jnp.abs(den), jnp.exp(-(b + u)))
        s_scale.append(c1 * inv)
        q_scale.append(e * inv)
        m_new.append(jnp.maximum(g + m_row, a_max))
        decay.append(jnp.exp(g + m_row - m_new[hd])[:, 0:1])
        scale.append(jnp.exp(a_max - m_new[hd])[:, 0:1])
    lhsT = jnp.concatenate([sT[0] * s_scale[0], sT[1] * s_scale[1],
                            qT_f * jnp.where(head_a_rows, q_scale[0], q_scale[1])],
                           axis=0).astype(BF16)
    zeros_v = jnp.zeros((L, M_V), BF16)
    rhs = jnp.concatenate([jnp.concatenate([v2[:, :M_V], zeros_v], axis=1),
                           jnp.concatenate([zeros_v, v2[:, M_V:]], axis=1),
                           c_pair[:, :2 * M_V].astype(BF16)], axis=0)
    h = lax.dot_general(lhsT, rhs, (((0,), (0,)), ((), ())), preferred_element_type=F32)
    c_new = jnp.where(row_a, decay[0], decay[1]) * c_pair + jnp.where(row_a, scale[0], scale[1]) * kv
    return h, c_new, m_new


def _mlstm_core_kernel(qTf_ref, kTf_ref, kf_ref, vf_ref, sf_ref, qTb_ref, kTb_ref, kb_ref, vb_ref,
                       sb_ref, hf_ref, hb_ref, c_ref, m_ref, *, nchunk, rb, m1, s1, s2):
    i = pl.program_id(1)
    nblk = pl.num_programs(1)
    pos_f, _ = _seq_pos(i * rb, m1, s1, s2)
    pos_b, len_b = _seq_pos((nblk - 1 - i) * rb, m1, s1, s2)

    def reset(dirn):
        c_ref[dirn] = jnp.zeros(c_ref.shape[1:], F32)
        for hd in range(2):
            m_ref[hd * 2 + dirn] = jnp.zeros(m_ref.shape[1:], F32)

    pl.when(pos_f == 0)(lambda: reset(0))
    pl.when(pos_b + rb == len_b)(lambda: reset(1))

    c_state = [c_ref[dirn] for dirn in range(2)]
    m_state = [[m_ref[hd * 2 + dirn][0:1] for hd in range(2)] for dirn in range(2)]
    refs = ((qTf_ref, kTf_ref, kf_ref, vf_ref, sf_ref, hf_ref),
            (qTb_ref, kTb_ref, kb_ref, vb_ref, sb_ref, hb_ref))

    def unit(step, dirn):
        qT_ref, kT_ref, k_ref, v_ref, st_ref, out_ref = refs[dirn]
        cc = nchunk - 1 - step if dirn else step
        rows = slice(cc * M_CHUNK, (cc + 1) * M_CHUNK)
        st = st_ref[cc]
        stats = [tuple(st[n * 8 + hd * 4 + dirn * 2:n * 8 + hd * 4 + dirn * 2 + 1]
                       for n in range(_N_STATS)) for hd in range(2)]
        return qT_ref[cc], kT_ref[cc], k_ref[rows, :], v_ref[rows, :], stats, out_ref, rows

    pre = {}
    for step in range(nchunk):
        for dirn in range(2):
            qT, kT, k, v2, stats, _, _ = unit(step, dirn)
            pre[step, dirn] = _mlstm_unit_pre(qT, kT, k, v2, stats, reverse=bool(dirn))
    for step in range(nchunk):
        for dirn in range(2):
            qT, _, _, v2, stats, out_ref, rows = unit(step, dirn)
            h, c_state[dirn], m_state[dirn] = _mlstm_unit_post(
                qT, v2, stats, pre[step, dirn], c_state[dirn], m_state[dirn])
            out_ref[rows, :] = h

    for dirn in range(2):
        c_ref[dirn] = c_state[dirn]
        for hd in range(2):
            m_ref[hd * 2 + dirn] = jnp.broadcast_to(m_state[dirn][hd], m_ref.shape[1:])


def _mlstm_core(qT, kT, k, v, gT, rb, m1, s1, s2):
    m = k.shape[0]
    nblk = m // rb
    nchunk = rb // M_CHUNK
    npair = M_HEADS // 2
    fwd = lambda p, i: (i, p)
    bwd = lambda p, i: (nblk - 1 - i, p)
    fwd3 = lambda p, i: (i, p, 0)
    bwd3 = lambda p, i: (nblk - 1 - i, p, 0)

    def specs(im2, im3):
        return [
            pl.BlockSpec((nchunk, LANES, M_CHUNK), im3),
            pl.BlockSpec((nchunk, LANES, M_CHUNK), im3),
            pl.BlockSpec((rb, LANES), im2),
            pl.BlockSpec((rb, 2 * M_V), im2),
            pl.BlockSpec((nchunk, _N_STATS * 8, M_CHUNK), im3),
        ]

    kern = functools.partial(_mlstm_core_kernel, nchunk=nchunk, rb=rb, m1=m1, s1=s1, s2=s2)
    return pl.pallas_call(
        kern,
        grid=(npair, nblk),
        in_specs=specs(fwd, fwd3) + specs(bwd, bwd3),
        out_specs=[pl.BlockSpec((rb, 2 * M_V), fwd), pl.BlockSpec((rb, 2 * M_V), bwd)],
        out_shape=[jax.ShapeDtypeStruct((m, M_HEADS * M_V), F32)] * 2,
        scratch_shapes=[pltpu.VMEM((2, LANES, 3 * M_V), F32), pltpu.VMEM((4, 8, M_CHUNK), F32)],
        compiler_params=_params(("parallel", "arbitrary"), 32),
        name="mlstm_core",
    )(qT, kT, k, v, gT, qT, kT, k, v, gT)


def _mlstm_out_kernel(hf_ref, hb_ref, o_ref, gain_ref, w_ref, x_ref, out_ref):
    h = hf_ref[...] + hb_ref[...]
    parts = []
    for hd in range(M_HEADS):
        hh = h[:, hd * M_V:(hd + 1) * M_V]
        ms = jnp.mean(hh * hh, axis=-1, keepdims=True)
        parts.append(hh * lax.rsqrt(ms + EPS))
    hn = jnp.concatenate(parts, axis=1) * gain_ref[...]
    hg = (_sigmoid(o_ref[...].astype(F32)) * hn).astype(BF16)
    out_ref[...] = x_ref[...] + _dot(hg, w_ref[...])


def _mlstm_out(hf, hb, o, gain, w, x, tm):
    m = x.shape[0]
    row = lambda i: (i, 0)
    fix = lambda i: (0, 0)
    blk = pl.BlockSpec((tm, D_MODEL), row)
    return pl.pallas_call(
        _mlstm_out_kernel,
        grid=(m // tm,),
        in_specs=[blk, blk, blk, pl.BlockSpec((1, D_MODEL), fix),
                  pl.BlockSpec((D_MODEL, D_MODEL), fix), blk],
        out_specs=blk,
        out_shape=jax.ShapeDtypeStruct((m, D_MODEL), F32),
        compiler_params=_params(("parallel",), 40),
        name="mlstm_out",
    )(hf, hb, o, gain, w, x)


def _ffn_kernel(x_ref, g_ref, wg_ref, wu_ref, wd_ref, out_ref, hn_ref, acc_ref):
    f = pl.program_id(1)

    @pl.when(f == 0)
    def _():
        hn_ref[...] = _rms(x_ref[...], g_ref[...]).astype(BF16)
        acc_ref[...] = jnp.zeros_like(acc_ref)

    hn = hn_ref[...]
    gate = _dot(hn, wg_ref[...])
    up = _dot(hn, wu_ref[...])
    act = (gate * _sigmoid(gate) * up).astype(BF16)
    acc_ref[...] += _dot(act, wd_ref[...])

    @pl.when(f == pl.num_programs(1) - 1)
    def _():
        out_ref[...] = x_ref[...] + acc_ref[...]


def _ffn(x, g, wg, wu, wd, tm, tf):
    m = x.shape[0]
    row = lambda i, f: (i, 0)
    return pl.pallas_call(
        _ffn_kernel,
        grid=(m // tm, D_FF // tf),
        in_specs=[
            pl.BlockSpec((tm, D_MODEL), row),
            pl.BlockSpec((1, D_MODEL), lambda i, f: (0, 0)),
            pl.BlockSpec((D_MODEL, tf), lambda i, f: (0, f)),
            pl.BlockSpec((D_MODEL, tf), lambda i, f: (0, f)),
            pl.BlockSpec((tf, D_MODEL), lambda i, f: (f, 0)),
        ],
        out_specs=pl.BlockSpec((tm, D_MODEL), row),
        out_shape=jax.ShapeDtypeStruct((m, D_MODEL), F32),
        scratch_shapes=[pltpu.VMEM((tm, D_MODEL), BF16), pltpu.VMEM((tm, D_MODEL), F32)],
        compiler_params=_params(("parallel", "arbitrary"), 48),
        name="ffn",
    )(x, g, wg, wu, wd)


def _rope_table_kernel(inv_ref, ma_ref, mb_ref, cos_ref, sa_ref, sb_ref):
    rows = cos_ref.shape[0]
    pos = pl.program_id(0) * rows + lax.broadcasted_iota(jnp.int32, (rows, LANES), 0)
    ang = pos.astype(F32) * inv_ref[...]
    sin = jnp.sin(ang)
    cos_ref[...] = jnp.cos(ang)
    sa_ref[...] = sin * ma_ref[...]
    sb_ref[...] = sin * mb_ref[...]


def _rope_tables(smax, rows):
    half = ROPE_DIM // 2
    d = jnp.arange(LANES) % A_HD
    inv = ROPE_THETA ** (-(jnp.arange(half, dtype=F32) * 2.0) / ROPE_DIM)
    inv_lane = jnp.where(d < ROPE_DIM, inv[d % half], 0.0).astype(F32)[None, :]
    ma = jnp.where(d < half, -1.0, 0.0).astype(F32)[None, :]
    mb = jnp.where((d >= half) & (d < ROPE_DIM), 1.0, 0.0).astype(F32)[None, :]
    fix = lambda i: (0, 0)
    vec = pl.BlockSpec((1, LANES), fix)
    tab = pl.BlockSpec((rows, LANES), lambda i: (i, 0))
    return pl.pallas_call(
        _rope_table_kernel,
        grid=(smax // rows,),
        in_specs=[vec, vec, vec],
        out_specs=[tab, tab, tab],
        out_shape=[jax.ShapeDtypeStruct((smax, LANES), F32)] * 3,
        compiler_params=_params(("parallel",), 16),
        name="rope_tables",
    )(inv_lane, ma, mb)


def _attn_in_kernel(x_ref, g_ref, wq_ref, wk_ref, wv_ref, qg_ref, kg_ref, gm_ref,
                    cos_ref, sa_ref, sb_ref, q_ref, k_ref, v_ref):
    hn = _rms(x_ref[...], g_ref[...]).astype(BF16)
    cos = cos_ref[...]
    sa = sa_ref[...]
    sb = sb_ref[...]
    half = ROPE_DIM // 2

    def norm_rope(x, gain, scale, out_ref):
        for j in range(x.shape[1] // LANES):
            xj = x[:, j * LANES:(j + 1) * LANES]
            ms = _dot((xj * xj).astype(BF16), gm_ref[...])
            y = xj * lax.rsqrt(ms + EPS) * gain
            y = y * cos + pltpu.roll(y, LANES - half, 1) * sa + pltpu.roll(y, half, 1) * sb
            out_ref[:, j * LANES:(j + 1) * LANES] = (y * scale).astype(BF16)

    norm_rope(_dot(hn, wq_ref[...]), qg_ref[...], A_HD ** -0.5, q_ref)
    norm_rope(_dot(hn, wk_ref[...]), kg_ref[...], 1.0, k_ref)
    v_ref[...] = _dot(hn, wv_ref[...]).astype(BF16)


def _attn_in(x, g, wq, wk, wv, qg, kg, gm, tabs, tm, m1, s1, s2):
    m = x.shape[0]
    nq = A_Q_HEADS * A_HD
    nkv = A_KV_HEADS * A_HD
    row = lambda i: (i, 0)
    fix = lambda i: (0, 0)

    def tab_map(i):
        pos, _ = _seq_pos(i * tm, m1, s1, s2)
        return (pos // tm, 0)

    tab = pl.BlockSpec((tm, LANES), tab_map)
    vec = pl.BlockSpec((1, LANES), fix)
    return pl.pallas_call(
        _attn_in_kernel,
        grid=(m // tm,),
        in_specs=[
            pl.BlockSpec((tm, D_MODEL), row),
            pl.BlockSpec((1, D_MODEL), fix),
            pl.BlockSpec((D_MODEL, nq), fix),
            pl.BlockSpec((D_MODEL, nkv), fix),
            pl.BlockSpec((D_MODEL, nkv), fix),
            vec, vec,
            pl.BlockSpec((LANES, LANES), fix),
            tab, tab, tab,
        ],
        out_specs=[pl.BlockSpec((tm, nq), row), pl.BlockSpec((tm, nkv), row),
                   pl.BlockSpec((tm, nkv), row)],
        out_shape=[jax.ShapeDtypeStruct((m, nq), BF16), jax.ShapeDtypeStruct((m, nkv), BF16),
                   jax.ShapeDtypeStruct((m, nkv), BF16)],
        compiler_params=_params(("parallel",), 40),
        name="attn_in",
    )(x, g, wq, wk, wv, qg, kg, gm, *tabs)


def _attn_kernel(sink_ref, q_ref, kp_ref, kc_ref, kn_ref, vp_ref, vc_ref, vn_ref, out_ref,
                 *, m1, s1, s2):
    blk = A_BLOCK
    pos0, slen = _seq_pos(pl.program_id(0) * blk, m1, s1, s2)
    prev_ok = pos0 > 0
    next_ok = pos0 + blk < slen
    t = lax.broadcasted_iota(jnp.int32, (blk, 3 * blk), 0)
    c = lax.broadcasted_iota(jnp.int32, (blk, 3 * blk), 1)
    valid = (jnp.abs(c - blk - t) <= WINDOW) & ((c >= blk) | prev_ok) & ((c < 2 * blk) | next_ok)

    kcat = jnp.concatenate([kp_ref[...], kc_ref[...], kn_ref[...]], axis=0).astype(F32)
    vcat = jnp.concatenate([vp_ref[...], vc_ref[...], vn_ref[...]], axis=0).astype(F32)
    lane_kv = lax.broadcasted_iota(jnp.int32, (3 * blk, LANES), 1)
    lane_q = lax.broadcasted_iota(jnp.int32, (blk, LANES), 1)

    def softmax_parts(s, sink):
        s = jnp.where(valid, s, -jnp.inf)
        m = jnp.maximum(jnp.max(s, axis=1, keepdims=True), sink)
        p = jnp.exp(s - m)
        denom = jnp.sum(p, axis=1, keepdims=True) + jnp.exp(sink - m)
        return p.astype(BF16), denom

    def lo_hi(cat, h):
        tile, half = divmod(h, 2)
        own = (lane_kv < A_HD) if half == 0 else (lane_kv >= A_HD)
        mine = jnp.where(own, cat[:, tile * LANES:(tile + 1) * LANES], 0.0)
        other = pltpu.roll(mine, A_HD, 1)
        lo, hi = (mine, other) if half == 0 else (other, mine)
        return jnp.concatenate([lo, hi], axis=0).astype(BF16)

    npair = A_GROUP // 2
    kk = [lo_hi(kcat, h) for h in range(A_KV_HEADS)]
    vv = [lo_hi(vcat, h) for h in range(A_KV_HEADS)]
    scores = []
    for h in range(A_KV_HEADS):
        q_pairs = jnp.concatenate([q_ref[:, (h * npair + j) * LANES:(h * npair + j + 1) * LANES]
                                   for j in range(npair)], axis=0)
        scores.append(lax.dot_general(q_pairs, kk[h], _NT, preferred_element_type=F32))
    for h in range(A_KV_HEADS):
        p_rows, inv_rows = [], []
        for j in range(npair):
            parts = [softmax_parts(scores[h][j * blk:(j + 1) * blk, par * 3 * blk:(par + 1) * 3 * blk],
                                   sink_ref[(h * npair + j) * 2 + par]) for par in range(2)]
            p_rows.append(jnp.concatenate([parts[0][0], parts[1][0]], axis=1))
            inv_rows.append(jnp.where(lane_q < A_HD, 1.0 / parts[0][1], 1.0 / parts[1][1]))
        o = _dot(jnp.concatenate(p_rows, axis=0), vv[h])
        for j in range(npair):
            grp = h * npair + j
            out_ref[:, grp * LANES:(grp + 1) * LANES] = (o[j * blk:(j + 1) * blk] * inv_rows[j]).astype(BF16)


def _attn(sink, q, k, v, m1, s1, s2):
    m = q.shape[0]
    blk = A_BLOCK
    nblk = m // blk
    nkv = A_KV_HEADS * A_HD
    cur = lambda i: (i, 0)
    prev = lambda i: (jnp.maximum(i - 1, 0), 0)
    nxt = lambda i: (jnp.minimum(i + 1, nblk - 1), 0)
    kv = lambda im: pl.BlockSpec((blk, nkv), im)
    kern = functools.partial(_attn_kernel, m1=m1, s1=s1, s2=s2)
    return pl.pallas_call(
        kern,
        grid=(nblk,),
        in_specs=[pl.BlockSpec(memory_space=pltpu.SMEM),
                  pl.BlockSpec((blk, D_MODEL), cur), kv(prev), kv(cur), kv(nxt),
                  kv(prev), kv(cur), kv(nxt)],
        out_specs=pl.BlockSpec((blk, D_MODEL), cur),
        out_shape=jax.ShapeDtypeStruct((m, D_MODEL), BF16),
        compiler_params=_params(("parallel",), 32),
        name="attn",
    )(sink, q, k, k, k, v, v, v)


def _proj_res_kernel(a_ref, w_ref, x_ref, out_ref):
    out_ref[...] = x_ref[...] + _dot(a_ref[...], w_ref[...])


def _proj_res(a, w, x, tm):
    m = x.shape[0]
    row = lambda i: (i, 0)
    blk = pl.BlockSpec((tm, D_MODEL), row)
    return pl.pallas_call(
        _proj_res_kernel,
        grid=(m // tm,),
        in_specs=[blk, pl.BlockSpec((D_MODEL, D_MODEL), lambda i: (0, 0)), blk],
        out_specs=blk,
        out_shape=jax.ShapeDtypeStruct((m, D_MODEL), F32),
        compiler_params=_params(("parallel",), 32),
        name="attn_out",
    )(a, w, x)


_R_E1, _R_E2, _R_P1, _R_P2, _R_RANK1, _R_RANK2 = range(6)


def _router_kernel(x_ref, g_ref, wr_ref, route_ref, count_ref, carry_ref):
    @pl.when(pl.program_id(0) == 0)
    def _():
        carry_ref[...] = jnp.zeros_like(carry_ref)

    hn = _rms(x_ref[...], g_ref[...]).astype(BF16)
    logits = _dot(hn, wr_ref[...])
    tm = logits.shape[0]
    lane = lax.broadcasted_iota(jnp.int32, logits.shape, 1).astype(F32)
    logits = jnp.where(lane < N_EXPERTS, logits, -jnp.inf)
    v1 = jnp.max(logits, axis=1, keepdims=True)
    i1 = jnp.min(jnp.where(logits == v1, lane, float(LANES)), axis=1, keepdims=True)
    rest = jnp.where(lane == i1, -jnp.inf, logits)
    v2 = jnp.max(rest, axis=1, keepdims=True)
    i2 = jnp.min(jnp.where(rest == v2, lane, float(LANES)), axis=1, keepdims=True)
    e2 = jnp.exp(v2 - v1)
    p1 = 1.0 / (1.0 + e2)
    p2 = e2 * p1

    sel = jnp.where((lane == i1) | (lane == i2), 1.0, 0.0)
    t_row = lax.broadcasted_iota(jnp.int32, (tm, tm), 0)
    t_col = lax.broadcasted_iota(jnp.int32, (tm, tm), 1)
    earlier = jnp.where(t_col < t_row, 1.0, 0.0).astype(BF16)
    before = _dot(earlier, sel.astype(BF16)) + carry_ref[0:1, :]
    rank1 = jnp.sum(jnp.where(lane == i1, before, 0.0), axis=1, keepdims=True)
    rank2 = jnp.sum(jnp.where(lane == i2, before, 0.0), axis=1, keepdims=True)
    total = carry_ref[0:1, :] + jnp.sum(sel, axis=0, keepdims=True)
    carry_ref[...] = jnp.broadcast_to(total, carry_ref.shape)
    count_ref[...] = jnp.broadcast_to(total, count_ref.shape)

    route = jnp.zeros_like(logits)
    for col, val in ((_R_E1, i1), (_R_E2, i2), (_R_P1, p1), (_R_P2, p2),
                     (_R_RANK1, rank1), (_R_RANK2, rank2)):
        route = jnp.where(lane == float(col), val, route)
    route_ref[...] = route


def _router(x, g, wr, tm):
    m = x.shape[0]
    row = lambda i: (i, 0)
    fix = lambda i: (0, 0)
    return pl.pallas_call(
        _router_kernel,
        grid=(m // tm,),
        in_specs=[pl.BlockSpec((tm, D_MODEL), row), pl.BlockSpec((1, D_MODEL), fix),
                  pl.BlockSpec((D_MODEL, LANES), fix)],
        out_specs=[pl.BlockSpec((tm, LANES), row), pl.BlockSpec((8, LANES), fix)],
        out_shape=[jax.ShapeDtypeStruct((m, LANES), F32), jax.ShapeDtypeStruct((8, LANES), F32)],
        scratch_shapes=[pltpu.VMEM((8, LANES), F32)],
        compiler_params=_params(("arbitrary",), 32),
        name="router",
    )(x, g, wr)


def _invert_kernel(pos_ref, tab_ref):
    i = pl.program_id(0)
    ta = pos_ref.shape[-1]

    @pl.when(i == 0)
    def _():
        tab_ref[...] = jnp.full(tab_ref.shape, -1, jnp.int32)

    lane = lax.broadcasted_iota(jnp.int32, (1, LANES), 1)
    unroll = 8

    def body(jj, carry):
        for u in range(unroll):
            j = jj * unroll + u
            p = pos_ref[0, 0, j]
            row = lax.shift_right_logical(p, 7)
            pltpu.store(tab_ref.at[pl.ds(row, 1), :], jnp.broadcast_to(i * ta + j, (1, LANES)),
                        mask=lane == (p & (LANES - 1)))
        return carry

    lax.fori_loop(0, ta // unroll, body, 0)


def _invert(pos, n_rows, ta):
    n = pos.shape[0]
    assert n % ta == 0 and n_rows % LANES == 0
    return pl.pallas_call(
        _invert_kernel,
        grid=(n // ta,),
        in_specs=[pl.BlockSpec((1, 1, ta), lambda i: (i, 0, 0), memory_space=pltpu.SMEM)],
        out_specs=pl.BlockSpec((n_rows // LANES, LANES), lambda i: (0, 0)),
        out_shape=jax.ShapeDtypeStruct((n_rows // LANES, LANES), jnp.int32),
        compiler_params=_params(("arbitrary",), 16),
        name="moe_invert",
    )(pos.reshape(n // ta, 1, ta)).reshape(n_rows)


def _experts_kernel(te_ref, nused_ref, src0_ref, src_next_ref, dst_prev_ref, x_hbm, g_ref,
                    wg_ref, wu_ref, wd_ref, y_hbm, xbuf, acc_ref, hn_ref, gsem, ssem, zsem,
                    *, n_real, n_trash):
    del te_ref
    t = pl.program_id(0)
    f = pl.program_id(1)
    nused = nused_ref[0]
    tm = hn_ref.shape[0]
    chunk = tm // (D_FF // wg_ref.shape[1])
    slot = t % 2
    other = 1 - slot
    row0 = f * chunk

    def gather_row(idx_ref, j, s):
        return pltpu.make_async_copy(x_hbm.at[pl.ds(idx_ref[0, 0, j], 1)],
                                     xbuf.at[s, pl.ds(j, 1)], gsem.at[s])

    def scatter_row(j, s):
        return pltpu.make_async_copy(acc_ref.at[s, pl.ds(j, 1)],
                                     y_hbm.at[pl.ds(dst_prev_ref[0, 0, j], 1)], ssem.at[s])

    def issue_gather():
        for j in range(chunk):
            gather_row(src_next_ref, row0 + j, other).start()

    def issue_scatter():
        for j in range(chunk):
            scatter_row(row0 + j, other).start()

    def compute():
        hn = hn_ref[...]
        gate = _dot(hn, wg_ref[...])
        up = _dot(hn, wu_ref[...])
        act = (gate * _sigmoid(gate) * up).astype(BF16)
        acc_ref[slot] += _dot(act, wd_ref[...])

    @pl.when(f == 0)
    def _():
        @pl.when(t == 0)
        def _():
            acc_ref[1] = jnp.zeros(acc_ref.shape[1:], F32)
            fills = [pltpu.make_async_copy(acc_ref.at[1], y_hbm.at[pl.ds(n_real + k * tm, tm)], zsem)
                     for k in range(n_trash // tm)]
            for c in fills:
                c.start()
            for c in fills:
                c.wait()

            def body(jj, carry):
                for u in range(8):
                    gather_row(src0_ref, jj * 8 + u, 0).start()
                return carry

            lax.fori_loop(0, tm // 8, body, 0)

        @pl.when(t <= nused)
        def _():
            pltpu.make_async_copy(x_hbm.at[pl.ds(0, tm)], xbuf.at[slot], gsem.at[slot]).wait()

        @pl.when((t >= 2) & (t - 2 < nused))
        def _():
            pltpu.make_async_copy(acc_ref.at[slot], y_hbm.at[pl.ds(0, tm)], ssem.at[slot]).wait()

        @pl.when(t < nused)
        def _():
            hn_ref[...] = _rms(xbuf[slot], g_ref[...]).astype(BF16)
            acc_ref[slot] = jnp.zeros(acc_ref.shape[1:], F32)

    @pl.when(t == 0)
    def _():
        issue_gather()
        compute()

    @pl.when((t >= 1) & (t < nused))
    def _():
        issue_gather()
        issue_scatter()
        compute()

    @pl.when(t == nused)
    def _():
        issue_scatter()


def _experts(tile_expert, nused, src, dst, x, g, wg, wu, wd, tm, tf):
    m = x.shape[0]
    ntile = src.shape[0]
    n_trash = N_EXPERTS * tm
    nf = D_FF // tf
    assert tm % nf == 0 and D_FF % tf == 0
    idx = lambda im: pl.BlockSpec((1, 1, tm), im, memory_space=pltpu.SMEM)
    any_spec = pl.BlockSpec(memory_space=pl.ANY)
    kern = functools.partial(_experts_kernel, n_real=2 * m, n_trash=n_trash)
    return pl.pallas_call(
        kern,
        grid_spec=pltpu.PrefetchScalarGridSpec(
            num_scalar_prefetch=2,
            grid=(ntile, nf),
            in_specs=[
                idx(lambda t, f, te, nu: (0, 0, 0)),
                idx(lambda t, f, te, nu: (jnp.minimum(t + 1, ntile - 1), 0, 0)),
                idx(lambda t, f, te, nu: (jnp.maximum(t - 1, 0), 0, 0)),
                any_spec,
                pl.BlockSpec((1, D_MODEL), lambda t, f, te, nu: (0, 0)),
                pl.BlockSpec((None, D_MODEL, tf), lambda t, f, te, nu: (te[t], 0, f)),
                pl.BlockSpec((None, D_MODEL, tf), lambda t, f, te, nu: (te[t], 0, f)),
                pl.BlockSpec((None, tf, D_MODEL), lambda t, f, te, nu: (te[t], f, 0)),
            ],
            out_specs=any_spec,
            scratch_shapes=[pltpu.VMEM((2, tm, D_MODEL), F32), pltpu.VMEM((2, tm, D_MODEL), F32),
                            pltpu.VMEM((tm, D_MODEL), BF16), pltpu.SemaphoreType.DMA((2,)),
                            pltpu.SemaphoreType.DMA((2,)), pltpu.SemaphoreType.DMA(())],
        ),
        out_shape=jax.ShapeDtypeStruct((2 * m + n_trash, D_MODEL), F32),
        compiler_params=_params(("arbitrary", "arbitrary"), 48),
        name="moe_experts",
    )(tile_expert, nused, src, src, dst, x, g, wg, wu, wd)


def _combine_kernel(x_ref, route_ref, y1_ref, y2_ref, out_ref):
    route = route_ref[...]
    p1 = route[:, _R_P1:_R_P1 + 1]
    p2 = route[:, _R_P2:_R_P2 + 1]
    out_ref[...] = x_ref[...] + (p1 * y1_ref[...] + p2 * y2_ref[...])


def _combine(x, route, y, tc):
    m = x.shape[0]
    row = lambda i: (i, 0)
    blk = pl.BlockSpec((tc, D_MODEL), row)
    return pl.pallas_call(
        _combine_kernel,
        grid=(m // tc,),
        in_specs=[blk, pl.BlockSpec((tc, LANES), row), blk,
                  pl.BlockSpec((tc, D_MODEL), lambda i: (i + m // tc, 0))],
        out_specs=blk,
        out_shape=jax.ShapeDtypeStruct((m, D_MODEL), F32),
        compiler_params=_params(("parallel",), 32),
        name="moe_combine",
    )(x, route, y, y)


def _moe(x, g, wr, wg, wu, wd, tm_route, tm, tf):
    m = x.shape[0]
    route, counts = _router(x, g, wr, tm_route)
    e1 = route[:, _R_E1].astype(jnp.int32)
    e2 = route[:, _R_E2].astype(jnp.int32)
    n_e = counts[0, :N_EXPERTS].astype(jnp.int32)
    tiles_e = (n_e + tm - 1) // tm
    tile_end = jnp.cumsum(tiles_e)
    row_start = (tile_end - tiles_e) * tm
    pos = jnp.concatenate([row_start[e1] + route[:, _R_RANK1].astype(jnp.int32),
                           row_start[e2] + route[:, _R_RANK2].astype(jnp.int32)])
    ntile = 2 * m // tm + N_EXPERTS + 2
    nused = tile_end[-1:]
    tile_ids = jnp.minimum(jnp.arange(ntile, dtype=jnp.int32), nused - 1)
    tile_expert = jnp.sum(tile_ids[:, None] >= tile_end[None, :], axis=1).astype(jnp.int32)
    rows = jnp.arange(ntile * tm, dtype=jnp.int32)
    spare = 2 * m + rows - jnp.cumsum(n_e)[jnp.repeat(tile_expert, tm)]
    spare = jnp.clip(spare, 2 * m, 2 * m + N_EXPERTS * tm - 1)
    assign = _invert(pos, ntile * tm, tm_route)
    src = jnp.where(assign >= m, assign - m, jnp.maximum(assign, 0))
    dst = jnp.where(assign >= 0, assign, spare)
    y = _experts(tile_expert, nused.astype(jnp.int32), src.reshape(ntile, 1, tm),
                 dst.reshape(ntile, 1, tm), x, g, wg, wu, wd, tm, tf)
    return _combine(x, route, y, tm_route)


def _mlstm_gate_perm():
    perm = []
    for pair in range(M_HEADS // 2):
        for hh in range(2):
            for gtype in range(4):
                perm.append(gtype * M_HEADS + 2 * pair + hh)
    return jnp.array(perm, jnp.int32)


def kernel(x_prompt, x_sample, norm_mix, norm_ffn, mlstm_w_in, mlstm_gate_bias, mlstm_head_gain,
           mlstm_w_out, attn_w_in, attn_q_gain, attn_k_gain, attn_sink, attn_w_out, ffn_w_gate,
           ffn_w_up, ffn_w_down, moe_w_router, moe_w_gate, moe_w_up, moe_w_down):
    b1, s1, _ = x_prompt.shape
    b2, s2, _ = x_sample.shape
    m1 = b1 * s1
    x = jnp.concatenate([x_prompt.reshape(m1, D_MODEL), x_sample.reshape(b2 * s2, D_MODEL)], axis=0)
    m = x.shape[0]
    seq_gcd = math.gcd(s1, s2)
    tm = _tile(seq_gcd, 512)
    rb = _tile(seq_gcd, 512)
    tm_ffn = _tile(m, 1024)
    tm_moe = _tile(2 * m, 1024)
    tf = 512
    tf_moe = 896
    depth = norm_mix.shape[0]
    vec = lambda a: a.astype(F32).reshape(1, -1)
    nq = M_HEADS * M_QK
    nv = M_HEADS * M_V

    tabs = _rope_tables(max(s1, s2), tm)
    group_mean = jnp.where((jnp.arange(LANES)[:, None] // A_HD) == (jnp.arange(LANES)[None, :] // A_HD),
                           1.0 / A_HD, 0.0).astype(BF16)
    gate_perm = _mlstm_gate_perm()

    for i in range(depth):
        j = i // 2
        if i % 2 == 0:
            w_in = mlstm_w_in[j]
            wqT = w_in[:, :nq].T.astype(BF16)
            wk = w_in[:, nq:2 * nq].astype(BF16)
            wv = w_in[:, 2 * nq:2 * nq + nv].astype(BF16)
            wo = w_in[:, 2 * nq + nv:2 * nq + 2 * nv].astype(BF16)
            wgT = w_in[:, 2 * nq + 2 * nv:][:, gate_perm].T.astype(BF16)
            bgT = mlstm_gate_bias[j].astype(F32)[gate_perm].reshape(-1, 1)
            qT, kT, k, v, o, gT = _mlstm_in(x, vec(norm_mix[i]), wqT, wk.T, wk, wv, wo, wgT, bgT, tm)
            hf, hb = _mlstm_core(qT, kT, k, v, gT, rb, m1, s1, s2)
            x = _mlstm_out(hf, hb, o, vec(mlstm_head_gain[j]), mlstm_w_out[j].astype(BF16), x, tm)
            x = _ffn(x, vec(norm_ffn[i]), ffn_w_gate[j].astype(BF16), ffn_w_up[j].astype(BF16),
                     ffn_w_down[j].astype(BF16), tm_ffn, tf)
        else:
            w_in = attn_w_in[j]
            a_q = A_Q_HEADS * A_HD
            a_kv = A_KV_HEADS * A_HD
            qg = jnp.tile(attn_q_gain[j].astype(F32), LANES // A_HD).reshape(1, LANES)
            kg = jnp.tile(attn_k_gain[j].astype(F32), LANES // A_HD).reshape(1, LANES)
            q, k, v = _attn_in(x, vec(norm_mix[i]), w_in[:, :a_q].astype(BF16),
                               w_in[:, a_q:a_q + a_kv].astype(BF16), w_in[:, a_q + a_kv:].astype(BF16),
                               qg, kg, group_mean, tabs, tm, m1, s1, s2)
            att = _attn(attn_sink[j].astype(F32), q, k, v, m1, s1, s2)
            x = _proj_res(att, attn_w_out[j].astype(BF16), x, tm)
            wr = jnp.pad(moe_w_router[j], ((0, 0), (0, LANES - N_EXPERTS))).astype(BF16)
            x = _moe(x, vec(norm_ffn[i]), wr, moe_w_gate[j].astype(BF16), moe_w_up[j].astype(BF16),
                     moe_w_down[j].astype(BF16), tm, tm_moe, tf_moe)

    return x[:m1].reshape(b1, s1, D_MODEL), x[m1:].reshape(b2, s2, D_MODEL)
```

```python
import functools
import math

import jax
import jax.numpy as jnp
from jax import lax
from jax.experimental import pallas as pl
from jax.experimental.pallas import tpu as pltpu

F32 = jnp.float32
BF16 = jnp.bfloat16

D_MODEL = 1024
EPS = 1e-6
M_HEADS = 8
M_QK = 64
M_V = 128
M_CHUNK = 128
A_Q_HEADS = 16
A_KV_HEADS = 4
A_HD = 64
A_GROUP = A_Q_HEADS // A_KV_HEADS
A_BLOCK = 128
WINDOW = 128
ROPE_THETA = 500000.0
ROPE_DIM = 16
D_FF = 3584
N_EXPERTS = 8

LANES = 128
VMEM_BYTES_V7X = 64 * 1024 * 1024

_NT = (((1,), (1,)), ((), ()))


def _params(semantics, vmem_mb):
    assert vmem_mb * 1024 * 1024 < VMEM_BYTES_V7X
    return pltpu.CompilerParams(dimension_semantics=semantics,
                                vmem_limit_bytes=vmem_mb * 1024 * 1024)


def _tile(total, pref):
    t = min(pref, total)
    t -= t % LANES
    while total % t:
        t -= LANES
    return t


def _rms(x, g):
    ms = jnp.mean(x * x, axis=-1, keepdims=True)
    return x * lax.rsqrt(ms + EPS) * g


def _sigmoid(x):
    return 1.0 / (1.0 + jnp.exp(-x))


def _dot(a, b):
    return jnp.dot(a, b, preferred_element_type=F32)


def _seq_pos(row0, m1, s1, s2):
    first = row0 < m1
    pos = jnp.where(first, row0 % s1, (row0 - m1) % s2)
    return pos, jnp.where(first, s1, s2)


def _pair_specs(tm, na):
    return [pl.BlockSpec((tm, D_MODEL), lambda i: (jnp.minimum(i, na - 1), 0)),
            pl.BlockSpec((tm, D_MODEL), lambda i: (jnp.maximum(i - na, 0), 0))]


def _pair_tile(xa_ref, xb_ref, na):
    return jnp.where(pl.program_id(0) < na, xa_ref[...], xb_ref[...])


def _as_pair(x, tm):
    if isinstance(x, tuple):
        return x[0], x[1], x[0].shape[0] // tm
    return x, x, x.shape[0] // tm


def _mlstm_in_kernel(xa_ref, xb_ref, g_ref, wqT_ref, wkT_ref, wk_ref, wv_ref, wo_ref, wgT_ref, bgT_ref,
                     qT_ref, kT_ref, k_ref, v_ref, o_ref, st_ref, *, na):
    hn = _rms(_pair_tile(xa_ref, xb_ref, na), g_ref[...]).astype(BF16)
    gT = lax.dot_general(wgT_ref[...], hn, _NT, preferred_element_type=F32) + bgT_ref[...]
    log_sig = jnp.minimum(gT, 0.0) - jnp.log1p(jnp.exp(-jnp.abs(gT)))
    row = lax.broadcasted_iota(jnp.int32, gT.shape, 0)
    gT = jnp.where(row % 2 == 1, log_sig, gT)
    nchunk = kT_ref.shape[0]
    for j in range(nchunk):
        cols = slice(j * M_CHUNK, (j + 1) * M_CHUNK)
        for pair in range(M_HEADS // 2):
            stats = _mlstm_gate_stats(gT[pair * 8:(pair + 1) * 8, cols])
            st_ref[j, pair * _N_STATS * 8:(pair + 1) * _N_STATS * 8, :] = jnp.concatenate(stats, axis=0)
    k_ref[...] = _dot(hn, wk_ref[...]).astype(BF16)
    v_ref[...] = _dot(hn, wv_ref[...]).astype(BF16)
    o_ref[...] = _dot(hn, wo_ref[...]).astype(BF16)
    qT = lax.dot_general(wqT_ref[...], hn, _NT, preferred_element_type=F32) * (M_QK ** -0.5)
    qT = qT.astype(BF16)
    kT = lax.dot_general(wkT_ref[...], hn, _NT, preferred_element_type=F32).astype(BF16)
    for j in range(nchunk):
        cols = slice(j * M_CHUNK, (j + 1) * M_CHUNK)
        qT_ref[j] = qT[:, cols]
        kT_ref[j] = kT[:, cols]


def _mlstm_in(x, g, wqT, wkT, wk, wv, wo, wgT, bgT, tm):
    xa, xb, na = _as_pair(x, tm)
    m = xa.shape[0] + xb.shape[0] if isinstance(x, tuple) else xa.shape[0]
    nq = M_HEADS * M_QK
    nv = M_HEADS * M_V
    ng = 4 * M_HEADS
    row = lambda i: (i, 0)
    fix = lambda i: (0, 0)
    chunked = lambda n: pl.BlockSpec((tm // M_CHUNK, n, M_CHUNK), lambda i: (i, 0, 0))
    chunked_shape = lambda n, dt: jax.ShapeDtypeStruct((m // M_CHUNK, n, M_CHUNK), dt)
    return pl.pallas_call(
        functools.partial(_mlstm_in_kernel, na=na),
        grid=(m // tm,),
        in_specs=_pair_specs(tm, na) + [
            pl.BlockSpec((1, D_MODEL), fix),
            pl.BlockSpec((nq, D_MODEL), fix),
            pl.BlockSpec((nq, D_MODEL), fix),
            pl.BlockSpec((D_MODEL, nq), fix),
            pl.BlockSpec((D_MODEL, nv), fix),
            pl.BlockSpec((D_MODEL, nv), fix),
            pl.BlockSpec((ng, D_MODEL), fix),
            pl.BlockSpec((ng, 1), fix),
        ],
        out_specs=[chunked(nq), chunked(nq), pl.BlockSpec((tm, nq), row),
                   pl.BlockSpec((tm, nv), row), pl.BlockSpec((tm, nv), row), chunked(_N_STATS * ng)],
        out_shape=[chunked_shape(nq, BF16), chunked_shape(nq, BF16),
                   jax.ShapeDtypeStruct((m, nq), BF16), jax.ShapeDtypeStruct((m, nv), BF16),
                   jax.ShapeDtypeStruct((m, nv), BF16), chunked_shape(_N_STATS * ng, F32)],
        compiler_params=_params(("parallel",), 40),
        name="mlstm_in",
    )(xa, xb, g, wqT, wkT, wk, wv, wo, wgT, bgT)


_N_STATS = 6


def _lane_scan(x, combine, identity, reverse):
    n = x.shape[-1]
    lane = lax.broadcasted_iota(jnp.int32, x.shape, 1)
    sh = 1
    while sh < n:
        if reverse:
            moved = jnp.where(lane < n - sh, pltpu.roll(x, n - sh, 1), identity)
        else:
            moved = jnp.where(lane >= sh, pltpu.roll(x, sh, 1), identity)
        x = combine(x, moved)
        sh *= 2
    return x


def _mlstm_gate_stats(gates):
    row = lax.broadcasted_iota(jnp.int32, gates.shape, 0)
    bwd_row = (row % 4) >= 2
    add = lambda x, y: x + y
    cum = jnp.where(bwd_row, _lane_scan(gates, add, 0.0, True), _lane_scan(gates, add, 0.0, False))
    b = pltpu.roll(cum, 7, 0)
    total = jnp.broadcast_to(jnp.sum(gates, axis=1, keepdims=True), gates.shape)
    g = pltpu.roll(total, 7, 0)
    r = gates - b
    pm = jnp.where(bwd_row, _lane_scan(r, jnp.maximum, -jnp.inf, True),
                   _lane_scan(r, jnp.maximum, -jnp.inf, False))
    a = g + r
    a_max = jnp.broadcast_to(jnp.max(a, axis=1, keepdims=True), gates.shape)
    return b, r, pm, g, a_max, jnp.exp(a - a_max)


def _mlstm_unit_pre(qT, kT, k, v2, stats, reverse):
    L = M_CHUNK
    s_idx = lax.broadcasted_iota(jnp.int32, (L, L), 0)
    t_idx = lax.broadcasted_iota(jnp.int32, (L, L), 1)
    keep = (s_idx >= t_idx) if reverse else (s_idx <= t_idx)
    head_a_lanes = lax.broadcasted_iota(jnp.int32, (L, LANES), 1) < M_QK
    head_a_rows = lax.broadcasted_iota(jnp.int32, (LANES, L), 0) < M_QK
    zeros_k = jnp.zeros_like(k)
    k2 = jnp.concatenate([jnp.where(head_a_lanes, k, zeros_k), jnp.where(head_a_lanes, zeros_k, k)],
                         axis=0)
    s2 = _dot(k2, qT)
    sT, den0 = [], []
    for hd in range(2):
        _, r, pm, _, _, _ = stats[hd]
        r_col = jnp.broadcast_to(r, (L, L)).T
        sT.append(s2[hd * L:(hd + 1) * L] * jnp.exp(jnp.where(keep, r_col - pm, -jnp.inf)))
        den0.append(jnp.sum(sT[hd], axis=0, keepdims=True))
    w = jnp.where(head_a_rows, stats[0][5], stats[1][5])
    ones = jnp.ones((L, M_V), BF16)
    kv = _dot((kT.astype(F32) * w).astype(BF16), jnp.concatenate([v2, ones], axis=1))
    col = lax.broadcasted_iota(jnp.int32, kv.shape, 1)
    row_a = lax.broadcasted_iota(jnp.int32, kv.shape, 0) < M_QK
    other_v = (row_a & (col >= M_V) & (col < 2 * M_V)) | (~row_a & (col < M_V))
    return sT, den0, jnp.where(other_v, 0.0, kv)


def _mlstm_unit_post(qT, v2, stats, pre, c_pair, m_rows):
    L = M_CHUNK
    sT, den0, kv = pre
    head_a_rows = lax.broadcasted_iota(jnp.int32, (LANES, L), 0) < M_QK
    row_a = lax.broadcasted_iota(jnp.int32, kv.shape, 0) < M_QK
    qT_f = qT.astype(F32)
    qn_all = qT_f * c_pair[:, 2 * M_V:]
    s_scale, q_scale, m_new, decay, scale = [], [], [], [], []
    for hd in range(2):
        b, _, pm, g, a_max, _ = stats[hd]
        m_row = m_rows[hd]
        u = jnp.maximum(pm, m_row)
        c1 = jnp.exp(pm - u)
        e = jnp.exp(m_row - u)
        qn = jnp.sum(qn_all[hd * M_QK:(hd + 1) * M_QK], axis=0, keepdims=True)
        den = c1 * den0[hd] + e * qn
        inv = 1.0 / jnp.maximum(jnp.abs(den), jnp.exp(-(b + u)))
        s_scale.append(c1 * inv)
        q_scale.append(e * inv)
        m_new.append(jnp.maximum(g + m_row, a_max))
        decay.append(jnp.exp(g + m_row - m_new[hd])[:, 0:1])
        scale.append(jnp.exp(a_max - m_new[hd])[:, 0:1])
    lhsT = jnp.concatenate([sT[0] * s_scale[0], sT[1] * s_scale[1],
                            qT_f * jnp.where(head_a_rows, q_scale[0], q_scale[1])],
                           axis=0).astype(BF16)
    zeros_v = jnp.zeros((L, M_V), BF16)
    rhs = jnp.concatenate([jnp.concatenate([v2[:, :M_V], zeros_v], axis=1),
                           jnp.concatenate([zeros_v, v2[:, M_V:]], axis=1),
                           c_pair[:, :2 * M_V].astype(BF16)], axis=0)
    h = lax.dot_general(lhsT, rhs, (((0,), (0,)), ((), ())), preferred_element_type=F32)
    c_new = jnp.where(row_a, decay[0], decay[1]) * c_pair + jnp.where(row_a, scale[0], scale[1]) * kv
    return h, c_new, m_new


def _mlstm_core_kernel(qTf_ref, kTf_ref, kf_ref, vf_ref, sf_ref, qTb_ref, kTb_ref, kb_ref, vb_ref,
                       sb_ref, hf_ref, hb_ref, c_ref, m_ref, *, nchunk, rb, m1, s1, s2):
    i = pl.program_id(1)
    nblk = pl.num_programs(1)
    pos_f, _ = _seq_pos(i * rb, m1, s1, s2)
    pos_b, len_b = _seq_pos((nblk - 1 - i) * rb, m1, s1, s2)

    def reset(dirn):
        c_ref[dirn] = jnp.zeros(c_ref.shape[1:], F32)
        for hd in range(2):
            m_ref[hd * 2 + dirn] = jnp.zeros(m_ref.shape[1:], F32)

    pl.when(pos_f == 0)(lambda: reset(0))
    pl.when(pos_b + rb == len_b)(lambda: reset(1))

    c_state = [c_ref[dirn] for dirn in range(2)]
    m_state = [[m_ref[hd * 2 + dirn][0:1] for hd in range(2)] for dirn in range(2)]
    refs = ((qTf_ref, kTf_ref, kf_ref, vf_ref, sf_ref, hf_ref),
            (qTb_ref, kTb_ref, kb_ref, vb_ref, sb_ref, hb_ref))

    def unit(step, dirn):
        qT_ref, kT_ref, k_ref, v_ref, st_ref, out_ref = refs[dirn]
        cc = nchunk - 1 - step if dirn else step
        rows = slice(cc * M_CHUNK, (cc + 1) * M_CHUNK)
        st = st_ref[cc]
        stats = [tuple(st[n * 8 + hd * 4 + dirn * 2:n * 8 + hd * 4 + dirn * 2 + 1]
                       for n in range(_N_STATS)) for hd in range(2)]
        return qT_ref[cc], kT_ref[cc], k_ref[rows, :], v_ref[rows, :], stats, out_ref, rows

    pre = {}
    for step in range(nchunk):
        for dirn in range(2):
            qT, kT, k, v2, stats, _, _ = unit(step, dirn)
            pre[step, dirn] = _mlstm_unit_pre(qT, kT, k, v2, stats, reverse=bool(dirn))
    for step in range(nchunk):
        for dirn in range(2):
            qT, _, _, v2, stats, out_ref, rows = unit(step, dirn)
            h, c_state[dirn], m_state[dirn] = _mlstm_unit_post(
                qT, v2, stats, pre[step, dirn], c_state[dirn], m_state[dirn])
            out_ref[rows, :] = h

    for dirn in range(2):
        c_ref[dirn] = c_state[dirn]
        for hd in range(2):
            m_ref[hd * 2 + dirn] = jnp.broadcast_to(m_state[dirn][hd], m_ref.shape[1:])


def _mlstm_core(qT, kT, k, v, gT, rb, m1, s1, s2):
    m = k.shape[0]
    nblk = m // rb
    nchunk = rb // M_CHUNK
    npair = M_HEADS // 2
    fwd = lambda p, i: (i, p)
    bwd = lambda p, i: (nblk - 1 - i, p)
    fwd3 = lambda p, i: (i, p, 0)
    bwd3 = lambda p, i: (nblk - 1 - i, p, 0)

    def specs(im2, im3):
        return [
            pl.BlockSpec((nchunk, LANES, M_CHUNK), im3),
            pl.BlockSpec((nchunk, LANES, M_CHUNK), im3),
            pl.BlockSpec((rb, LANES), im2),
            pl.BlockSpec((rb, 2 * M_V), im2),
            pl.BlockSpec((nchunk, _N_STATS * 8, M_CHUNK), im3),
        ]

    kern = functools.partial(_mlstm_core_kernel, nchunk=nchunk, rb=rb, m1=m1, s1=s1, s2=s2)
    return pl.pallas_call(
        kern,
        grid=(npair, nblk),
        in_specs=specs(fwd, fwd3) + specs(bwd, bwd3),
        out_specs=[pl.BlockSpec((rb, 2 * M_V), fwd), pl.BlockSpec((rb, 2 * M_V), bwd)],
        out_shape=[jax.ShapeDtypeStruct((m, M_HEADS * M_V), F32)] * 2,
        scratch_shapes=[pltpu.VMEM((2, LANES, 3 * M_V), F32), pltpu.VMEM((4, 8, M_CHUNK), F32)],
        compiler_params=_params(("parallel", "arbitrary"), 32),
        name="mlstm_core",
    )(qT, kT, k, v, gT, qT, kT, k, v, gT)


def _mlstm_out_kernel(hf_ref, hb_ref, o_ref, gain_ref, w_ref, xa_ref, xb_ref, out_ref, *, na):
    h = hf_ref[...] + hb_ref[...]
    parts = []
    for hd in range(M_HEADS):
        hh = h[:, hd * M_V:(hd + 1) * M_V]
        ms = jnp.mean(hh * hh, axis=-1, keepdims=True)
        parts.append(hh * lax.rsqrt(ms + EPS))
    hn = jnp.concatenate(parts, axis=1) * gain_ref[...]
    hg = (_sigmoid(o_ref[...].astype(F32)) * hn).astype(BF16)
    out_ref[...] = _pair_tile(xa_ref, xb_ref, na) + _dot(hg, w_ref[...])


def _mlstm_out(hf, hb, o, gain, w, x, tm):
    xa, xb, na = _as_pair(x, tm)
    m = hf.shape[0]
    row = lambda i: (i, 0)
    fix = lambda i: (0, 0)
    blk = pl.BlockSpec((tm, D_MODEL), row)
    return pl.pallas_call(
        functools.partial(_mlstm_out_kernel, na=na),
        grid=(m // tm,),
        in_specs=[blk, blk, blk, pl.BlockSpec((1, D_MODEL), fix),
                  pl.BlockSpec((D_MODEL, D_MODEL), fix)] + _pair_specs(tm, na),
        out_specs=blk,
        out_shape=jax.ShapeDtypeStruct((m, D_MODEL), F32),
        compiler_params=_params(("parallel",), 40),
        name="mlstm_out",
    )(hf, hb, o, gain, w, xa, xb)


def _ffn_kernel(x_ref, g_ref, wg_ref, wu_ref, wd_ref, out_ref, hn_ref, acc_ref):
    f = pl.program_id(1)

    @pl.when(f == 0)
    def _():
        hn_ref[...] = _rms(x_ref[...], g_ref[...]).astype(BF16)
        acc_ref[...] = jnp.zeros_like(acc_ref)

    hn = hn_ref[...]
    gate = _dot(hn, wg_ref[...])
    up = _dot(hn, wu_ref[...])
    act = (gate * _sigmoid(gate) * up).astype(BF16)
    acc_ref[...] += _dot(act, wd_ref[...])

    @pl.when(f == pl.num_programs(1) - 1)
    def _():
        out_ref[...] = x_ref[...] + acc_ref[...]


def _ffn(x, g, wg, wu, wd, tm, tf):
    m = x.shape[0]
    row = lambda i, f: (i, 0)
    return pl.pallas_call(
        _ffn_kernel,
        grid=(m // tm, D_FF // tf),
        in_specs=[
            pl.BlockSpec((tm, D_MODEL), row),
            pl.BlockSpec((1, D_MODEL), lambda i, f: (0, 0)),
            pl.BlockSpec((D_MODEL, tf), lambda i, f: (0, f)),
            pl.BlockSpec((D_MODEL, tf), lambda i, f: (0, f)),
            pl.BlockSpec((tf, D_MODEL), lambda i, f: (f, 0)),
        ],
        out_specs=pl.BlockSpec((tm, D_MODEL), row),
        out_shape=jax.ShapeDtypeStruct((m, D_MODEL), F32),
        scratch_shapes=[pltpu.VMEM((tm, D_MODEL), BF16), pltpu.VMEM((tm, D_MODEL), F32)],
        compiler_params=_params(("parallel", "arbitrary"), 48),
        name="ffn",
    )(x, g, wg, wu, wd)


def _rope_table_kernel(inv_ref, ma_ref, mb_ref, cos_ref, sa_ref, sb_ref):
    rows = cos_ref.shape[0]
    pos = pl.program_id(0) * rows + lax.broadcasted_iota(jnp.int32, (rows, LANES), 0)
    ang = pos.astype(F32) * inv_ref[...]
    sin = jnp.sin(ang)
    cos_ref[...] = jnp.cos(ang)
    sa_ref[...] = sin * ma_ref[...]
    sb_ref[...] = sin * mb_ref[...]


def _rope_tables(smax, rows):
    half = ROPE_DIM // 2
    d = jnp.arange(LANES) % A_HD
    inv = ROPE_THETA ** (-(jnp.arange(half, dtype=F32) * 2.0) / ROPE_DIM)
    inv_lane = jnp.where(d < ROPE_DIM, inv[d % half], 0.0).astype(F32)[None, :]
    ma = jnp.where(d < half, -1.0, 0.0).astype(F32)[None, :]
    mb = jnp.where((d >= half) & (d < ROPE_DIM), 1.0, 0.0).astype(F32)[None, :]
    fix = lambda i: (0, 0)
    vec = pl.BlockSpec((1, LANES), fix)
    tab = pl.BlockSpec((rows, LANES), lambda i: (i, 0))
    return pl.pallas_call(
        _rope_table_kernel,
        grid=(smax // rows,),
        in_specs=[vec, vec, vec],
        out_specs=[tab, tab, tab],
        out_shape=[jax.ShapeDtypeStruct((smax, LANES), F32)] * 3,
        compiler_params=_params(("parallel",), 16),
        name="rope_tables",
    )(inv_lane, ma, mb)


def _attn_in_kernel(x_ref, g_ref, wq_ref, wk_ref, wv_ref, qg_ref, kg_ref, gm_ref,
                    cos_ref, sa_ref, sb_ref, q_ref, k_ref, v_ref):
    hn = _rms(x_ref[...], g_ref[...]).astype(BF16)
    cos = cos_ref[...]
    sa = sa_ref[...]
    sb = sb_ref[...]
    half = ROPE_DIM // 2

    def norm_rope(x, gain, scale, out_ref):
        for j in range(x.shape[1] // LANES):
            xj = x[:, j * LANES:(j + 1) * LANES]
            ms = _dot((xj * xj).astype(BF16), gm_ref[...])
            y = xj * lax.rsqrt(ms + EPS) * gain
            y = y * cos + pltpu.roll(y, LANES - half, 1) * sa + pltpu.roll(y, half, 1) * sb
            out_ref[:, j * LANES:(j + 1) * LANES] = (y * scale).astype(BF16)

    norm_rope(_dot(hn, wq_ref[...]), qg_ref[...], A_HD ** -0.5, q_ref)
    norm_rope(_dot(hn, wk_ref[...]), kg_ref[...], 1.0, k_ref)
    v_ref[...] = _dot(hn, wv_ref[...]).astype(BF16)


def _attn_in(x, g, wq, wk, wv, qg, kg, gm, tabs, tm, m1, s1, s2):
    m = x.shape[0]
    nq = A_Q_HEADS * A_HD
    nkv = A_KV_HEADS * A_HD
    row = lambda i: (i, 0)
    fix = lambda i: (0, 0)

    def tab_map(i):
        pos, _ = _seq_pos(i * tm, m1, s1, s2)
        return (pos // tm, 0)

    tab = pl.BlockSpec((tm, LANES), tab_map)
    vec = pl.BlockSpec((1, LANES), fix)
    return pl.pallas_call(
        _attn_in_kernel,
        grid=(m // tm,),
        in_specs=[
            pl.BlockSpec((tm, D_MODEL), row),
            pl.BlockSpec((1, D_MODEL), fix),
            pl.BlockSpec((D_MODEL, nq), fix),
            pl.BlockSpec((D_MODEL, nkv), fix),
            pl.BlockSpec((D_MODEL, nkv), fix),
            vec, vec,
            pl.BlockSpec((LANES, LANES), fix),
            tab, tab, tab,
        ],
        out_specs=[pl.BlockSpec((tm, nq), row), pl.BlockSpec((tm, nkv), row),
                   pl.BlockSpec((tm, nkv), row)],
        out_shape=[jax.ShapeDtypeStruct((m, nq), BF16), jax.ShapeDtypeStruct((m, nkv), BF16),
                   jax.ShapeDtypeStruct((m, nkv), BF16)],
        compiler_params=_params(("parallel",), 40),
        name="attn_in",
    )(x, g, wq, wk, wv, qg, kg, gm, *tabs)


def _attn_kernel(sink_ref, q_ref, kp_ref, kc_ref, kn_ref, vp_ref, vc_ref, vn_ref, out_ref,
                 *, m1, s1, s2):
    blk = A_BLOCK
    pos0, slen = _seq_pos(pl.program_id(0) * blk, m1, s1, s2)
    prev_ok = pos0 > 0
    next_ok = pos0 + blk < slen
    t = lax.broadcasted_iota(jnp.int32, (blk, 3 * blk), 0)
    c = lax.broadcasted_iota(jnp.int32, (blk, 3 * blk), 1)
    valid = (jnp.abs(c - blk - t) <= WINDOW) & ((c >= blk) | prev_ok) & ((c < 2 * blk) | next_ok)

    kcat = jnp.concatenate([kp_ref[...], kc_ref[...], kn_ref[...]], axis=0).astype(F32)
    vcat = jnp.concatenate([vp_ref[...], vc_ref[...], vn_ref[...]], axis=0).astype(F32)
    lane_kv = lax.broadcasted_iota(jnp.int32, (3 * blk, LANES), 1)
    lane_q = lax.broadcasted_iota(jnp.int32, (blk, LANES), 1)

    def softmax_parts(s, sink):
        s = jnp.where(valid, s, -jnp.inf)
        m = jnp.maximum(jnp.max(s, axis=1, keepdims=True), sink)
        p = jnp.exp(s - m)
        denom = jnp.sum(p, axis=1, keepdims=True) + jnp.exp(sink - m)
        return p.astype(BF16), denom

    def lo_hi(cat, h):
        tile, half = divmod(h, 2)
        own = (lane_kv < A_HD) if half == 0 else (lane_kv >= A_HD)
        mine = jnp.where(own, cat[:, tile * LANES:(tile + 1) * LANES], 0.0)
        other = pltpu.roll(mine, A_HD, 1)
        lo, hi = (mine, other) if half == 0 else (other, mine)
        return jnp.concatenate([lo, hi], axis=0).astype(BF16)

    npair = A_GROUP // 2
    kk = [lo_hi(kcat, h) for h in range(A_KV_HEADS)]
    vv = [lo_hi(vcat, h) for h in range(A_KV_HEADS)]
    scores = []
    for h in range(A_KV_HEADS):
        q_pairs = jnp.concatenate([q_ref[:, (h * npair + j) * LANES:(h * npair + j + 1) * LANES]
                                   for j in range(npair)], axis=0)
        scores.append(lax.dot_general(q_pairs, kk[h], _NT, preferred_element_type=F32))
    for h in range(A_KV_HEADS):
        p_rows, inv_rows = [], []
        for j in range(npair):
            parts = [softmax_parts(scores[h][j * blk:(j + 1) * blk, par * 3 * blk:(par + 1) * 3 * blk],
                                   sink_ref[(h * npair + j) * 2 + par]) for par in range(2)]
            p_rows.append(jnp.concatenate([parts[0][0], parts[1][0]], axis=1))
            inv_rows.append(jnp.where(lane_q < A_HD, 1.0 / parts[0][1], 1.0 / parts[1][1]))
        o = _dot(jnp.concatenate(p_rows, axis=0), vv[h])
        for j in range(npair):
            grp = h * npair + j
            out_ref[:, grp * LANES:(grp + 1) * LANES] = (o[j * blk:(j + 1) * blk] * inv_rows[j]).astype(BF16)


def _attn(sink, q, k, v, m1, s1, s2):
    m = q.shape[0]
    blk = A_BLOCK
    nblk = m // blk
    nkv = A_KV_HEADS * A_HD
    cur = lambda i: (i, 0)
    prev = lambda i: (jnp.maximum(i - 1, 0), 0)
    nxt = lambda i: (jnp.minimum(i + 1, nblk - 1), 0)
    kv = lambda im: pl.BlockSpec((blk, nkv), im)
    kern = functools.partial(_attn_kernel, m1=m1, s1=s1, s2=s2)
    return pl.pallas_call(
        kern,
        grid=(nblk,),
        in_specs=[pl.BlockSpec(memory_space=pltpu.SMEM),
                  pl.BlockSpec((blk, D_MODEL), cur), kv(prev), kv(cur), kv(nxt),
                  kv(prev), kv(cur), kv(nxt)],
        out_specs=pl.BlockSpec((blk, D_MODEL), cur),
        out_shape=jax.ShapeDtypeStruct((m, D_MODEL), BF16),
        compiler_params=_params(("parallel",), 32),
        name="attn",
    )(sink, q, k, k, k, v, v, v)


def _proj_res_kernel(a_ref, w_ref, x_ref, out_ref):
    out_ref[...] = x_ref[...] + _dot(a_ref[...], w_ref[...])


def _proj_res(a, w, x, tm):
    m = x.shape[0]
    row = lambda i: (i, 0)
    blk = pl.BlockSpec((tm, D_MODEL), row)
    return pl.pallas_call(
        _proj_res_kernel,
        grid=(m // tm,),
        in_specs=[blk, pl.BlockSpec((D_MODEL, D_MODEL), lambda i: (0, 0)), blk],
        out_specs=blk,
        out_shape=jax.ShapeDtypeStruct((m, D_MODEL), F32),
        compiler_params=_params(("parallel",), 32),
        name="attn_out",
    )(a, w, x)


_R_E1, _R_E2, _R_P1, _R_P2, _R_RANK1, _R_RANK2 = range(6)


def _router_kernel(x_ref, g_ref, wr_ref, route_ref, count_ref, carry_ref):
    @pl.when(pl.program_id(0) == 0)
    def _():
        carry_ref[...] = jnp.zeros_like(carry_ref)

    hn = _rms(x_ref[...], g_ref[...]).astype(BF16)
    logits = _dot(hn, wr_ref[...])
    tm = logits.shape[0]
    lane = lax.broadcasted_iota(jnp.int32, logits.shape, 1).astype(F32)
    logits = jnp.where(lane < N_EXPERTS, logits, -jnp.inf)
    v1 = jnp.max(logits, axis=1, keepdims=True)
    i1 = jnp.min(jnp.where(logits == v1, lane, float(LANES)), axis=1, keepdims=True)
    rest = jnp.where(lane == i1, -jnp.inf, logits)
    v2 = jnp.max(rest, axis=1, keepdims=True)
    i2 = jnp.min(jnp.where(rest == v2, lane, float(LANES)), axis=1, keepdims=True)
    e2 = jnp.exp(v2 - v1)
    p1 = 1.0 / (1.0 + e2)
    p2 = e2 * p1

    sel = jnp.where((lane == i1) | (lane == i2), 1.0, 0.0)
    t_row = lax.broadcasted_iota(jnp.int32, (tm, tm), 0)
    t_col = lax.broadcasted_iota(jnp.int32, (tm, tm), 1)
    earlier = jnp.where(t_col < t_row, 1.0, 0.0).astype(BF16)
    before = _dot(earlier, sel.astype(BF16)) + carry_ref[0:1, :]
    rank1 = jnp.sum(jnp.where(lane == i1, before, 0.0), axis=1, keepdims=True)
    rank2 = jnp.sum(jnp.where(lane == i2, before, 0.0), axis=1, keepdims=True)
    total = carry_ref[0:1, :] + jnp.sum(sel, axis=0, keepdims=True)
    carry_ref[...] = jnp.broadcast_to(total, carry_ref.shape)
    count_ref[...] = jnp.broadcast_to(total, count_ref.shape)

    route = jnp.zeros_like(logits)
    for col, val in ((_R_E1, i1), (_R_E2, i2), (_R_P1, p1), (_R_P2, p2),
                     (_R_RANK1, rank1), (_R_RANK2, rank2)):
        route = jnp.where(lane == float(col), val, route)
    route_ref[...] = route


def _router(x, g, wr, tm):
    m = x.shape[0]
    row = lambda i: (i, 0)
    fix = lambda i: (0, 0)
    return pl.pallas_call(
        _router_kernel,
        grid=(m // tm,),
        in_specs=[pl.BlockSpec((tm, D_MODEL), row), pl.BlockSpec((1, D_MODEL), fix),
                  pl.BlockSpec((D_MODEL, LANES), fix)],
        out_specs=[pl.BlockSpec((tm, LANES), row), pl.BlockSpec((8, LANES), fix)],
        out_shape=[jax.ShapeDtypeStruct((m, LANES), F32), jax.ShapeDtypeStruct((8, LANES), F32)],
        scratch_shapes=[pltpu.VMEM((8, LANES), F32)],
        compiler_params=_params(("arbitrary",), 32),
        name="router",
    )(x, g, wr)


def _invert_kernel(pos_ref, tab_ref):
    i = pl.program_id(0)
    ta = pos_ref.shape[-1]

    @pl.when(i == 0)
    def _():
        tab_ref[...] = jnp.full(tab_ref.shape, -1, jnp.int32)

    lane = lax.broadcasted_iota(jnp.int32, (1, LANES), 1)
    unroll = 8

    def body(jj, carry):
        for u in range(unroll):
            j = jj * unroll + u
            p = pos_ref[0, 0, j]
            row = lax.shift_right_logical(p, 7)
            pltpu.store(tab_ref.at[pl.ds(row, 1), :], jnp.broadcast_to(i * ta + j, (1, LANES)),
                        mask=lane == (p & (LANES - 1)))
        return carry

    lax.fori_loop(0, ta // unroll, body, 0)


def _invert(pos, n_rows, ta):
    n = pos.shape[0]
    assert n % ta == 0 and n_rows % LANES == 0
    return pl.pallas_call(
        _invert_kernel,
        grid=(n // ta,),
        in_specs=[pl.BlockSpec((1, 1, ta), lambda i: (i, 0, 0), memory_space=pltpu.SMEM)],
        out_specs=pl.BlockSpec((n_rows // LANES, LANES), lambda i: (0, 0)),
        out_shape=jax.ShapeDtypeStruct((n_rows // LANES, LANES), jnp.int32),
        compiler_params=_params(("arbitrary",), 16),
        name="moe_invert",
    )(pos.reshape(n // ta, 1, ta)).reshape(n_rows)


def _experts_kernel(te_ref, nused_ref, src0_ref, src_next_ref, dst_prev_ref, x_hbm, g_ref,
                    wg_ref, wu_ref, wd_ref, y_hbm, xbuf, acc_ref, hn_ref, gsem, ssem, zsem,
                    *, n_real, n_trash):
    del te_ref
    t = pl.program_id(0)
    f = pl.program_id(1)
    nused = nused_ref[0]
    tm = hn_ref.shape[0]
    chunk = tm // (D_FF // wg_ref.shape[1])
    slot = t % 2
    other = 1 - slot
    row0 = f * chunk

    def gather_row(idx_ref, j, s):
        return pltpu.make_async_copy(x_hbm.at[pl.ds(idx_ref[0, 0, j], 1)],
                                     xbuf.at[s, pl.ds(j, 1)], gsem.at[s])

    def scatter_row(j, s):
        return pltpu.make_async_copy(acc_ref.at[s, pl.ds(j, 1)],
                                     y_hbm.at[pl.ds(dst_prev_ref[0, 0, j], 1)], ssem.at[s])

    def issue_gather():
        for j in range(chunk):
            gather_row(src_next_ref, row0 + j, other).start(priority=j % 2)

    def issue_scatter():
        for j in range(chunk):
            scatter_row(row0 + j, other).start(priority=j % 2)

    def compute():
        hn = hn_ref[...]
        gate = _dot(hn, wg_ref[...])
        up = _dot(hn, wu_ref[...])
        act = (gate * _sigmoid(gate) * up).astype(BF16)
        acc_ref[slot] += _dot(act, wd_ref[...])

    @pl.when(f == 0)
    def _():
        @pl.when(t == 0)
        def _():
            acc_ref[1] = jnp.zeros(acc_ref.shape[1:], F32)
            fills = [pltpu.make_async_copy(acc_ref.at[1], y_hbm.at[pl.ds(n_real + k * tm, tm)], zsem)
                     for k in range(n_trash // tm)]
            for c in fills:
                c.start()
            for c in fills:
                c.wait()

            def body(jj, carry):
                for u in range(8):
                    gather_row(src0_ref, jj * 8 + u, 0).start()
                return carry

            lax.fori_loop(0, tm // 8, body, 0)

        @pl.when(t <= nused)
        def _():
            pltpu.make_async_copy(x_hbm.at[pl.ds(0, tm)], xbuf.at[slot], gsem.at[slot]).wait()

        @pl.when((t >= 2) & (t - 2 < nused))
        def _():
            pltpu.make_async_copy(acc_ref.at[slot], y_hbm.at[pl.ds(0, tm)], ssem.at[slot]).wait()

        @pl.when(t < nused)
        def _():
            hn_ref[...] = _rms(xbuf[slot], g_ref[...]).astype(BF16)
            acc_ref[slot] = jnp.zeros(acc_ref.shape[1:], F32)

    @pl.when(t == 0)
    def _():
        issue_gather()
        compute()

    @pl.when((t >= 1) & (t < nused))
    def _():
        issue_gather()
        issue_scatter()
        compute()

    @pl.when(t == nused)
    def _():
        issue_scatter()


def _experts(tile_expert, nused, src, dst, x, g, wg, wu, wd, tm, tf):
    m = x.shape[0]
    ntile = src.shape[0]
    n_trash = N_EXPERTS * tm
    nf = D_FF // tf
    assert tm % nf == 0 and D_FF % tf == 0
    idx = lambda im: pl.BlockSpec((1, 1, tm), im, memory_space=pltpu.SMEM)
    any_spec = pl.BlockSpec(memory_space=pl.ANY)
    kern = functools.partial(_experts_kernel, n_real=2 * m, n_trash=n_trash)
    return pl.pallas_call(
        kern,
        grid_spec=pltpu.PrefetchScalarGridSpec(
            num_scalar_prefetch=2,
            grid=(ntile, nf),
            in_specs=[
                idx(lambda t, f, te, nu: (0, 0, 0)),
                idx(lambda t, f, te, nu: (jnp.minimum(t + 1, ntile - 1), 0, 0)),
                idx(lambda t, f, te, nu: (jnp.maximum(t - 1, 0), 0, 0)),
                any_spec,
                pl.BlockSpec((1, D_MODEL), lambda t, f, te, nu: (0, 0)),
                pl.BlockSpec((None, D_MODEL, tf), lambda t, f, te, nu: (te[t], 0, f)),
                pl.BlockSpec((None, D_MODEL, tf), lambda t, f, te, nu: (te[t], 0, f)),
                pl.BlockSpec((None, tf, D_MODEL), lambda t, f, te, nu: (te[t], f, 0)),
            ],
            out_specs=any_spec,
            scratch_shapes=[pltpu.VMEM((2, tm, D_MODEL), F32), pltpu.VMEM((2, tm, D_MODEL), F32),
                            pltpu.VMEM((tm, D_MODEL), BF16), pltpu.SemaphoreType.DMA((2,)),
                            pltpu.SemaphoreType.DMA((2,)), pltpu.SemaphoreType.DMA(())],
        ),
        out_shape=jax.ShapeDtypeStruct((2 * m + n_trash, D_MODEL), F32),
        compiler_params=_params(("arbitrary", "arbitrary"), 48),
        name="moe_experts",
    )(tile_expert, nused, src, src, dst, x, g, wg, wu, wd)


def _combine_kernel(x_ref, route_ref, y1_ref, y2_ref, outa_ref, outb_ref, *, na):
    route = route_ref[...]
    p1 = route[:, _R_P1:_R_P1 + 1]
    p2 = route[:, _R_P2:_R_P2 + 1]
    out = x_ref[...] + (p1 * y1_ref[...] + p2 * y2_ref[...])
    first = pl.program_id(0) < na

    @pl.when(first)
    def _():
        outa_ref[...] = out

    @pl.when(jnp.logical_not(first))
    def _():
        outb_ref[...] = out


def _combine(x, route, y, tc, m1):
    m = x.shape[0]
    na = m1 // tc
    row = lambda i: (i, 0)
    blk = pl.BlockSpec((tc, D_MODEL), row)
    return pl.pallas_call(
        functools.partial(_combine_kernel, na=na),
        grid=(m // tc,),
        in_specs=[blk, pl.BlockSpec((tc, LANES), row), blk,
                  pl.BlockSpec((tc, D_MODEL), lambda i: (i + m // tc, 0))],
        out_specs=_pair_specs(tc, na),
        out_shape=[jax.ShapeDtypeStruct((m1, D_MODEL), F32),
                   jax.ShapeDtypeStruct((m - m1, D_MODEL), F32)],
        compiler_params=_params(("arbitrary",), 32),
        name="moe_combine",
    )(x, route, y, y)


def _moe(x, g, wr, wg, wu, wd, tm_route, tm, tf, m1):
    m = x.shape[0]
    route, counts = _router(x, g, wr, tm_route)
    e1 = route[:, _R_E1].astype(jnp.int32)
    e2 = route[:, _R_E2].astype(jnp.int32)
    n_e = counts[0, :N_EXPERTS].astype(jnp.int32)
    tiles_e = (n_e + tm - 1) // tm
    tile_end = jnp.cumsum(tiles_e)
    row_start = (tile_end - tiles_e) * tm
    pos = jnp.concatenate([row_start[e1] + route[:, _R_RANK1].astype(jnp.int32),
                           row_start[e2] + route[:, _R_RANK2].astype(jnp.int32)])
    ntile = pl.cdiv(2 * m, tm) + N_EXPERTS + 2
    nused = tile_end[-1:]
    tile_ids = jnp.minimum(jnp.arange(ntile, dtype=jnp.int32), nused - 1)
    tile_expert = jnp.sum(tile_ids[:, None] >= tile_end[None, :], axis=1).astype(jnp.int32)
    rows = jnp.arange(ntile * tm, dtype=jnp.int32)
    spare = 2 * m + rows - jnp.cumsum(n_e)[jnp.repeat(tile_expert, tm)]
    spare = jnp.clip(spare, 2 * m, 2 * m + N_EXPERTS * tm - 1)
    assign = _invert(pos, ntile * tm, tm_route)
    src = jnp.where(assign >= m, assign - m, jnp.maximum(assign, 0))
    dst = jnp.where(assign >= 0, assign, spare)
    y = _experts(tile_expert, nused.astype(jnp.int32), src.reshape(ntile, 1, tm),
                 dst.reshape(ntile, 1, tm), x, g, wg, wu, wd, tm, tf)
    return tuple(_combine(x, route, y, tm_route, m1))


def _mlstm_gate_perm():
    perm = []
    for pair in range(M_HEADS // 2):
        for hh in range(2):
            for gtype in range(4):
                perm.append(gtype * M_HEADS + 2 * pair + hh)
    return jnp.array(perm, jnp.int32)


def kernel(x_prompt, x_sample, norm_mix, norm_ffn, mlstm_w_in, mlstm_gate_bias, mlstm_head_gain,
           mlstm_w_out, attn_w_in, attn_q_gain, attn_k_gain, attn_sink, attn_w_out, ffn_w_gate,
           ffn_w_up, ffn_w_down, moe_w_router, moe_w_gate, moe_w_up, moe_w_down):
    b1, s1, _ = x_prompt.shape
    b2, s2, _ = x_sample.shape
    m1 = b1 * s1
    x = (x_prompt.reshape(m1, D_MODEL), x_sample.reshape(b2 * s2, D_MODEL))
    m = m1 + b2 * s2
    seq_gcd = math.gcd(s1, s2)
    tm = _tile(seq_gcd, 512)
    rb = _tile(seq_gcd, 1024)
    tm_ffn = _tile(m, 1024)
    tf = 512
    tm_moe = LANES * (D_FF // tf)
    depth = norm_mix.shape[0]
    vec = lambda a: a.astype(F32).reshape(1, -1)
    nq = M_HEADS * M_QK
    nv = M_HEADS * M_V

    tabs = _rope_tables(max(s1, s2), tm)
    group_mean = jnp.where((jnp.arange(LANES)[:, None] // A_HD) == (jnp.arange(LANES)[None, :] // A_HD),
                           1.0 / A_HD, 0.0).astype(BF16)
    gate_perm = _mlstm_gate_perm()

    for i in range(depth):
        j = i // 2
        if i % 2 == 0:
            w_in = mlstm_w_in[j]
            wqT = w_in[:, :nq].T.astype(BF16)
            wk = w_in[:, nq:2 * nq].astype(BF16)
            wv = w_in[:, 2 * nq:2 * nq + nv].astype(BF16)
            wo = w_in[:, 2 * nq + nv:2 * nq + 2 * nv].astype(BF16)
            wgT = w_in[:, 2 * nq + 2 * nv:][:, gate_perm].T.astype(BF16)
            bgT = mlstm_gate_bias[j].astype(F32)[gate_perm].reshape(-1, 1)
            qT, kT, k, v, o, gT = _mlstm_in(x, vec(norm_mix[i]), wqT, wk.T, wk, wv, wo, wgT, bgT, tm)
            hf, hb = _mlstm_core(qT, kT, k, v, gT, rb, m1, s1, s2)
            x = _mlstm_out(hf, hb, o, vec(mlstm_head_gain[j]), mlstm_w_out[j].astype(BF16), x, tm)
            x = _ffn(x, vec(norm_ffn[i]), ffn_w_gate[j].astype(BF16), ffn_w_up[j].astype(BF16),
                     ffn_w_down[j].astype(BF16), tm_ffn, tf)
        else:
            w_in = attn_w_in[j]
            a_q = A_Q_HEADS * A_HD
            a_kv = A_KV_HEADS * A_HD
            qg = jnp.tile(attn_q_gain[j].astype(F32), LANES // A_HD).reshape(1, LANES)
            kg = jnp.tile(attn_k_gain[j].astype(F32), LANES // A_HD).reshape(1, LANES)
            q, k, v = _attn_in(x, vec(norm_mix[i]), w_in[:, :a_q].astype(BF16),
                               w_in[:, a_q:a_q + a_kv].astype(BF16), w_in[:, a_q + a_kv:].astype(BF16),
                               qg, kg, group_mean, tabs, tm, m1, s1, s2)
            att = _attn(attn_sink[j].astype(F32), q, k, v, m1, s1, s2)
            x = _proj_res(att, attn_w_out[j].astype(BF16), x, tm)
            wr = jnp.pad(moe_w_router[j], ((0, 0), (0, LANES - N_EXPERTS))).astype(BF16)
            x = _moe(x, vec(norm_ffn[i]), wr, moe_w_gate[j].astype(BF16), moe_w_up[j].astype(BF16),
                     moe_w_down[j].astype(BF16), tm, tm_moe, tf, m1)

    xa, xb = x if isinstance(x, tuple) else (x[:m1], x[m1:])
    return xa.reshape(b1, s1, D_MODEL), xb.reshape(b2, s2, D_MODEL)
```

```python
import functools
import math

import jax
import jax.numpy as jnp
from jax import lax
from jax.experimental import pallas as pl
from jax.experimental.pallas import tpu as pltpu

F32 = jnp.float32
BF16 = jnp.bfloat16

D_MODEL = 1024
EPS = 1e-6
M_HEADS = 8
M_QK = 64
M_V = 128
M_CHUNK = 128
A_Q_HEADS = 16
A_KV_HEADS = 4
A_HD = 64
A_GROUP = A_Q_HEADS // A_KV_HEADS
A_BLOCK = 128
WINDOW = 128
ROPE_THETA = 500000.0
ROPE_DIM = 16
D_FF = 3584
N_EXPERTS = 8

LANES = 128
VMEM_BYTES_V7X = 64 * 1024 * 1024

_NT = (((1,), (1,)), ((), ()))


def _params(semantics, vmem_mb):
    assert vmem_mb * 1024 * 1024 < VMEM_BYTES_V7X
    return pltpu.CompilerParams(dimension_semantics=semantics,
                                vmem_limit_bytes=vmem_mb * 1024 * 1024)


def _tile(total, pref):
    t = min(pref, total)
    t -= t % LANES
    while total % t:
        t -= LANES
    return t


def _rms(x, g):
    ms = jnp.mean(x * x, axis=-1, keepdims=True)
    return x * lax.rsqrt(ms + EPS) * g


def _sigmoid(x):
    return 1.0 / (1.0 + jnp.exp(-x))


def _dot(a, b):
    return jnp.dot(a, b, preferred_element_type=F32)


def _seq_pos(row0, m1, s1, s2):
    first = row0 < m1
    pos = jnp.where(first, row0 % s1, (row0 - m1) % s2)
    return pos, jnp.where(first, s1, s2)


def _pair_specs(tm, na):
    return [pl.BlockSpec((tm, D_MODEL), lambda i: (jnp.minimum(i, na - 1), 0)),
            pl.BlockSpec((tm, D_MODEL), lambda i: (jnp.maximum(i - na, 0), 0))]


def _pair_tile(xa_ref, xb_ref, na):
    return jnp.where(pl.program_id(0) < na, xa_ref[...], xb_ref[...])


def _as_pair(x, tm):
    if isinstance(x, tuple):
        return x[0], x[1], x[0].shape[0] // tm
    return x, x, x.shape[0] // tm


def _mlstm_in_kernel(xa_ref, xb_ref, g_ref, wqT_ref, wkT_ref, wk_ref, wv_ref, wo_ref, wgT_ref, bgT_ref,
                     qT_ref, kT_ref, k_ref, v_ref, o_ref, st_ref, *, na):
    hn = _rms(_pair_tile(xa_ref, xb_ref, na), g_ref[...]).astype(BF16)
    gT = lax.dot_general(wgT_ref[...], hn, _NT, preferred_element_type=F32) + bgT_ref[...]
    log_sig = jnp.minimum(gT, 0.0) - jnp.log1p(jnp.exp(-jnp.abs(gT)))
    row = lax.broadcasted_iota(jnp.int32, gT.shape, 0)
    gT = jnp.where(row % 2 == 1, log_sig, gT)
    nchunk = kT_ref.shape[0]
    for j in range(nchunk):
        cols = slice(j * M_CHUNK, (j + 1) * M_CHUNK)
        for pair in range(M_HEADS // 2):
            stats = _mlstm_gate_stats(gT[pair * 8:(pair + 1) * 8, cols])
            st_ref[j, pair * _N_STATS * 8:(pair + 1) * _N_STATS * 8, :] = jnp.concatenate(stats, axis=0)
    k_ref[...] = _dot(hn, wk_ref[...]).astype(BF16)
    v_ref[...] = _dot(hn, wv_ref[...]).astype(BF16)
    o_ref[...] = _dot(hn, wo_ref[...]).astype(BF16)
    qT = lax.dot_general(wqT_ref[...], hn, _NT, preferred_element_type=F32) * (M_QK ** -0.5)
    qT = qT.astype(BF16)
    kT = lax.dot_general(wkT_ref[...], hn, _NT, preferred_element_type=F32).astype(BF16)
    for j in range(nchunk):
        cols = slice(j * M_CHUNK, (j + 1) * M_CHUNK)
        qT_ref[j] = qT[:, cols]
        kT_ref[j] = kT[:, cols]


def _mlstm_in(x, g, wqT, wkT, wk, wv, wo, wgT, bgT, tm):
    xa, xb, na = _as_pair(x, tm)
    m = xa.shape[0] + xb.shape[0] if isinstance(x, tuple) else xa.shape[0]
    nq = M_HEADS * M_QK
    nv = M_HEADS * M_V
    ng = 4 * M_HEADS
    row = lambda i: (i, 0)
    fix = lambda i: (0, 0)
    chunked = lambda n: pl.BlockSpec((tm // M_CHUNK, n, M_CHUNK), lambda i: (i, 0, 0))
    chunked_shape = lambda n, dt: jax.ShapeDtypeStruct((m // M_CHUNK, n, M_CHUNK), dt)
    return pl.pallas_call(
        functools.partial(_mlstm_in_kernel, na=na),
        grid=(m // tm,),
        in_specs=_pair_specs(tm, na) + [
            pl.BlockSpec((1, D_MODEL), fix),
            pl.BlockSpec((nq, D_MODEL), fix),
            pl.BlockSpec((nq, D_MODEL), fix),
            pl.BlockSpec((D_MODEL, nq), fix),
            pl.BlockSpec((D_MODEL, nv), fix),
            pl.BlockSpec((D_MODEL, nv), fix),
            pl.BlockSpec((ng, D_MODEL), fix),
            pl.BlockSpec((ng, 1), fix),
        ],
        out_specs=[chunked(nq), chunked(nq), pl.BlockSpec((tm, nq), row),
                   pl.BlockSpec((tm, nv), row), pl.BlockSpec((tm, nv), row), chunked(_N_STATS * ng)],
        out_shape=[chunked_shape(nq, BF16), chunked_shape(nq, BF16),
                   jax.ShapeDtypeStruct((m, nq), BF16), jax.ShapeDtypeStruct((m, nv), BF16),
                   jax.ShapeDtypeStruct((m, nv), BF16), chunked_shape(_N_STATS * ng, F32)],
        compiler_params=_params(("parallel",), 40),
        name="mlstm_in",
    )(xa, xb, g, wqT, wkT, wk, wv, wo, wgT, bgT)


_N_STATS = 6


def _lane_scan(x, combine, identity, reverse):
    n = x.shape[-1]
    lane = lax.broadcasted_iota(jnp.int32, x.shape, 1)
    sh = 1
    while sh < n:
        if reverse:
            moved = jnp.where(lane < n - sh, pltpu.roll(x, n - sh, 1), identity)
        else:
            moved = jnp.where(lane >= sh, pltpu.roll(x, sh, 1), identity)
        x = combine(x, moved)
        sh *= 2
    return x


def _mlstm_gate_stats(gates):
    row = lax.broadcasted_iota(jnp.int32, gates.shape, 0)
    bwd_row = (row % 4) >= 2
    add = lambda x, y: x + y
    cum = jnp.where(bwd_row, _lane_scan(gates, add, 0.0, True), _lane_scan(gates, add, 0.0, False))
    b = pltpu.roll(cum, 7, 0)
    total = jnp.broadcast_to(jnp.sum(gates, axis=1, keepdims=True), gates.shape)
    g = pltpu.roll(total, 7, 0)
    r = gates - b
    pm = jnp.where(bwd_row, _lane_scan(r, jnp.maximum, -jnp.inf, True),
                   _lane_scan(r, jnp.maximum, -jnp.inf, False))
    a = g + r
    a_max = jnp.broadcast_to(jnp.max(a, axis=1, keepdims=True), gates.shape)
    return b, r, pm, g, a_max, jnp.exp(a - a_max)


def _mlstm_unit_pre(qT, kT, k, v2, stats, reverse):
    L = M_CHUNK
    s_idx = lax.broadcasted_iota(jnp.int32, (L, L), 0)
    t_idx = lax.broadcasted_iota(jnp.int32, (L, L), 1)
    keep = (s_idx >= t_idx) if reverse else (s_idx <= t_idx)
    head_a_lanes = lax.broadcasted_iota(jnp.int32, (L, LANES), 1) < M_QK
    head_a_rows = lax.broadcasted_iota(jnp.int32, (LANES, L), 0) < M_QK
    zeros_k = jnp.zeros_like(k)
    k2 = jnp.concatenate([jnp.where(head_a_lanes, k, zeros_k), jnp.where(head_a_lanes, zeros_k, k)],
                         axis=0)
    s2 = _dot(k2, qT)
    sT, den0 = [], []
    for hd in range(2):
        _, r, pm, _, _, _ = stats[hd]
        r_col = jnp.broadcast_to(r, (L, L)).T
        sT.append(s2[hd * L:(hd + 1) * L] * jnp.exp(jnp.where(keep, r_col - pm, -jnp.inf)))
        den0.append(jnp.sum(sT[hd], axis=0, keepdims=True))
    w = jnp.where(head_a_rows, stats[0][5], stats[1][5])
    ones = jnp.ones((L, M_V), BF16)
    kv = _dot((kT.astype(F32) * w).astype(BF16), jnp.concatenate([v2, ones], axis=1))
    col = lax.broadcasted_iota(jnp.int32, kv.shape, 1)
    row_a = lax.broadcasted_iota(jnp.int32, kv.shape, 0) < M_QK
    other_v = (row_a & (col >= M_V) & (col < 2 * M_V)) | (~row_a & (col < M_V))
    return sT, den0, jnp.where(other_v, 0.0, kv)


def _mlstm_unit_post(qT, v2, stats, pre, c_pair, m_rows):
    L = M_CHUNK
    sT, den0, kv = pre
    head_a_rows = lax.broadcasted_iota(jnp.int32, (LANES, L), 0) < M_QK
    row_a = lax.broadcasted_iota(jnp.int32, kv.shape, 0) < M_QK
    qT_f = qT.astype(F32)
    qn_all = qT_f * c_pair[:, 2 * M_V:]
    s_scale, q_scale, m_new, decay, scale = [], [], [], [], []
    for hd in range(2):
        b, _, pm, g, a_max, _ = stats[hd]
        m_row = m_rows[hd]
        u = jnp.maximum(pm, m_row)
        c1 = jnp.exp(pm - u)
        e = jnp.exp(m_row - u)
        qn = jnp.sum(qn_all[hd * M_QK:(hd + 1) * M_QK], axis=0, keepdims=True)
        den = c1 * den0[hd] + e * qn
        inv = 1.0 / jnp.maximum(jnp.abs(den), jnp.exp(-(b + u)))
        s_scale.append(c1 * inv)
        q_scale.append(e * inv)
        m_new.append(jnp.maximum(g + m_row, a_max))
        decay.append(jnp.exp(g + m_row - m_new[hd])[:, 0:1])
        scale.append(jnp.exp(a_max - m_new[hd])[:, 0:1])
    lhsT = jnp.concatenate([sT[0] * s_scale[0], sT[1] * s_scale[1],
                            qT_f * jnp.where(head_a_rows, q_scale[0], q_scale[1])],
                           axis=0).astype(BF16)
    zeros_v = jnp.zeros((L, M_V), BF16)
    rhs = jnp.concatenate([jnp.concatenate([v2[:, :M_V], zeros_v], axis=1),
                           jnp.concatenate([zeros_v, v2[:, M_V:]], axis=1),
                           c_pair[:, :2 * M_V].astype(BF16)], axis=0)
    h = lax.dot_general(lhsT, rhs, (((0,), (0,)), ((), ())), preferred_element_type=F32)
    c_new = jnp.where(row_a, decay[0], decay[1]) * c_pair + jnp.where(row_a, scale[0], scale[1]) * kv
    return h, c_new, m_new


def _mlstm_core_kernel(qTf_ref, kTf_ref, kf_ref, vf_ref, sf_ref, qTb_ref, kTb_ref, kb_ref, vb_ref,
                       sb_ref, hf_ref, hb_ref, c_ref, m_ref, *, nchunk, rb, m1, s1, s2):
    i = pl.program_id(1)
    nblk = pl.num_programs(1)
    pos_f, _ = _seq_pos(i * rb, m1, s1, s2)
    pos_b, len_b = _seq_pos((nblk - 1 - i) * rb, m1, s1, s2)

    def reset(dirn):
        c_ref[dirn] = jnp.zeros(c_ref.shape[1:], F32)
        for hd in range(2):
            m_ref[hd * 2 + dirn] = jnp.zeros(m_ref.shape[1:], F32)

    pl.when(pos_f == 0)(lambda: reset(0))
    pl.when(pos_b + rb == len_b)(lambda: reset(1))

    c_state = [c_ref[dirn] for dirn in range(2)]
    m_state = [[m_ref[hd * 2 + dirn][0:1] for hd in range(2)] for dirn in range(2)]
    refs = ((qTf_ref, kTf_ref, kf_ref, vf_ref, sf_ref, hf_ref),
            (qTb_ref, kTb_ref, kb_ref, vb_ref, sb_ref, hb_ref))

    def unit(step, dirn):
        qT_ref, kT_ref, k_ref, v_ref, st_ref, out_ref = refs[dirn]
        cc = nchunk - 1 - step if dirn else step
        rows = slice(cc * M_CHUNK, (cc + 1) * M_CHUNK)
        st = st_ref[cc]
        stats = [tuple(st[n * 8 + hd * 4 + dirn * 2:n * 8 + hd * 4 + dirn * 2 + 1]
                       for n in range(_N_STATS)) for hd in range(2)]
        return qT_ref[cc], kT_ref[cc], k_ref[rows, :], v_ref[rows, :], stats, out_ref, rows

    pre = {}
    for step in range(nchunk):
        for dirn in range(2):
            qT, kT, k, v2, stats, _, _ = unit(step, dirn)
            pre[step, dirn] = _mlstm_unit_pre(qT, kT, k, v2, stats, reverse=bool(dirn))
    for step in range(nchunk):
        for dirn in range(2):
            qT, _, _, v2, stats, out_ref, rows = unit(step, dirn)
            h, c_state[dirn], m_state[dirn] = _mlstm_unit_post(
                qT, v2, stats, pre[step, dirn], c_state[dirn], m_state[dirn])
            out_ref[rows, :] = h

    for dirn in range(2):
        c_ref[dirn] = c_state[dirn]
        for hd in range(2):
            m_ref[hd * 2 + dirn] = jnp.broadcast_to(m_state[dirn][hd], m_ref.shape[1:])


def _mlstm_core(qT, kT, k, v, gT, rb, m1, s1, s2):
    m = k.shape[0]
    nblk = m // rb
    nchunk = rb // M_CHUNK
    npair = M_HEADS // 2
    fwd = lambda p, i: (i, p)
    bwd = lambda p, i: (nblk - 1 - i, p)
    fwd3 = lambda p, i: (i, p, 0)
    bwd3 = lambda p, i: (nblk - 1 - i, p, 0)

    def specs(im2, im3):
        return [
            pl.BlockSpec((nchunk, LANES, M_CHUNK), im3),
            pl.BlockSpec((nchunk, LANES, M_CHUNK), im3),
            pl.BlockSpec((rb, LANES), im2),
            pl.BlockSpec((rb, 2 * M_V), im2),
            pl.BlockSpec((nchunk, _N_STATS * 8, M_CHUNK), im3),
        ]

    kern = functools.partial(_mlstm_core_kernel, nchunk=nchunk, rb=rb, m1=m1, s1=s1, s2=s2)
    return pl.pallas_call(
        kern,
        grid=(npair, nblk),
        in_specs=specs(fwd, fwd3) + specs(bwd, bwd3),
        out_specs=[pl.BlockSpec((rb, 2 * M_V), fwd), pl.BlockSpec((rb, 2 * M_V), bwd)],
        out_shape=[jax.ShapeDtypeStruct((m, M_HEADS * M_V), F32)] * 2,
        scratch_shapes=[pltpu.VMEM((2, LANES, 3 * M_V), F32), pltpu.VMEM((4, 8, M_CHUNK), F32)],
        compiler_params=_params(("parallel", "arbitrary"), 32),
        name="mlstm_core",
    )(qT, kT, k, v, gT, qT, kT, k, v, gT)


def _mlstm_out_kernel(hf_ref, hb_ref, o_ref, gain_ref, w_ref, xa_ref, xb_ref, out_ref, *, na):
    h = hf_ref[...] + hb_ref[...]
    parts = []
    for hd in range(M_HEADS):
        hh = h[:, hd * M_V:(hd + 1) * M_V]
        ms = jnp.mean(hh * hh, axis=-1, keepdims=True)
        parts.append(hh * lax.rsqrt(ms + EPS))
    hn = jnp.concatenate(parts, axis=1) * gain_ref[...]
    hg = (_sigmoid(o_ref[...].astype(F32)) * hn).astype(BF16)
    out_ref[...] = _pair_tile(xa_ref, xb_ref, na) + _dot(hg, w_ref[...])


def _mlstm_out(hf, hb, o, gain, w, x, tm):
    xa, xb, na = _as_pair(x, tm)
    m = hf.shape[0]
    row = lambda i: (i, 0)
    fix = lambda i: (0, 0)
    blk = pl.BlockSpec((tm, D_MODEL), row)
    return pl.pallas_call(
        functools.partial(_mlstm_out_kernel, na=na),
        grid=(m // tm,),
        in_specs=[blk, blk, blk, pl.BlockSpec((1, D_MODEL), fix),
                  pl.BlockSpec((D_MODEL, D_MODEL), fix)] + _pair_specs(tm, na),
        out_specs=blk,
        out_shape=jax.ShapeDtypeStruct((m, D_MODEL), F32),
        compiler_params=_params(("parallel",), 40),
        name="mlstm_out",
    )(hf, hb, o, gain, w, xa, xb)


def _ffn_kernel(x_ref, g_ref, wg_ref, wu_ref, wd_ref, out_ref, hn_ref, acc_ref):
    f = pl.program_id(1)

    @pl.when(f == 0)
    def _():
        hn_ref[...] = _rms(x_ref[...], g_ref[...]).astype(BF16)
        acc_ref[...] = jnp.zeros_like(acc_ref)

    hn = hn_ref[...]
    gate = _dot(hn, wg_ref[...])
    up = _dot(hn, wu_ref[...])
    act = (gate * _sigmoid(gate) * up).astype(BF16)
    acc_ref[...] += _dot(act, wd_ref[...])

    @pl.when(f == pl.num_programs(1) - 1)
    def _():
        out_ref[...] = x_ref[...] + acc_ref[...]


def _ffn(x, g, wg, wu, wd, tm, tf):
    m = x.shape[0]
    row = lambda i, f: (i, 0)
    return pl.pallas_call(
        _ffn_kernel,
        grid=(m // tm, D_FF // tf),
        in_specs=[
            pl.BlockSpec((tm, D_MODEL), row),
            pl.BlockSpec((1, D_MODEL), lambda i, f: (0, 0)),
            pl.BlockSpec((D_MODEL, tf), lambda i, f: (0, f)),
            pl.BlockSpec((D_MODEL, tf), lambda i, f: (0, f)),
            pl.BlockSpec((tf, D_MODEL), lambda i, f: (f, 0)),
        ],
        out_specs=pl.BlockSpec((tm, D_MODEL), row),
        out_shape=jax.ShapeDtypeStruct((m, D_MODEL), F32),
        scratch_shapes=[pltpu.VMEM((tm, D_MODEL), BF16), pltpu.VMEM((tm, D_MODEL), F32)],
        compiler_params=_params(("parallel", "arbitrary"), 48),
        name="ffn",
    )(x, g, wg, wu, wd)


def _rope_table_kernel(inv_ref, ma_ref, mb_ref, cos_ref, sa_ref, sb_ref):
    rows = cos_ref.shape[0]
    pos = pl.program_id(0) * rows + lax.broadcasted_iota(jnp.int32, (rows, LANES), 0)
    ang = pos.astype(F32) * inv_ref[...]
    sin = jnp.sin(ang)
    cos_ref[...] = jnp.cos(ang)
    sa_ref[...] = sin * ma_ref[...]
    sb_ref[...] = sin * mb_ref[...]


def _rope_tables(smax, rows):
    half = ROPE_DIM // 2
    d = jnp.arange(LANES) % A_HD
    inv = ROPE_THETA ** (-(jnp.arange(half, dtype=F32) * 2.0) / ROPE_DIM)
    inv_lane = jnp.where(d < ROPE_DIM, inv[d % half], 0.0).astype(F32)[None, :]
    ma = jnp.where(d < half, -1.0, 0.0).astype(F32)[None, :]
    mb = jnp.where((d >= half) & (d < ROPE_DIM), 1.0, 0.0).astype(F32)[None, :]
    fix = lambda i: (0, 0)
    vec = pl.BlockSpec((1, LANES), fix)
    tab = pl.BlockSpec((rows, LANES), lambda i: (i, 0))
    return pl.pallas_call(
        _rope_table_kernel,
        grid=(smax // rows,),
        in_specs=[vec, vec, vec],
        out_specs=[tab, tab, tab],
        out_shape=[jax.ShapeDtypeStruct((smax, LANES), F32)] * 3,
        compiler_params=_params(("parallel",), 16),
        name="rope_tables",
    )(inv_lane, ma, mb)


def _attn_in_kernel(x_ref, g_ref, wq_ref, wk_ref, wv_ref, qg_ref, kg_ref, gm_ref,
                    cos_ref, sa_ref, sb_ref, q_ref, k_ref, v_ref):
    hn = _rms(x_ref[...], g_ref[...]).astype(BF16)
    cos = cos_ref[...]
    sa = sa_ref[...]
    sb = sb_ref[...]
    half = ROPE_DIM // 2

    def norm_rope(x, gain, scale, out_ref):
        for j in range(x.shape[1] // LANES):
            xj = x[:, j * LANES:(j + 1) * LANES]
            ms = _dot((xj * xj).astype(BF16), gm_ref[...])
            y = xj * lax.rsqrt(ms + EPS) * gain
            y = y * cos + pltpu.roll(y, LANES - half, 1) * sa + pltpu.roll(y, half, 1) * sb
            out_ref[:, j * LANES:(j + 1) * LANES] = (y * scale).astype(BF16)

    norm_rope(_dot(hn, wq_ref[...]), qg_ref[...], A_HD ** -0.5, q_ref)
    norm_rope(_dot(hn, wk_ref[...]), kg_ref[...], 1.0, k_ref)
    v_ref[...] = _dot(hn, wv_ref[...]).astype(BF16)


def _attn_in(x, g, wq, wk, wv, qg, kg, gm, tabs, tm, m1, s1, s2):
    m = x.shape[0]
    nq = A_Q_HEADS * A_HD
    nkv = A_KV_HEADS * A_HD
    row = lambda i: (i, 0)
    fix = lambda i: (0, 0)

    def tab_map(i):
        pos, _ = _seq_pos(i * tm, m1, s1, s2)
        return (pos // tm, 0)

    tab = pl.BlockSpec((tm, LANES), tab_map)
    vec = pl.BlockSpec((1, LANES), fix)
    return pl.pallas_call(
        _attn_in_kernel,
        grid=(m // tm,),
        in_specs=[
            pl.BlockSpec((tm, D_MODEL), row),
            pl.BlockSpec((1, D_MODEL), fix),
            pl.BlockSpec((D_MODEL, nq), fix),
            pl.BlockSpec((D_MODEL, nkv), fix),
            pl.BlockSpec((D_MODEL, nkv), fix),
            vec, vec,
            pl.BlockSpec((LANES, LANES), fix),
            tab, tab, tab,
        ],
        out_specs=[pl.BlockSpec((tm, nq), row), pl.BlockSpec((tm, nkv), row),
                   pl.BlockSpec((tm, nkv), row)],
        out_shape=[jax.ShapeDtypeStruct((m, nq), BF16), jax.ShapeDtypeStruct((m, nkv), BF16),
                   jax.ShapeDtypeStruct((m, nkv), BF16)],
        compiler_params=_params(("parallel",), 40),
        name="attn_in",
    )(x, g, wq, wk, wv, qg, kg, gm, *tabs)


def _attn_kernel(sink_ref, q_ref, kp_ref, kc_ref, kn_ref, vp_ref, vc_ref, vn_ref, out_ref,
                 *, m1, s1, s2):
    blk = A_BLOCK
    pos0, slen = _seq_pos(pl.program_id(0) * blk, m1, s1, s2)
    prev_ok = pos0 > 0
    next_ok = pos0 + blk < slen
    t = lax.broadcasted_iota(jnp.int32, (blk, 3 * blk), 0)
    c = lax.broadcasted_iota(jnp.int32, (blk, 3 * blk), 1)
    valid = (jnp.abs(c - blk - t) <= WINDOW) & ((c >= blk) | prev_ok) & ((c < 2 * blk) | next_ok)

    kcat = jnp.concatenate([kp_ref[...], kc_ref[...], kn_ref[...]], axis=0).astype(F32)
    vcat = jnp.concatenate([vp_ref[...], vc_ref[...], vn_ref[...]], axis=0).astype(F32)
    lane_kv = lax.broadcasted_iota(jnp.int32, (3 * blk, LANES), 1)
    lane_q = lax.broadcasted_iota(jnp.int32, (blk, LANES), 1)

    def softmax_parts(s, sink):
        s = jnp.where(valid, s, -jnp.inf)
        m = jnp.maximum(jnp.max(s, axis=1, keepdims=True), sink)
        p = jnp.exp(s - m)
        denom = jnp.sum(p, axis=1, keepdims=True) + jnp.exp(sink - m)
        return p.astype(BF16), denom

    def lo_hi(cat, h):
        tile, half = divmod(h, 2)
        own = (lane_kv < A_HD) if half == 0 else (lane_kv >= A_HD)
        mine = jnp.where(own, cat[:, tile * LANES:(tile + 1) * LANES], 0.0)
        other = pltpu.roll(mine, A_HD, 1)
        lo, hi = (mine, other) if half == 0 else (other, mine)
        return jnp.concatenate([lo, hi], axis=0).astype(BF16)

    npair = A_GROUP // 2
    kk = [lo_hi(kcat, h) for h in range(A_KV_HEADS)]
    vv = [lo_hi(vcat, h) for h in range(A_KV_HEADS)]
    scores = []
    for h in range(A_KV_HEADS):
        q_pairs = jnp.concatenate([q_ref[:, (h * npair + j) * LANES:(h * npair + j + 1) * LANES]
                                   for j in range(npair)], axis=0)
        scores.append(lax.dot_general(q_pairs, kk[h], _NT, preferred_element_type=F32))
    for h in range(A_KV_HEADS):
        p_rows, inv_rows = [], []
        for j in range(npair):
            parts = [softmax_parts(scores[h][j * blk:(j + 1) * blk, par * 3 * blk:(par + 1) * 3 * blk],
                                   sink_ref[(h * npair + j) * 2 + par]) for par in range(2)]
            p_rows.append(jnp.concatenate([parts[0][0], parts[1][0]], axis=1))
            inv_rows.append(jnp.where(lane_q < A_HD, 1.0 / parts[0][1], 1.0 / parts[1][1]))
        o = _dot(jnp.concatenate(p_rows, axis=0), vv[h])
        for j in range(npair):
            grp = h * npair + j
            out_ref[:, grp * LANES:(grp + 1) * LANES] = (o[j * blk:(j + 1) * blk] * inv_rows[j]).astype(BF16)


def _attn(sink, q, k, v, m1, s1, s2):
    m = q.shape[0]
    blk = A_BLOCK
    nblk = m // blk
    nkv = A_KV_HEADS * A_HD
    cur = lambda i: (i, 0)
    prev = lambda i: (jnp.maximum(i - 1, 0), 0)
    nxt = lambda i: (jnp.minimum(i + 1, nblk - 1), 0)
    kv = lambda im: pl.BlockSpec((blk, nkv), im)
    kern = functools.partial(_attn_kernel, m1=m1, s1=s1, s2=s2)
    return pl.pallas_call(
        kern,
        grid=(nblk,),
        in_specs=[pl.BlockSpec(memory_space=pltpu.SMEM),
                  pl.BlockSpec((blk, D_MODEL), cur), kv(prev), kv(cur), kv(nxt),
                  kv(prev), kv(cur), kv(nxt)],
        out_specs=pl.BlockSpec((blk, D_MODEL), cur),
        out_shape=jax.ShapeDtypeStruct((m, D_MODEL), BF16),
        compiler_params=_params(("parallel",), 32),
        name="attn",
    )(sink, q, k, k, k, v, v, v)


def _proj_res_kernel(a_ref, w_ref, x_ref, out_ref):
    out_ref[...] = x_ref[...] + _dot(a_ref[...], w_ref[...])


def _proj_res(a, w, x, tm):
    m = x.shape[0]
    row = lambda i: (i, 0)
    blk = pl.BlockSpec((tm, D_MODEL), row)
    return pl.pallas_call(
        _proj_res_kernel,
        grid=(m // tm,),
        in_specs=[blk, pl.BlockSpec((D_MODEL, D_MODEL), lambda i: (0, 0)), blk],
        out_specs=blk,
        out_shape=jax.ShapeDtypeStruct((m, D_MODEL), F32),
        compiler_params=_params(("parallel",), 32),
        name="attn_out",
    )(a, w, x)


_R_E1, _R_E2, _R_P1, _R_P2, _R_RANK1, _R_RANK2 = range(6)


def _router_kernel(x_ref, g_ref, wr_ref, route_ref, count_ref, carry_ref):
    @pl.when(pl.program_id(0) == 0)
    def _():
        carry_ref[...] = jnp.zeros_like(carry_ref)

    hn = _rms(x_ref[...], g_ref[...]).astype(BF16)
    logits = _dot(hn, wr_ref[...])
    tm = logits.shape[0]
    lane = lax.broadcasted_iota(jnp.int32, logits.shape, 1).astype(F32)
    logits = jnp.where(lane < N_EXPERTS, logits, -jnp.inf)
    v1 = jnp.max(logits, axis=1, keepdims=True)
    i1 = jnp.min(jnp.where(logits == v1, lane, float(LANES)), axis=1, keepdims=True)
    rest = jnp.where(lane == i1, -jnp.inf, logits)
    v2 = jnp.max(rest, axis=1, keepdims=True)
    i2 = jnp.min(jnp.where(rest == v2, lane, float(LANES)), axis=1, keepdims=True)
    e2 = jnp.exp(v2 - v1)
    p1 = 1.0 / (1.0 + e2)
    p2 = e2 * p1

    sel = jnp.where((lane == i1) | (lane == i2), 1.0, 0.0)
    t_row = lax.broadcasted_iota(jnp.int32, (tm, tm), 0)
    t_col = lax.broadcasted_iota(jnp.int32, (tm, tm), 1)
    earlier = jnp.where(t_col < t_row, 1.0, 0.0).astype(BF16)
    before = _dot(earlier, sel.astype(BF16)) + carry_ref[0:1, :]
    rank1 = jnp.sum(jnp.where(lane == i1, before, 0.0), axis=1, keepdims=True)
    rank2 = jnp.sum(jnp.where(lane == i2, before, 0.0), axis=1, keepdims=True)
    total = carry_ref[0:1, :] + jnp.sum(sel, axis=0, keepdims=True)
    carry_ref[...] = jnp.broadcast_to(total, carry_ref.shape)
    count_ref[...] = jnp.broadcast_to(total, count_ref.shape)

    route = jnp.zeros_like(logits)
    for col, val in ((_R_E1, i1), (_R_E2, i2), (_R_P1, p1), (_R_P2, p2),
                     (_R_RANK1, rank1), (_R_RANK2, rank2)):
        route = jnp.where(lane == float(col), val, route)
    route_ref[...] = route


def _router(x, g, wr, tm):
    m = x.shape[0]
    row = lambda i: (i, 0)
    fix = lambda i: (0, 0)
    return pl.pallas_call(
        _router_kernel,
        grid=(m // tm,),
        in_specs=[pl.BlockSpec((tm, D_MODEL), row), pl.BlockSpec((1, D_MODEL), fix),
                  pl.BlockSpec((D_MODEL, LANES), fix)],
        out_specs=[pl.BlockSpec((tm, LANES), row), pl.BlockSpec((8, LANES), fix)],
        out_shape=[jax.ShapeDtypeStruct((m, LANES), F32), jax.ShapeDtypeStruct((8, LANES), F32)],
        scratch_shapes=[pltpu.VMEM((8, LANES), F32)],
        compiler_params=_params(("arbitrary",), 32),
        name="router",
    )(x, g, wr)


def _invert_kernel(pos_ref, tab_ref):
    i = pl.program_id(0)
    ta = pos_ref.shape[-1]

    @pl.when(i == 0)
    def _():
        tab_ref[...] = jnp.full(tab_ref.shape, -1, jnp.int32)

    lane = lax.broadcasted_iota(jnp.int32, (1, LANES), 1)
    unroll = 8

    def body(jj, carry):
        for u in range(unroll):
            j = jj * unroll + u
            p = pos_ref[0, 0, j]
            row = lax.shift_right_logical(p, 7)
            pltpu.store(tab_ref.at[pl.ds(row, 1), :], jnp.broadcast_to(i * ta + j, (1, LANES)),
                        mask=lane == (p & (LANES - 1)))
        return carry

    lax.fori_loop(0, ta // unroll, body, 0)


def _invert(pos, n_rows, ta):
    n = pos.shape[0]
    assert n % ta == 0 and n_rows % LANES == 0
    return pl.pallas_call(
        _invert_kernel,
        grid=(n // ta,),
        in_specs=[pl.BlockSpec((1, 1, ta), lambda i: (i, 0, 0), memory_space=pltpu.SMEM)],
        out_specs=pl.BlockSpec((n_rows // LANES, LANES), lambda i: (0, 0)),
        out_shape=jax.ShapeDtypeStruct((n_rows // LANES, LANES), jnp.int32),
        compiler_params=_params(("arbitrary",), 16),
        name="moe_invert",
    )(pos.reshape(n // ta, 1, ta)).reshape(n_rows)


def _experts_kernel(te_ref, nused_ref, src0_ref, src_next_ref, dst_prev_ref, x_hbm, g_ref,
                    wg_hbm, wu_hbm, wd_hbm, y_hbm, xbuf, acc_ref, hn_ref, wg_buf, wu_buf, wd_buf,
                    gsem, ssem, zsem, wsem, *, n_real, n_trash):
    t = pl.program_id(0)
    nused = nused_ref[0]
    nf, _, tf = wg_buf.shape
    tm = hn_ref.shape[0]
    chunk = tm // nf
    slot = t % 2
    other = 1 - slot

    def gather_row(idx_ref, j, s):
        return pltpu.make_async_copy(x_hbm.at[pl.ds(idx_ref[0, 0, j], 1)],
                                     xbuf.at[s, pl.ds(j, 1)], gsem.at[s])

    def scatter_row(j, s):
        return pltpu.make_async_copy(acc_ref.at[s, pl.ds(j, 1)],
                                     y_hbm.at[pl.ds(dst_prev_ref[0, 0, j], 1)], ssem.at[s])

    def issue_gather(row0):
        for j in range(chunk):
            gather_row(src_next_ref, row0 + j, other).start(priority=j % 2)

    def issue_scatter(row0):
        for j in range(chunk):
            scatter_row(row0 + j, other).start(priority=j % 2)

    def compute(f):
        hn = hn_ref[...]
        gate = _dot(hn, wg_buf[f])
        up = _dot(hn, wu_buf[f])
        act = (gate * _sigmoid(gate) * up).astype(BF16)
        acc_ref[slot] += _dot(act, wd_buf[f])

    def chunks(body):
        def step(f, carry):
            body(f, f * chunk)
            return carry
        lax.fori_loop(0, nf, step, 0)

    @pl.when(t == 0)
    def _():
        acc_ref[1] = jnp.zeros(acc_ref.shape[1:], F32)
        fills = [pltpu.make_async_copy(acc_ref.at[1], y_hbm.at[pl.ds(n_real + k * tm, tm)], zsem)
                 for k in range(n_trash // tm)]
        for c in fills:
            c.start()
        for c in fills:
            c.wait()

        def body(jj, carry):
            for u in range(8):
                gather_row(src0_ref, jj * 8 + u, 0).start()
            return carry

        lax.fori_loop(0, tm // 8, body, 0)

    expert = te_ref[t]

    @pl.when((t < nused) & ((t == 0) | (expert != te_ref[jnp.maximum(t - 1, 0)])))
    def _():
        loads = []
        for f in range(nf):
            cols = pl.ds(f * tf, tf)
            loads += [pltpu.make_async_copy(wg_hbm.at[expert, :, cols], wg_buf.at[f], wsem),
                      pltpu.make_async_copy(wu_hbm.at[expert, :, cols], wu_buf.at[f], wsem),
                      pltpu.make_async_copy(wd_hbm.at[expert, cols, :], wd_buf.at[f], wsem)]
        for c in loads:
            c.start()
        for c in loads:
            c.wait()

    @pl.when(t <= nused)
    def _():
        pltpu.make_async_copy(x_hbm.at[pl.ds(0, tm)], xbuf.at[slot], gsem.at[slot]).wait()

    @pl.when((t >= 2) & (t - 2 < nused))
    def _():
        pltpu.make_async_copy(acc_ref.at[slot], y_hbm.at[pl.ds(0, tm)], ssem.at[slot]).wait()

    @pl.when(t < nused)
    def _():
        hn_ref[...] = _rms(xbuf[slot], g_ref[...]).astype(BF16)
        acc_ref[slot] = jnp.zeros(acc_ref.shape[1:], F32)

    @pl.when(t == 0)
    def _():
        def body(f, row0):
            issue_gather(row0)
            compute(f)
        chunks(body)

    @pl.when((t >= 1) & (t < nused))
    def _():
        def body(f, row0):
            issue_gather(row0)
            issue_scatter(row0)
            compute(f)
        chunks(body)

    @pl.when(t == nused)
    def _():
        chunks(lambda f, row0: issue_scatter(row0))


def _experts(tile_expert, nused, src, dst, x, g, wg, wu, wd, tm, tf):
    m = x.shape[0]
    ntile = src.shape[0]
    n_trash = N_EXPERTS * tm
    nf = D_FF // tf
    assert tm % nf == 0 and D_FF % tf == 0
    idx = lambda im: pl.BlockSpec((1, 1, tm), im, memory_space=pltpu.SMEM)
    any_spec = pl.BlockSpec(memory_space=pl.ANY)
    kern = functools.partial(_experts_kernel, n_real=2 * m, n_trash=n_trash)
    return pl.pallas_call(
        kern,
        grid_spec=pltpu.PrefetchScalarGridSpec(
            num_scalar_prefetch=2,
            grid=(ntile,),
            in_specs=[
                idx(lambda t, te, nu: (0, 0, 0)),
                idx(lambda t, te, nu: (jnp.minimum(t + 1, ntile - 1), 0, 0)),
                idx(lambda t, te, nu: (jnp.maximum(t - 1, 0), 0, 0)),
                any_spec,
                pl.BlockSpec((1, D_MODEL), lambda t, te, nu: (0, 0)),
                any_spec, any_spec, any_spec,
            ],
            out_specs=any_spec,
            scratch_shapes=[pltpu.VMEM((2, tm, D_MODEL), F32), pltpu.VMEM((2, tm, D_MODEL), F32),
                            pltpu.VMEM((tm, D_MODEL), BF16),
                            pltpu.VMEM((nf, D_MODEL, tf), BF16), pltpu.VMEM((nf, D_MODEL, tf), BF16),
                            pltpu.VMEM((nf, tf, D_MODEL), BF16),
                            pltpu.SemaphoreType.DMA((2,)), pltpu.SemaphoreType.DMA((2,)),
                            pltpu.SemaphoreType.DMA(()), pltpu.SemaphoreType.DMA(())],
        ),
        out_shape=jax.ShapeDtypeStruct((2 * m + n_trash, D_MODEL), F32),
        compiler_params=_params(("arbitrary",), 56),
        name="moe_experts",
    )(tile_expert, nused, src, src, dst, x, g, wg, wu, wd)


def _combine_kernel(x_ref, route_ref, y1_ref, y2_ref, outa_ref, outb_ref, *, na):
    route = route_ref[...]
    p1 = route[:, _R_P1:_R_P1 + 1]
    p2 = route[:, _R_P2:_R_P2 + 1]
    out = x_ref[...] + (p1 * y1_ref[...] + p2 * y2_ref[...])
    first = pl.program_id(0) < na

    @pl.when(first)
    def _():
        outa_ref[...] = out

    @pl.when(jnp.logical_not(first))
    def _():
        outb_ref[...] = out


def _combine(x, route, y, tc, m1):
    m = x.shape[0]
    na = m1 // tc
    row = lambda i: (i, 0)
    blk = pl.BlockSpec((tc, D_MODEL), row)
    return pl.pallas_call(
        functools.partial(_combine_kernel, na=na),
        grid=(m // tc,),
        in_specs=[blk, pl.BlockSpec((tc, LANES), row), blk,
                  pl.BlockSpec((tc, D_MODEL), lambda i: (i + m // tc, 0))],
        out_specs=_pair_specs(tc, na),
        out_shape=[jax.ShapeDtypeStruct((m1, D_MODEL), F32),
                   jax.ShapeDtypeStruct((m - m1, D_MODEL), F32)],
        compiler_params=_params(("arbitrary",), 32),
        name="moe_combine",
    )(x, route, y, y)


def _moe(x, g, wr, wg, wu, wd, tm_route, tm, tf, m1):
    m = x.shape[0]
    route, counts = _router(x, g, wr, tm_route)
    e1 = route[:, _R_E1].astype(jnp.int32)
    e2 = route[:, _R_E2].astype(jnp.int32)
    n_e = counts[0, :N_EXPERTS].astype(jnp.int32)
    tiles_e = (n_e + tm - 1) // tm
    tile_end = jnp.cumsum(tiles_e)
    row_start = (tile_end - tiles_e) * tm
    pos = jnp.concatenate([row_start[e1] + route[:, _R_RANK1].astype(jnp.int32),
                           row_start[e2] + route[:, _R_RANK2].astype(jnp.int32)])
    ntile = pl.cdiv(2 * m, tm) + N_EXPERTS + 2
    nused = tile_end[-1:]
    tile_ids = jnp.minimum(jnp.arange(ntile, dtype=jnp.int32), nused - 1)
    tile_expert = jnp.sum(tile_ids[:, None] >= tile_end[None, :], axis=1).astype(jnp.int32)
    rows = jnp.arange(ntile * tm, dtype=jnp.int32)
    spare = 2 * m + rows - jnp.cumsum(n_e)[jnp.repeat(tile_expert, tm)]
    spare = jnp.clip(spare, 2 * m, 2 * m + N_EXPERTS * tm - 1)
    assign = _invert(pos, ntile * tm, tm_route)
    src = jnp.where(assign >= m, assign - m, jnp.maximum(assign, 0))
    dst = jnp.where(assign >= 0, assign, spare)
    y = _experts(tile_expert, nused.astype(jnp.int32), src.reshape(ntile, 1, tm),
                 dst.reshape(ntile, 1, tm), x, g, wg, wu, wd, tm, tf)
    return tuple(_combine(x, route, y, tm_route, m1))


def _mlstm_gate_perm():
    perm = []
    for pair in range(M_HEADS // 2):
        for hh in range(2):
            for gtype in range(4):
                perm.append(gtype * M_HEADS + 2 * pair + hh)
    return jnp.array(perm, jnp.int32)


def kernel(x_prompt, x_sample, norm_mix, norm_ffn, mlstm_w_in, mlstm_gate_bias, mlstm_head_gain,
           mlstm_w_out, attn_w_in, attn_q_gain, attn_k_gain, attn_sink, attn_w_out, ffn_w_gate,
           ffn_w_up, ffn_w_down, moe_w_router, moe_w_gate, moe_w_up, moe_w_down):
    b1, s1, _ = x_prompt.shape
    b2, s2, _ = x_sample.shape
    m1 = b1 * s1
    x = (x_prompt.reshape(m1, D_MODEL), x_sample.reshape(b2 * s2, D_MODEL))
    m = m1 + b2 * s2
    seq_gcd = math.gcd(s1, s2)
    tm = _tile(seq_gcd, 512)
    rb = _tile(seq_gcd, 1024)
    tm_ffn = _tile(m, 1024)
    tf = 512
    tm_moe = LANES * (D_FF // tf)
    depth = norm_mix.shape[0]
    vec = lambda a: a.astype(F32).reshape(1, -1)
    nq = M_HEADS * M_QK
    nv = M_HEADS * M_V

    tabs = _rope_tables(max(s1, s2), tm)
    group_mean = jnp.where((jnp.arange(LANES)[:, None] // A_HD) == (jnp.arange(LANES)[None, :] // A_HD),
                           1.0 / A_HD, 0.0).astype(BF16)
    gate_perm = _mlstm_gate_perm()

    for i in range(depth):
        j = i // 2
        if i % 2 == 0:
            w_in = mlstm_w_in[j]
            wqT = w_in[:, :nq].T.astype(BF16)
            wk = w_in[:, nq:2 * nq].astype(BF16)
            wv = w_in[:, 2 * nq:2 * nq + nv].astype(BF16)
            wo = w_in[:, 2 * nq + nv:2 * nq + 2 * nv].astype(BF16)
            wgT = w_in[:, 2 * nq + 2 * nv:][:, gate_perm].T.astype(BF16)
            bgT = mlstm_gate_bias[j].astype(F32)[gate_perm].reshape(-1, 1)
            qT, kT, k, v, o, gT = _mlstm_in(x, vec(norm_mix[i]), wqT, wk.T, wk, wv, wo, wgT, bgT, tm)
            hf, hb = _mlstm_core(qT, kT, k, v, gT, rb, m1, s1, s2)
            x = _mlstm_out(hf, hb, o, vec(mlstm_head_gain[j]), mlstm_w_out[j].astype(BF16), x, tm)
            x = _ffn(x, vec(norm_ffn[i]), ffn_w_gate[j].astype(BF16), ffn_w_up[j].astype(BF16),
                     ffn_w_down[j].astype(BF16), tm_ffn, tf)
        else:
            w_in = attn_w_in[j]
            a_q = A_Q_HEADS * A_HD
            a_kv = A_KV_HEADS * A_HD
            qg = jnp.tile(attn_q_gain[j].astype(F32), LANES // A_HD).reshape(1, LANES)
            kg = jnp.tile(attn_k_gain[j].astype(F32), LANES // A_HD).reshape(1, LANES)
            q, k, v = _attn_in(x, vec(norm_mix[i]), w_in[:, :a_q].astype(BF16),
                               w_in[:, a_q:a_q + a_kv].astype(BF16), w_in[:, a_q + a_kv:].astype(BF16),
                               qg, kg, group_mean, tabs, tm, m1, s1, s2)
            att = _attn(attn_sink[j].astype(F32), q, k, v, m1, s1, s2)
            x = _proj_res(att, attn_w_out[j].astype(BF16), x, tm)
            wr = jnp.pad(moe_w_router[j], ((0, 0), (0, LANES - N_EXPERTS))).astype(BF16)
            x = _moe(x, vec(norm_ffn[i]), wr, moe_w_gate[j].astype(BF16), moe_w_up[j].astype(BF16),
                     moe_w_down[j].astype(BF16), tm, tm_moe, tf, m1)

    xa, xb = x if isinstance(x, tuple) else (x[:m1], x[m1:])
    return xa.reshape(b1, s1, D_MODEL), xb.reshape(b2, s2, D_MODEL)
```

```python
import functools
import math

import jax
import jax.numpy as jnp
from jax import lax
from jax.experimental import pallas as pl
from jax.experimental.pallas import tpu as pltpu

F32 = jnp.float32
BF16 = jnp.bfloat16

D_MODEL = 1024
EPS = 1e-6
M_HEADS = 8
M_QK = 64
M_V = 128
M_CHUNK = 128
A_Q_HEADS = 16
A_KV_HEADS = 4
A_HD = 64
A_GROUP = A_Q_HEADS // A_KV_HEADS
A_BLOCK = 128
WINDOW = 128
ROPE_THETA = 500000.0
ROPE_DIM = 16
D_FF = 3584
N_EXPERTS = 8

LANES = 128
_A_KV_COLS = A_KV_HEADS * 2 * LANES
VMEM_BYTES_V7X = 64 * 1024 * 1024

_NT = (((1,), (1,)), ((), ()))


def _params(semantics, vmem_mb):
    assert vmem_mb * 1024 * 1024 < VMEM_BYTES_V7X
    return pltpu.CompilerParams(dimension_semantics=semantics,
                                vmem_limit_bytes=vmem_mb * 1024 * 1024)


def _tile(total, pref):
    t = min(pref, total)
    t -= t % LANES
    while total % t:
        t -= LANES
    return t


def _rms(x, g):
    ms = jnp.mean(x * x, axis=-1, keepdims=True)
    return x * lax.rsqrt(ms + EPS) * g


def _sigmoid(x):
    return 1.0 / (1.0 + jnp.exp(-x))


def _dot(a, b):
    return jnp.dot(a, b, preferred_element_type=F32)


def _seq_pos(row0, m1, s1, s2):
    first = row0 < m1
    pos = jnp.where(first, row0 % s1, (row0 - m1) % s2)
    return pos, jnp.where(first, s1, s2)


def _pair_specs(tm, na):
    return [pl.BlockSpec((tm, D_MODEL), lambda i: (jnp.minimum(i, na - 1), 0)),
            pl.BlockSpec((tm, D_MODEL), lambda i: (jnp.maximum(i - na, 0), 0))]


def _pair_tile(xa_ref, xb_ref, na):
    return jnp.where(pl.program_id(0) < na, xa_ref[...], xb_ref[...])


def _as_pair(x, tm):
    if isinstance(x, tuple):
        return x[0], x[1], x[0].shape[0] // tm
    return x, x, x.shape[0] // tm


def _mlstm_in_kernel(xa_ref, xb_ref, g_ref, wqT_ref, wkT_ref, wk_ref, wv_ref, wo_ref, wgT_ref, bgT_ref,
                     qT_ref, kT_ref, k_ref, v_ref, o_ref, st_ref, *, na):
    hn = _rms(_pair_tile(xa_ref, xb_ref, na), g_ref[...]).astype(BF16)
    gT = lax.dot_general(wgT_ref[...], hn, _NT, preferred_element_type=F32) + bgT_ref[...]
    log_sig = jnp.minimum(gT, 0.0) - jnp.log1p(jnp.exp(-jnp.abs(gT)))
    row = lax.broadcasted_iota(jnp.int32, gT.shape, 0)
    gT = jnp.where(row % 2 == 1, log_sig, gT)
    nchunk = kT_ref.shape[0]
    for j in range(nchunk):
        cols = slice(j * M_CHUNK, (j + 1) * M_CHUNK)
        for pair in range(M_HEADS // 2):
            stats = _mlstm_gate_stats(gT[pair * 8:(pair + 1) * 8, cols])
            st_ref[j, pair * _N_STATS * 8:(pair + 1) * _N_STATS * 8, :] = jnp.concatenate(stats, axis=0)
    k_ref[...] = _dot(hn, wk_ref[...]).astype(BF16)
    v_ref[...] = _dot(hn, wv_ref[...]).astype(BF16)
    o_ref[...] = _dot(hn, wo_ref[...]).astype(BF16)
    qT = lax.dot_general(wqT_ref[...], hn, _NT, preferred_element_type=F32) * (M_QK ** -0.5)
    qT = qT.astype(BF16)
    kT = lax.dot_general(wkT_ref[...], hn, _NT, preferred_element_type=F32).astype(BF16)
    for j in range(nchunk):
        cols = slice(j * M_CHUNK, (j + 1) * M_CHUNK)
        qT_ref[j] = qT[:, cols]
        kT_ref[j] = kT[:, cols]


def _mlstm_in(x, g, wqT, wkT, wk, wv, wo, wgT, bgT, tm):
    xa, xb, na = _as_pair(x, tm)
    m = xa.shape[0] + xb.shape[0] if isinstance(x, tuple) else xa.shape[0]
    nq = M_HEADS * M_QK
    nv = M_HEADS * M_V
    ng = 4 * M_HEADS
    row = lambda i: (i, 0)
    fix = lambda i: (0, 0)
    chunked = lambda n: pl.BlockSpec((tm // M_CHUNK, n, M_CHUNK), lambda i: (i, 0, 0))
    chunked_shape = lambda n, dt: jax.ShapeDtypeStruct((m // M_CHUNK, n, M_CHUNK), dt)
    return pl.pallas_call(
        functools.partial(_mlstm_in_kernel, na=na),
        grid=(m // tm,),
        in_specs=_pair_specs(tm, na) + [
            pl.BlockSpec((1, D_MODEL), fix),
            pl.BlockSpec((nq, D_MODEL), fix),
            pl.BlockSpec((nq, D_MODEL), fix),
            pl.BlockSpec((D_MODEL, nq), fix),
            pl.BlockSpec((D_MODEL, nv), fix),
            pl.BlockSpec((D_MODEL, nv), fix),
            pl.BlockSpec((ng, D_MODEL), fix),
            pl.BlockSpec((ng, 1), fix),
        ],
        out_specs=[chunked(nq), chunked(nq), pl.BlockSpec((tm, nq), row),
                   pl.BlockSpec((tm, nv), row), pl.BlockSpec((tm, nv), row), chunked(_N_STATS * ng)],
        out_shape=[chunked_shape(nq, BF16), chunked_shape(nq, BF16),
                   jax.ShapeDtypeStruct((m, nq), BF16), jax.ShapeDtypeStruct((m, nv), BF16),
                   jax.ShapeDtypeStruct((m, nv), BF16), chunked_shape(_N_STATS * ng, F32)],
        compiler_params=_params(("parallel",), 40),
        name="mlstm_in",
    )(xa, xb, g, wqT, wkT, wk, wv, wo, wgT, bgT)


_N_STATS = 6


def _lane_scan(x, combine, identity, reverse):
    n = x.shape[-1]
    lane = lax.broadcasted_iota(jnp.int32, x.shape, 1)
    sh = 1
    while sh < n:
        if reverse:
            moved = jnp.where(lane < n - sh, pltpu.roll(x, n - sh, 1), identity)
        else:
            moved = jnp.where(lane >= sh, pltpu.roll(x, sh, 1), identity)
        x = combine(x, moved)
        sh *= 2
    return x


def _mlstm_gate_stats(gates):
    row = lax.broadcasted_iota(jnp.int32, gates.shape, 0)
    bwd_row = (row % 4) >= 2
    add = lambda x, y: x + y
    cum = jnp.where(bwd_row, _lane_scan(gates, add, 0.0, True), _lane_scan(gates, add, 0.0, False))
    b = pltpu.roll(cum, 7, 0)
    total = jnp.broadcast_to(jnp.sum(gates, axis=1, keepdims=True), gates.shape)
    g = pltpu.roll(total, 7, 0)
    r = gates - b
    pm = jnp.where(bwd_row, _lane_scan(r, jnp.maximum, -jnp.inf, True),
                   _lane_scan(r, jnp.maximum, -jnp.inf, False))
    a = g + r
    a_max = jnp.broadcast_to(jnp.max(a, axis=1, keepdims=True), gates.shape)
    return b, r, pm, g, a_max, jnp.exp(a - a_max)


def _mlstm_unit_pre(qT, kT, k, v2, stats, reverse):
    L = M_CHUNK
    s_idx = lax.broadcasted_iota(jnp.int32, (L, L), 0)
    t_idx = lax.broadcasted_iota(jnp.int32, (L, L), 1)
    keep = (s_idx >= t_idx) if reverse else (s_idx <= t_idx)
    head_a_lanes = lax.broadcasted_iota(jnp.int32, (L, LANES), 1) < M_QK
    head_a_rows = lax.broadcasted_iota(jnp.int32, (LANES, L), 0) < M_QK
    zeros_k = jnp.zeros_like(k)
    k2 = jnp.concatenate([jnp.where(head_a_lanes, k, zeros_k), jnp.where(head_a_lanes, zeros_k, k)],
                         axis=0)
    s2 = _dot(k2, qT)
    sT, den0 = [], []
    for hd in range(2):
        _, r, pm, _, _, _ = stats[hd]
        r_col = jnp.broadcast_to(r, (L, L)).T
        sT.append(s2[hd * L:(hd + 1) * L] * jnp.exp(jnp.where(keep, r_col - pm, -jnp.inf)))
        den0.append(jnp.sum(sT[hd], axis=0, keepdims=True))
    w = jnp.where(head_a_rows, stats[0][5], stats[1][5])
    ones = jnp.ones((L, M_V), BF16)
    kv = _dot((kT.astype(F32) * w).astype(BF16), jnp.concatenate([v2, ones], axis=1))
    col = lax.broadcasted_iota(jnp.int32, kv.shape, 1)
    row_a = lax.broadcasted_iota(jnp.int32, kv.shape, 0) < M_QK
    other_v = (row_a & (col >= M_V) & (col < 2 * M_V)) | (~row_a & (col < M_V))
    return sT, den0, jnp.where(other_v, 0.0, kv)


def _mlstm_unit_post(qT, v2, stats, pre, c_pair, m_rows):
    L = M_CHUNK
    sT, den0, kv = pre
    head_a_rows = lax.broadcasted_iota(jnp.int32, (LANES, L), 0) < M_QK
    row_a = lax.broadcasted_iota(jnp.int32, kv.shape, 0) < M_QK
    qT_f = qT.astype(F32)
    qn_all = qT_f * c_pair[:, 2 * M_V:]
    s_scale, q_scale, m_new, decay, scale = [], [], [], [], []
    for hd in range(2):
        b, _, pm, g, a_max, _ = stats[hd]
        m_row = m_rows[hd]
        u = jnp.maximum(pm, m_row)
        c1 = jnp.exp(pm - u)
        e = jnp.exp(m_row - u)
        qn = jnp.sum(qn_all[hd * M_QK:(hd + 1) * M_QK], axis=0, keepdims=True)
        den = c1 * den0[hd] + e * qn
        inv = 1.0 / jnp.maximum(jnp.abs(den), jnp.exp(-(b + u)))
        s_scale.append(c1 * inv)
        q_scale.append(e * inv)
        m_new.append(jnp.maximum(g + m_row, a_max))
        decay.append(jnp.exp(g + m_row - m_new[hd])[:, 0:1])
        scale.append(jnp.exp(a_max - m_new[hd])[:, 0:1])
    lhsT = jnp.concatenate([sT[0] * s_scale[0], sT[1] * s_scale[1],
                            qT_f * jnp.where(head_a_rows, q_scale[0], q_scale[1])],
                           axis=0).astype(BF16)
    zeros_v = jnp.zeros((L, M_V), BF16)
    rhs = jnp.concatenate([jnp.concatenate([v2[:, :M_V], zeros_v], axis=1),
                           jnp.concatenate([zeros_v, v2[:, M_V:]], axis=1),
                           c_pair[:, :2 * M_V].astype(BF16)], axis=0)
    h = lax.dot_general(lhsT, rhs, (((0,), (0,)), ((), ())), preferred_element_type=F32)
    c_new = jnp.where(row_a, decay[0], decay[1]) * c_pair + jnp.where(row_a, scale[0], scale[1]) * kv
    return h, c_new, m_new


def _mlstm_core_kernel(qTf_ref, kTf_ref, kf_ref, vf_ref, sf_ref, qTb_ref, kTb_ref, kb_ref, vb_ref,
                       sb_ref, hf_ref, hb_ref, c_ref, m_ref, *, nchunk, rb, m1, s1, s2):
    i = pl.program_id(1)
    nblk = pl.num_programs(1)
    pos_f, _ = _seq_pos(i * rb, m1, s1, s2)
    pos_b, len_b = _seq_pos((nblk - 1 - i) * rb, m1, s1, s2)

    def reset(dirn):
        c_ref[dirn] = jnp.zeros(c_ref.shape[1:], F32)
        for hd in range(2):
            m_ref[hd * 2 + dirn] = jnp.zeros(m_ref.shape[1:], F32)

    pl.when(pos_f == 0)(lambda: reset(0))
    pl.when(pos_b + rb == len_b)(lambda: reset(1))

    c_state = [c_ref[dirn] for dirn in range(2)]
    m_state = [[m_ref[hd * 2 + dirn][0:1] for hd in range(2)] for dirn in range(2)]
    refs = ((qTf_ref, kTf_ref, kf_ref, vf_ref, sf_ref, hf_ref),
            (qTb_ref, kTb_ref, kb_ref, vb_ref, sb_ref, hb_ref))

    def unit(step, dirn):
        qT_ref, kT_ref, k_ref, v_ref, st_ref, out_ref = refs[dirn]
        cc = nchunk - 1 - step if dirn else step
        rows = slice(cc * M_CHUNK, (cc + 1) * M_CHUNK)
        st = st_ref[cc]
        stats = [tuple(st[n * 8 + hd * 4 + dirn * 2:n * 8 + hd * 4 + dirn * 2 + 1]
                       for n in range(_N_STATS)) for hd in range(2)]
        return qT_ref[cc], kT_ref[cc], k_ref[rows, :], v_ref[rows, :], stats, out_ref, rows

    pre = {}
    for step in range(nchunk):
        for dirn in range(2):
            qT, kT, k, v2, stats, _, _ = unit(step, dirn)
            pre[step, dirn] = _mlstm_unit_pre(qT, kT, k, v2, stats, reverse=bool(dirn))
    for step in range(nchunk):
        for dirn in range(2):
            qT, _, _, v2, stats, out_ref, rows = unit(step, dirn)
            h, c_state[dirn], m_state[dirn] = _mlstm_unit_post(
                qT, v2, stats, pre[step, dirn], c_state[dirn], m_state[dirn])
            out_ref[rows, :] = h.astype(out_ref.dtype)

    for dirn in range(2):
        c_ref[dirn] = c_state[dirn]
        for hd in range(2):
            m_ref[hd * 2 + dirn] = jnp.broadcast_to(m_state[dirn][hd], m_ref.shape[1:])


def _mlstm_core(qT, kT, k, v, gT, rb, m1, s1, s2):
    m = k.shape[0]
    nblk = m // rb
    nchunk = rb // M_CHUNK
    npair = M_HEADS // 2
    fwd = lambda p, i: (i, p)
    bwd = lambda p, i: (nblk - 1 - i, p)
    fwd3 = lambda p, i: (i, p, 0)
    bwd3 = lambda p, i: (nblk - 1 - i, p, 0)

    def specs(im2, im3):
        return [
            pl.BlockSpec((nchunk, LANES, M_CHUNK), im3),
            pl.BlockSpec((nchunk, LANES, M_CHUNK), im3),
            pl.BlockSpec((rb, LANES), im2),
            pl.BlockSpec((rb, 2 * M_V), im2),
            pl.BlockSpec((nchunk, _N_STATS * 8, M_CHUNK), im3),
        ]

    kern = functools.partial(_mlstm_core_kernel, nchunk=nchunk, rb=rb, m1=m1, s1=s1, s2=s2)
    return pl.pallas_call(
        kern,
        grid=(npair, nblk),
        in_specs=specs(fwd, fwd3) + specs(bwd, bwd3),
        out_specs=[pl.BlockSpec((rb, 2 * M_V), fwd), pl.BlockSpec((rb, 2 * M_V), bwd)],
        out_shape=[jax.ShapeDtypeStruct((m, M_HEADS * M_V), BF16)] * 2,
        scratch_shapes=[pltpu.VMEM((2, LANES, 3 * M_V), F32), pltpu.VMEM((4, 8, M_CHUNK), F32)],
        compiler_params=_params(("parallel", "arbitrary"), 32),
        name="mlstm_core",
    )(qT, kT, k, v, gT, qT, kT, k, v, gT)


def _mlstm_out_kernel(hf_ref, hb_ref, o_ref, gain_ref, w_ref, xa_ref, xb_ref, out_ref, *, na):
    h = hf_ref[...].astype(F32) + hb_ref[...].astype(F32)
    parts = []
    for hd in range(M_HEADS):
        hh = h[:, hd * M_V:(hd + 1) * M_V]
        ms = jnp.mean(hh * hh, axis=-1, keepdims=True)
        parts.append(hh * lax.rsqrt(ms + EPS))
    hn = jnp.concatenate(parts, axis=1) * gain_ref[...]
    hg = (_sigmoid(o_ref[...].astype(F32)) * hn).astype(BF16)
    out_ref[...] = _pair_tile(xa_ref, xb_ref, na) + _dot(hg, w_ref[...])


def _mlstm_out(hf, hb, o, gain, w, x, tm):
    xa, xb, na = _as_pair(x, tm)
    m = hf.shape[0]
    row = lambda i: (i, 0)
    fix = lambda i: (0, 0)
    blk = pl.BlockSpec((tm, D_MODEL), row)
    return pl.pallas_call(
        functools.partial(_mlstm_out_kernel, na=na),
        grid=(m // tm,),
        in_specs=[blk, blk, blk, pl.BlockSpec((1, D_MODEL), fix),
                  pl.BlockSpec((D_MODEL, D_MODEL), fix)] + _pair_specs(tm, na),
        out_specs=blk,
        out_shape=jax.ShapeDtypeStruct((m, D_MODEL), F32),
        compiler_params=_params(("parallel",), 40),
        name="mlstm_out",
    )(hf, hb, o, gain, w, xa, xb)


def _ffn_kernel(x_ref, g_ref, wg_ref, wu_ref, wd_ref, out_ref, hn_ref, acc_ref):
    f = pl.program_id(1)

    @pl.when(f == 0)
    def _():
        hn_ref[...] = _rms(x_ref[...], g_ref[...]).astype(BF16)
        acc_ref[...] = jnp.zeros_like(acc_ref)

    hn = hn_ref[...]
    gate = _dot(hn, wg_ref[...])
    up = _dot(hn, wu_ref[...])
    act = (gate * _sigmoid(gate) * up).astype(BF16)
    acc_ref[...] += _dot(act, wd_ref[...])

    @pl.when(f == pl.num_programs(1) - 1)
    def _():
        out_ref[...] = x_ref[...] + acc_ref[...]


def _ffn(x, g, wg, wu, wd, tm, tf):
    m = x.shape[0]
    row = lambda i, f: (i, 0)
    return pl.pallas_call(
        _ffn_kernel,
        grid=(m // tm, D_FF // tf),
        in_specs=[
            pl.BlockSpec((tm, D_MODEL), row),
            pl.BlockSpec((1, D_MODEL), lambda i, f: (0, 0)),
            pl.BlockSpec((D_MODEL, tf), lambda i, f: (0, f)),
            pl.BlockSpec((D_MODEL, tf), lambda i, f: (0, f)),
            pl.BlockSpec((tf, D_MODEL), lambda i, f: (f, 0)),
        ],
        out_specs=pl.BlockSpec((tm, D_MODEL), row),
        out_shape=jax.ShapeDtypeStruct((m, D_MODEL), F32),
        scratch_shapes=[pltpu.VMEM((tm, D_MODEL), BF16), pltpu.VMEM((tm, D_MODEL), F32)],
        compiler_params=_params(("parallel", "arbitrary"), 48),
        name="ffn",
    )(x, g, wg, wu, wd)


def _rope_table_kernel(inv_ref, ma_ref, mb_ref, cos_ref, sa_ref, sb_ref):
    rows = cos_ref.shape[0]
    pos = pl.program_id(0) * rows + lax.broadcasted_iota(jnp.int32, (rows, LANES), 0)
    ang = pos.astype(F32) * inv_ref[...]
    sin = jnp.sin(ang)
    cos_ref[...] = jnp.cos(ang)
    sa_ref[...] = sin * ma_ref[...]
    sb_ref[...] = sin * mb_ref[...]


def _rope_tables(smax, rows):
    half = ROPE_DIM // 2
    d = jnp.arange(LANES) % A_HD
    inv = ROPE_THETA ** (-(jnp.arange(half, dtype=F32) * 2.0) / ROPE_DIM)
    inv_lane = jnp.where(d < ROPE_DIM, inv[d % half], 0.0).astype(F32)[None, :]
    ma = jnp.where(d < half, -1.0, 0.0).astype(F32)[None, :]
    mb = jnp.where((d >= half) & (d < ROPE_DIM), 1.0, 0.0).astype(F32)[None, :]
    fix = lambda i: (0, 0)
    vec = pl.BlockSpec((1, LANES), fix)
    tab = pl.BlockSpec((rows, LANES), lambda i: (i, 0))
    return pl.pallas_call(
        _rope_table_kernel,
        grid=(smax // rows,),
        in_specs=[vec, vec, vec],
        out_specs=[tab, tab, tab],
        out_shape=[jax.ShapeDtypeStruct((smax, LANES), F32)] * 3,
        compiler_params=_params(("parallel",), 16),
        name="rope_tables",
    )(inv_lane, ma, mb)


def _attn_in_kernel(x_ref, g_ref, wq_ref, wk_ref, wv_ref, qg_ref, kg_ref, gm_ref,
                    cos_ref, sa_ref, sb_ref, q_ref, k_ref, v_ref):
    hn = _rms(x_ref[...], g_ref[...]).astype(BF16)
    cos = cos_ref[...]
    sa = sa_ref[...]
    sb = sb_ref[...]
    half = ROPE_DIM // 2

    def tiles(x):
        return [x[:, j * LANES:(j + 1) * LANES] for j in range(x.shape[1] // LANES)]

    def norm_rope(xj, gain):
        ms = _dot((xj * xj).astype(BF16), gm_ref[...])
        y = xj * lax.rsqrt(ms + EPS) * gain
        return y * cos + pltpu.roll(y, LANES - half, 1) * sa + pltpu.roll(y, half, 1) * sb

    def store_lo_hi(pair_tiles, out_ref):
        low = lax.broadcasted_iota(jnp.int32, pair_tiles[0].shape, 1) < A_HD
        for j, y in enumerate(pair_tiles):
            for half_idx in range(2):
                mine = jnp.where(low if half_idx == 0 else jnp.logical_not(low), y, 0.0)
                other = pltpu.roll(mine, A_HD, 1)
                lo, hi = (mine, other) if half_idx == 0 else (other, mine)
                col = (2 * j + half_idx) * 2 * LANES
                out_ref[:, col:col + LANES] = lo.astype(BF16)
                out_ref[:, col + LANES:col + 2 * LANES] = hi.astype(BF16)

    for j, xj in enumerate(tiles(_dot(hn, wq_ref[...]))):
        q_ref[:, j * LANES:(j + 1) * LANES] = (norm_rope(xj, qg_ref[...]) * (A_HD ** -0.5)).astype(BF16)
    store_lo_hi([norm_rope(xj, kg_ref[...]) for xj in tiles(_dot(hn, wk_ref[...]))], k_ref)
    store_lo_hi(tiles(_dot(hn, wv_ref[...])), v_ref)


def _attn_in(x, g, wq, wk, wv, qg, kg, gm, tabs, tm, m1, s1, s2):
    m = x.shape[0]
    nq = A_Q_HEADS * A_HD
    nkv = A_KV_HEADS * A_HD
    row = lambda i: (i, 0)
    fix = lambda i: (0, 0)

    def tab_map(i):
        pos, _ = _seq_pos(i * tm, m1, s1, s2)
        return (pos // tm, 0)

    tab = pl.BlockSpec((tm, LANES), tab_map)
    vec = pl.BlockSpec((1, LANES), fix)
    return pl.pallas_call(
        _attn_in_kernel,
        grid=(m // tm,),
        in_specs=[
            pl.BlockSpec((tm, D_MODEL), row),
            pl.BlockSpec((1, D_MODEL), fix),
            pl.BlockSpec((D_MODEL, nq), fix),
            pl.BlockSpec((D_MODEL, nkv), fix),
            pl.BlockSpec((D_MODEL, nkv), fix),
            vec, vec,
            pl.BlockSpec((LANES, LANES), fix),
            tab, tab, tab,
        ],
        out_specs=[pl.BlockSpec((tm, nq), row), pl.BlockSpec((tm, _A_KV_COLS), row),
                   pl.BlockSpec((tm, _A_KV_COLS), row)],
        out_shape=[jax.ShapeDtypeStruct((m, nq), BF16), jax.ShapeDtypeStruct((m, _A_KV_COLS), BF16),
                   jax.ShapeDtypeStruct((m, _A_KV_COLS), BF16)],
        compiler_params=_params(("parallel",), 40),
        name="attn_in",
    )(x, g, wq, wk, wv, qg, kg, gm, *tabs)


def _attn_kernel(sink_ref, q_ref, kp_ref, kc_ref, kn_ref, vp_ref, vc_ref, vn_ref, out_ref,
                 *, m1, s1, s2):
    blk = A_BLOCK
    pos0, slen = _seq_pos(pl.program_id(0) * blk, m1, s1, s2)
    prev_ok = pos0 > 0
    next_ok = pos0 + blk < slen
    t = lax.broadcasted_iota(jnp.int32, (blk, 3 * blk), 0)
    c = lax.broadcasted_iota(jnp.int32, (blk, 3 * blk), 1)
    valid = (jnp.abs(c - blk - t) <= WINDOW) & ((c >= blk) | prev_ok) & ((c < 2 * blk) | next_ok)

    kcat = jnp.concatenate([kp_ref[...], kc_ref[...], kn_ref[...]], axis=0)
    vcat = jnp.concatenate([vp_ref[...], vc_ref[...], vn_ref[...]], axis=0)
    lane_q = lax.broadcasted_iota(jnp.int32, (blk, LANES), 1)

    def softmax_parts(s, sink):
        s = jnp.concatenate([jnp.where(valid[:, :blk], s[:, :blk], -jnp.inf), s[:, blk:2 * blk],
                             jnp.where(valid[:, 2 * blk:], s[:, 2 * blk:], -jnp.inf)], axis=1)
        m = jnp.maximum(jnp.max(s, axis=1, keepdims=True), sink)
        p = jnp.exp(s - m)
        denom = jnp.sum(p, axis=1, keepdims=True) + jnp.exp(sink - m)
        return p.astype(BF16), denom

    def lo_hi(cat, h):
        col = h * 2 * LANES
        return jnp.concatenate([cat[:, col:col + LANES], cat[:, col + LANES:col + 2 * LANES]], axis=0)

    npair = A_GROUP // 2
    kk = [lo_hi(kcat, h) for h in range(A_KV_HEADS)]
    vv = [lo_hi(vcat, h) for h in range(A_KV_HEADS)]
    def score(h):
        q_pairs = jnp.concatenate([q_ref[:, (h * npair + j) * LANES:(h * npair + j + 1) * LANES]
                                   for j in range(npair)], axis=0)
        return lax.dot_general(q_pairs, kk[h], _NT, preferred_element_type=F32)

    scores = {0: score(0)}
    for h in range(A_KV_HEADS):
        if h + 1 < A_KV_HEADS:
            scores[h + 1] = score(h + 1)
        p_rows, inv_rows = [], []
        for j in range(npair):
            parts = [softmax_parts(scores[h][j * blk:(j + 1) * blk, par * 3 * blk:(par + 1) * 3 * blk],
                                   sink_ref[(h * npair + j) * 2 + par]) for par in range(2)]
            p_rows.append(jnp.concatenate([parts[0][0], parts[1][0]], axis=1))
            inv_rows.append(jnp.where(lane_q < A_HD, 1.0 / parts[0][1], 1.0 / parts[1][1]))
        o = _dot(jnp.concatenate(p_rows, axis=0), vv[h])
        for j in range(npair):
            grp = h * npair + j
            out_ref[:, grp * LANES:(grp + 1) * LANES] = (o[j * blk:(j + 1) * blk] * inv_rows[j]).astype(BF16)


def _attn(sink, q, k, v, m1, s1, s2):
    m = q.shape[0]
    blk = A_BLOCK
    nblk = m // blk
    cur = lambda i: (i, 0)
    prev = lambda i: (jnp.maximum(i - 1, 0), 0)
    nxt = lambda i: (jnp.minimum(i + 1, nblk - 1), 0)
    kv = lambda im: pl.BlockSpec((blk, _A_KV_COLS), im)
    kern = functools.partial(_attn_kernel, m1=m1, s1=s1, s2=s2)
    return pl.pallas_call(
        kern,
        grid=(nblk,),
        in_specs=[pl.BlockSpec(memory_space=pltpu.SMEM),
                  pl.BlockSpec((blk, D_MODEL), cur), kv(prev), kv(cur), kv(nxt),
                  kv(prev), kv(cur), kv(nxt)],
        out_specs=pl.BlockSpec((blk, D_MODEL), cur),
        out_shape=jax.ShapeDtypeStruct((m, D_MODEL), BF16),
        compiler_params=_params(("parallel",), 32),
        name="attn",
    )(sink, q, k, k, k, v, v, v)


def _proj_res_kernel(a_ref, w_ref, x_ref, out_ref):
    out_ref[...] = x_ref[...] + _dot(a_ref[...], w_ref[...])


def _proj_res(a, w, x, tm):
    m = x.shape[0]
    row = lambda i: (i, 0)
    blk = pl.BlockSpec((tm, D_MODEL), row)
    return pl.pallas_call(
        _proj_res_kernel,
        grid=(m // tm,),
        in_specs=[blk, pl.BlockSpec((D_MODEL, D_MODEL), lambda i: (0, 0)), blk],
        out_specs=blk,
        out_shape=jax.ShapeDtypeStruct((m, D_MODEL), F32),
        compiler_params=_params(("parallel",), 32),
        name="attn_out",
    )(a, w, x)


_R_E1, _R_E2, _R_P1, _R_P2, _R_RANK1, _R_RANK2 = range(6)


def _router_kernel(x_ref, g_ref, wr_ref, route_ref, route_t_ref, count_ref, carry_ref):
    @pl.when(pl.program_id(0) == 0)
    def _():
        carry_ref[...] = jnp.zeros_like(carry_ref)

    hn = _rms(x_ref[...], g_ref[...]).astype(BF16)
    logits = _dot(hn, wr_ref[...])
    tm = logits.shape[0]
    lane = lax.broadcasted_iota(jnp.int32, logits.shape, 1).astype(F32)
    logits = jnp.where(lane < N_EXPERTS, logits, -jnp.inf)
    v1 = jnp.max(logits, axis=1, keepdims=True)
    i1 = jnp.min(jnp.where(logits == v1, lane, float(LANES)), axis=1, keepdims=True)
    rest = jnp.where(lane == i1, -jnp.inf, logits)
    v2 = jnp.max(rest, axis=1, keepdims=True)
    i2 = jnp.min(jnp.where(rest == v2, lane, float(LANES)), axis=1, keepdims=True)
    e2 = jnp.exp(v2 - v1)
    p1 = 1.0 / (1.0 + e2)
    p2 = e2 * p1

    sel = jnp.where((lane == i1) | (lane == i2), 1.0, 0.0)
    t_row = lax.broadcasted_iota(jnp.int32, (tm, tm), 0)
    t_col = lax.broadcasted_iota(jnp.int32, (tm, tm), 1)
    earlier = jnp.where(t_col < t_row, 1.0, 0.0).astype(BF16)
    before = _dot(earlier, sel.astype(BF16)) + carry_ref[0:1, :]
    rank1 = jnp.sum(jnp.where(lane == i1, before, 0.0), axis=1, keepdims=True)
    rank2 = jnp.sum(jnp.where(lane == i2, before, 0.0), axis=1, keepdims=True)
    total = carry_ref[0:1, :] + jnp.sum(sel, axis=0, keepdims=True)
    carry_ref[...] = jnp.broadcast_to(total, carry_ref.shape)
    count_ref[...] = jnp.broadcast_to(total, count_ref.shape)

    route = jnp.zeros_like(logits)
    for col, val in ((_R_E1, i1), (_R_E2, i2), (_R_P1, p1), (_R_P2, p2),
                     (_R_RANK1, rank1), (_R_RANK2, rank2)):
        route = jnp.where(lane == float(col), val, route)
    route_ref[...] = route
    route_t_ref[...] = route.T[:8]


def _router(x, g, wr, tm):
    m = x.shape[0]
    row = lambda i: (i, 0)
    fix = lambda i: (0, 0)
    return pl.pallas_call(
        _router_kernel,
        grid=(m // tm,),
        in_specs=[pl.BlockSpec((tm, D_MODEL), row), pl.BlockSpec((1, D_MODEL), fix),
                  pl.BlockSpec((D_MODEL, LANES), fix)],
        out_specs=[pl.BlockSpec((tm, LANES), row), pl.BlockSpec((8, tm), lambda i: (0, i)),
                   pl.BlockSpec((8, LANES), fix)],
        out_shape=[jax.ShapeDtypeStruct((m, LANES), F32), jax.ShapeDtypeStruct((8, m), F32),
                   jax.ShapeDtypeStruct((8, LANES), F32)],
        scratch_shapes=[pltpu.VMEM((8, LANES), F32)],
        compiler_params=_params(("arbitrary",), 32),
        name="router",
    )(x, g, wr)


def _invert_kernel(pos_ref, tab_ref):
    i = pl.program_id(0)
    ta = pos_ref.shape[-1]

    @pl.when(i == 0)
    def _():
        tab_ref[...] = jnp.full(tab_ref.shape, -1, jnp.int32)

    lane = lax.broadcasted_iota(jnp.int32, (1, LANES), 1)
    unroll = 8

    def body(jj, carry):
        for u in range(unroll):
            j = jj * unroll + u
            p = pos_ref[0, 0, j]
            row = lax.shift_right_logical(p, 7)
            pltpu.store(tab_ref.at[pl.ds(row, 1), :], jnp.broadcast_to(i * ta + j, (1, LANES)),
                        mask=lane == (p & (LANES - 1)))
        return carry

    lax.fori_loop(0, ta // unroll, body, 0)


def _invert(pos, n_rows, ta):
    n = pos.shape[0]
    assert n % ta == 0 and n_rows % LANES == 0
    return pl.pallas_call(
        _invert_kernel,
        grid=(n // ta,),
        in_specs=[pl.BlockSpec((1, 1, ta), lambda i: (i, 0, 0), memory_space=pltpu.SMEM)],
        out_specs=pl.BlockSpec((n_rows // LANES, LANES), lambda i: (0, 0)),
        out_shape=jax.ShapeDtypeStruct((n_rows // LANES, LANES), jnp.int32),
        compiler_params=_params(("arbitrary",), 16),
        name="moe_invert",
    )(pos.reshape(n // ta, 1, ta)).reshape(n_rows)


def _experts_kernel(te_ref, nused_ref, src0_ref, src_next_ref, dst_prev_ref, x_hbm, g_ref,
                    wg_hbm, wu_hbm, wd_hbm, y_hbm, xbuf, acc_ref, hn_ref, wg_buf, wu_buf, wd_buf,
                    gsem, ssem, zsem, wsem, *, n_real, n_trash):
    t = pl.program_id(0)
    nused = nused_ref[0]
    nf, _, tf = wg_buf.shape
    tm = hn_ref.shape[0]
    chunk = tm // nf
    slot = t % 2
    other = 1 - slot

    def gather_row(idx_ref, j, s):
        return pltpu.make_async_copy(x_hbm.at[pl.ds(idx_ref[0, 0, j], 1)],
                                     xbuf.at[s, pl.ds(j, 1)], gsem.at[s])

    def scatter_row(j, s):
        return pltpu.make_async_copy(acc_ref.at[s, pl.ds(j, 1)],
                                     y_hbm.at[pl.ds(dst_prev_ref[0, 0, j], 1)], ssem.at[s])

    def issue_gather(row0):
        for j in range(chunk):
            gather_row(src_next_ref, row0 + j, other).start(priority=j % 2)

    def issue_scatter(row0):
        for j in range(chunk):
            scatter_row(row0 + j, other).start(priority=j % 2)

    def compute(f):
        hn = hn_ref[...]
        gate = _dot(hn, wg_buf[f])
        up = _dot(hn, wu_buf[f])
        act = (gate * _sigmoid(gate) * up).astype(BF16)
        acc_ref[slot] += _dot(act, wd_buf[f])

    def chunks(body):
        def step(f, carry):
            body(f, f * chunk)
            return carry
        lax.fori_loop(0, nf, step, 0)

    @pl.when(t == 0)
    def _():
        acc_ref[1] = jnp.zeros(acc_ref.shape[1:], F32)
        fills = [pltpu.make_async_copy(acc_ref.at[1], y_hbm.at[pl.ds(n_real + k * tm, tm)], zsem)
                 for k in range(n_trash // tm)]
        for c in fills:
            c.start()
        for c in fills:
            c.wait()

        def body(jj, carry):
            for u in range(8):
                gather_row(src0_ref, jj * 8 + u, 0).start()
            return carry

        lax.fori_loop(0, tm // 8, body, 0)

    expert = te_ref[t]

    @pl.when((t < nused) & ((t == 0) | (expert != te_ref[jnp.maximum(t - 1, 0)])))
    def _():
        loads = []
        for f in range(nf):
            cols = pl.ds(f * tf, tf)
            loads += [pltpu.make_async_copy(wg_hbm.at[expert, :, cols], wg_buf.at[f], wsem),
                      pltpu.make_async_copy(wu_hbm.at[expert, :, cols], wu_buf.at[f], wsem),
                      pltpu.make_async_copy(wd_hbm.at[expert, cols, :], wd_buf.at[f], wsem)]
        for c in loads:
            c.start()
        for c in loads:
            c.wait()

    @pl.when(t <= nused)
    def _():
        pltpu.make_async_copy(x_hbm.at[pl.ds(0, tm)], xbuf.at[slot], gsem.at[slot]).wait()

    @pl.when((t >= 2) & (t - 2 < nused))
    def _():
        pltpu.make_async_copy(acc_ref.at[slot], y_hbm.at[pl.ds(0, tm)], ssem.at[slot]).wait()

    @pl.when(t < nused)
    def _():
        hn_ref[...] = _rms(xbuf[slot], g_ref[...]).astype(BF16)
        acc_ref[slot] = jnp.zeros(acc_ref.shape[1:], F32)

    @pl.when(t == 0)
    def _():
        def body(f, row0):
            issue_gather(row0)
            compute(f)
        chunks(body)

    @pl.when((t >= 1) & (t < nused))
    def _():
        def body(f, row0):
            issue_gather(row0)
            issue_scatter(row0)
            compute(f)
        chunks(body)

    @pl.when(t == nused)
    def _():
        chunks(lambda f, row0: issue_scatter(row0))


def _experts(tile_expert, nused, src, dst, x, g, wg, wu, wd, tm, tf):
    m = x.shape[0]
    ntile = src.shape[0]
    n_trash = N_EXPERTS * tm
    nf = D_FF // tf
    assert tm % nf == 0 and D_FF % tf == 0
    idx = lambda im: pl.BlockSpec((1, 1, tm), im, memory_space=pltpu.SMEM)
    any_spec = pl.BlockSpec(memory_space=pl.ANY)
    kern = functools.partial(_experts_kernel, n_real=2 * m, n_trash=n_trash)
    return pl.pallas_call(
        kern,
        grid_spec=pltpu.PrefetchScalarGridSpec(
            num_scalar_prefetch=2,
            grid=(ntile,),
            in_specs=[
                idx(lambda t, te, nu: (0, 0, 0)),
                idx(lambda t, te, nu: (jnp.minimum(t + 1, ntile - 1), 0, 0)),
                idx(lambda t, te, nu: (jnp.maximum(t - 1, 0), 0, 0)),
                any_spec,
                pl.BlockSpec((1, D_MODEL), lambda t, te, nu: (0, 0)),
                any_spec, any_spec, any_spec,
            ],
            out_specs=any_spec,
            scratch_shapes=[pltpu.VMEM((2, tm, D_MODEL), F32), pltpu.VMEM((2, tm, D_MODEL), F32),
                            pltpu.VMEM((tm, D_MODEL), BF16),
                            pltpu.VMEM((nf, D_MODEL, tf), BF16), pltpu.VMEM((nf, D_MODEL, tf), BF16),
                            pltpu.VMEM((nf, tf, D_MODEL), BF16),
                            pltpu.SemaphoreType.DMA((2,)), pltpu.SemaphoreType.DMA((2,)),
                            pltpu.SemaphoreType.DMA(()), pltpu.SemaphoreType.DMA(())],
        ),
        out_shape=jax.ShapeDtypeStruct((2 * m + n_trash, D_MODEL), F32),
        compiler_params=_params(("arbitrary",), 56),
        name="moe_experts",
    )(tile_expert, nused, src, src, dst, x, g, wg, wu, wd)


def _combine_kernel(x_ref, route_ref, y1_ref, y2_ref, outa_ref, outb_ref, *, na):
    route = route_ref[...]
    p1 = route[:, _R_P1:_R_P1 + 1]
    p2 = route[:, _R_P2:_R_P2 + 1]
    out = x_ref[...] + (p1 * y1_ref[...] + p2 * y2_ref[...])
    first = pl.program_id(0) < na

    @pl.when(first)
    def _():
        outa_ref[...] = out

    @pl.when(jnp.logical_not(first))
    def _():
        outb_ref[...] = out


def _combine(x, route, y, tc, m1):
    m = x.shape[0]
    na = m1 // tc
    row = lambda i: (i, 0)
    blk = pl.BlockSpec((tc, D_MODEL), row)
    return pl.pallas_call(
        functools.partial(_combine_kernel, na=na),
        grid=(m // tc,),
        in_specs=[blk, pl.BlockSpec((tc, LANES), row), blk,
                  pl.BlockSpec((tc, D_MODEL), lambda i: (i + m // tc, 0))],
        out_specs=_pair_specs(tc, na),
        out_shape=[jax.ShapeDtypeStruct((m1, D_MODEL), F32),
                   jax.ShapeDtypeStruct((m - m1, D_MODEL), F32)],
        compiler_params=_params(("arbitrary",), 32),
        name="moe_combine",
    )(x, route, y, y)


def _moe(x, g, wr, wg, wu, wd, tm_route, tm, tf, m1):
    m = x.shape[0]
    route, route_t, counts = _router(x, g, wr, tm_route)
    e1 = route_t[_R_E1].astype(jnp.int32)
    e2 = route_t[_R_E2].astype(jnp.int32)
    n_e = counts[0, :N_EXPERTS].astype(jnp.int32)
    tiles_e = (n_e + tm - 1) // tm
    tile_end = jnp.cumsum(tiles_e)
    row_start = (tile_end - tiles_e) * tm
    pos = jnp.concatenate([row_start[e1] + route_t[_R_RANK1].astype(jnp.int32),
                           row_start[e2] + route_t[_R_RANK2].astype(jnp.int32)])
    ntile = pl.cdiv(2 * m, tm) + N_EXPERTS + 2
    nused = tile_end[-1:]
    tile_ids = jnp.minimum(jnp.arange(ntile, dtype=jnp.int32), nused - 1)
    tile_expert = jnp.sum(tile_ids[:, None] >= tile_end[None, :], axis=1).astype(jnp.int32)
    rows = jnp.arange(ntile * tm, dtype=jnp.int32)
    spare = 2 * m + rows - jnp.cumsum(n_e)[jnp.repeat(tile_expert, tm)]
    spare = jnp.clip(spare, 2 * m, 2 * m + N_EXPERTS * tm - 1)
    assign = _invert(pos, ntile * tm, tm_route)
    src = jnp.where(assign >= m, assign - m, jnp.maximum(assign, 0))
    dst = jnp.where(assign >= 0, assign, spare)
    y = _experts(tile_expert, nused.astype(jnp.int32), src.reshape(ntile, 1, tm),
                 dst.reshape(ntile, 1, tm), x, g, wg, wu, wd, tm, tf)
    return tuple(_combine(x, route, y, tm_route, m1))


def _mlstm_gate_perm():
    perm = []
    for pair in range(M_HEADS // 2):
        for hh in range(2):
            for gtype in range(4):
                perm.append(gtype * M_HEADS + 2 * pair + hh)
    return jnp.array(perm, jnp.int32)


def kernel(x_prompt, x_sample, norm_mix, norm_ffn, mlstm_w_in, mlstm_gate_bias, mlstm_head_gain,
           mlstm_w_out, attn_w_in, attn_q_gain, attn_k_gain, attn_sink, attn_w_out, ffn_w_gate,
           ffn_w_up, ffn_w_down, moe_w_router, moe_w_gate, moe_w_up, moe_w_down):
    b1, s1, _ = x_prompt.shape
    b2, s2, _ = x_sample.shape
    m1 = b1 * s1
    x = (x_prompt.reshape(m1, D_MODEL), x_sample.reshape(b2 * s2, D_MODEL))
    m = m1 + b2 * s2
    seq_gcd = math.gcd(s1, s2)
    tm = _tile(seq_gcd, 512)
    rb = _tile(seq_gcd, 1024)
    tm_ffn = _tile(m, 1024)
    tf = 512
    tm_moe = LANES * (D_FF // tf)
    depth = norm_mix.shape[0]
    vec = lambda a: a.astype(F32).reshape(1, -1)
    nq = M_HEADS * M_QK
    nv = M_HEADS * M_V

    tabs = _rope_tables(max(s1, s2), tm)
    group_mean = jnp.where((jnp.arange(LANES)[:, None] // A_HD) == (jnp.arange(LANES)[None, :] // A_HD),
                           1.0 / A_HD, 0.0).astype(BF16)
    gate_perm = _mlstm_gate_perm()

    for i in range(depth):
        j = i // 2
        if i % 2 == 0:
            w_in = mlstm_w_in[j]
            wqT = w_in[:, :nq].T.astype(BF16)
            wk = w_in[:, nq:2 * nq].astype(BF16)
            wv = w_in[:, 2 * nq:2 * nq + nv].astype(BF16)
            wo = w_in[:, 2 * nq + nv:2 * nq + 2 * nv].astype(BF16)
            wgT = w_in[:, 2 * nq + 2 * nv:][:, gate_perm].T.astype(BF16)
            bgT = mlstm_gate_bias[j].astype(F32)[gate_perm].reshape(-1, 1)
            qT, kT, k, v, o, gT = _mlstm_in(x, vec(norm_mix[i]), wqT, wk.T, wk, wv, wo, wgT, bgT, tm)
            hf, hb = _mlstm_core(qT, kT, k, v, gT, rb, m1, s1, s2)
            x = _mlstm_out(hf, hb, o, vec(mlstm_head_gain[j]), mlstm_w_out[j].astype(BF16), x, tm)
            x = _ffn(x, vec(norm_ffn[i]), ffn_w_gate[j].astype(BF16), ffn_w_up[j].astype(BF16),
                     ffn_w_down[j].astype(BF16), tm_ffn, tf)
        else:
            w_in = attn_w_in[j]
            a_q = A_Q_HEADS * A_HD
            a_kv = A_KV_HEADS * A_HD
            qg = jnp.tile(attn_q_gain[j].astype(F32), LANES // A_HD).reshape(1, LANES)
            kg = jnp.tile(attn_k_gain[j].astype(F32), LANES // A_HD).reshape(1, LANES)
            q, k, v = _attn_in(x, vec(norm_mix[i]), w_in[:, :a_q].astype(BF16),
                               w_in[:, a_q:a_q + a_kv].astype(BF16), w_in[:, a_q + a_kv:].astype(BF16),
                               qg, kg, group_mean, tabs, tm, m1, s1, s2)
            att = _attn(attn_sink[j].astype(F32), q, k, v, m1, s1, s2)
            x = _proj_res(att, attn_w_out[j].astype(BF16), x, tm)
            wr = jnp.pad(moe_w_router[j], ((0, 0), (0, LANES - N_EXPERTS))).astype(BF16)
            x = _moe(x, vec(norm_ffn[i]), wr, moe_w_gate[j].astype(BF16), moe_w_up[j].astype(BF16),
                     moe_w_down[j].astype(BF16), tm, tm_moe, tf, m1)

    xa, xb = x if isinstance(x, tuple) else (x[:m1], x[m1:])
    return xa.reshape(b1, s1, D_MODEL), xb.reshape(b2, s2, D_MODEL)
```

```python
import functools
import math

import jax
import jax.numpy as jnp
from jax import lax
from jax.experimental import pallas as pl
from jax.experimental.pallas import tpu as pltpu

F32 = jnp.float32
BF16 = jnp.bfloat16

D_MODEL = 1024
EPS = 1e-6
M_HEADS = 8
M_QK = 64
M_V = 128
M_CHUNK = 128
A_Q_HEADS = 16
A_KV_HEADS = 4
A_HD = 64
A_GROUP = A_Q_HEADS // A_KV_HEADS
A_BLOCK = 128
WINDOW = 128
ROPE_THETA = 500000.0
ROPE_DIM = 16
D_FF = 3584
N_EXPERTS = 8

LANES = 128
_A_KV_COLS = A_KV_HEADS * 2 * LANES
VMEM_BYTES_V7X = 64 * 1024 * 1024

_NT = (((1,), (1,)), ((), ()))


def _params(semantics, vmem_mb):
    assert vmem_mb * 1024 * 1024 < VMEM_BYTES_V7X
    return pltpu.CompilerParams(dimension_semantics=semantics,
                                vmem_limit_bytes=vmem_mb * 1024 * 1024)


def _tile(total, pref):
    t = min(pref, total)
    t -= t % LANES
    while total % t:
        t -= LANES
    return t


def _rms(x, g):
    ms = jnp.mean(x * x, axis=-1, keepdims=True)
    return x * lax.rsqrt(ms + EPS) * g


def _sigmoid(x):
    return 1.0 / (1.0 + jnp.exp(-x))


def _dot(a, b):
    return jnp.dot(a, b, preferred_element_type=F32)


def _seq_pos(row0, m1, s1, s2):
    first = row0 < m1
    pos = jnp.where(first, row0 % s1, (row0 - m1) % s2)
    return pos, jnp.where(first, s1, s2)


def _pair_specs(tm, na):
    return [pl.BlockSpec((tm, D_MODEL), lambda i: (jnp.minimum(i, na - 1), 0)),
            pl.BlockSpec((tm, D_MODEL), lambda i: (jnp.maximum(i - na, 0), 0))]


def _pair_tile(xa_ref, xb_ref, na):
    return jnp.where(pl.program_id(0) < na, xa_ref[...], xb_ref[...])


def _as_pair(x, tm):
    if isinstance(x, tuple):
        return x[0], x[1], x[0].shape[0] // tm
    return x, x, x.shape[0] // tm


def _mlstm_in_kernel(xa_ref, xb_ref, g_ref, wqT_ref, wkT_ref, wk_ref, wv_ref, wo_ref, wgT_ref, bgT_ref,
                     qT_ref, kT_ref, k_ref, v_ref, o_ref, st_ref, *, na):
    hn = _rms(_pair_tile(xa_ref, xb_ref, na), g_ref[...]).astype(BF16)
    gT = lax.dot_general(wgT_ref[...], hn, _NT, preferred_element_type=F32) + bgT_ref[...]
    log_sig = jnp.minimum(gT, 0.0) - jnp.log1p(jnp.exp(-jnp.abs(gT)))
    row = lax.broadcasted_iota(jnp.int32, gT.shape, 0)
    gT = jnp.where(row % 2 == 1, log_sig, gT)
    nchunk = kT_ref.shape[0]
    for j in range(nchunk):
        cols = slice(j * M_CHUNK, (j + 1) * M_CHUNK)
        for pair in range(M_HEADS // 2):
            stats = _mlstm_gate_stats(gT[pair * 8:(pair + 1) * 8, cols])
            st_ref[j, pair * _N_STATS * 8:(pair + 1) * _N_STATS * 8, :] = jnp.concatenate(stats, axis=0)
    k_ref[...] = _dot(hn, wk_ref[...]).astype(BF16)
    v_ref[...] = _dot(hn, wv_ref[...]).astype(BF16)
    o_ref[...] = _dot(hn, wo_ref[...]).astype(BF16)
    qT = lax.dot_general(wqT_ref[...], hn, _NT, preferred_element_type=F32) * (M_QK ** -0.5)
    qT = qT.astype(BF16)
    kT = lax.dot_general(wkT_ref[...], hn, _NT, preferred_element_type=F32).astype(BF16)
    for j in range(nchunk):
        cols = slice(j * M_CHUNK, (j + 1) * M_CHUNK)
        qT_ref[j] = qT[:, cols]
        kT_ref[j] = kT[:, cols]


def _mlstm_in(x, g, wqT, wkT, wk, wv, wo, wgT, bgT, tm):
    xa, xb, na = _as_pair(x, tm)
    m = xa.shape[0] + xb.shape[0] if isinstance(x, tuple) else xa.shape[0]
    nq = M_HEADS * M_QK
    nv = M_HEADS * M_V
    ng = 4 * M_HEADS
    row = lambda i: (i, 0)
    fix = lambda i: (0, 0)
    chunked = lambda n: pl.BlockSpec((tm // M_CHUNK, n, M_CHUNK), lambda i: (i, 0, 0))
    chunked_shape = lambda n, dt: jax.ShapeDtypeStruct((m // M_CHUNK, n, M_CHUNK), dt)
    return pl.pallas_call(
        functools.partial(_mlstm_in_kernel, na=na),
        grid=(m // tm,),
        in_specs=_pair_specs(tm, na) + [
            pl.BlockSpec((1, D_MODEL), fix),
            pl.BlockSpec((nq, D_MODEL), fix),
            pl.BlockSpec((nq, D_MODEL), fix),
            pl.BlockSpec((D_MODEL, nq), fix),
            pl.BlockSpec((D_MODEL, nv), fix),
            pl.BlockSpec((D_MODEL, nv), fix),
            pl.BlockSpec((ng, D_MODEL), fix),
            pl.BlockSpec((ng, 1), fix),
        ],
        out_specs=[chunked(nq), chunked(nq), pl.BlockSpec((tm, nq), row),
                   pl.BlockSpec((tm, nv), row), pl.BlockSpec((tm, nv), row), chunked(_N_STATS * ng)],
        out_shape=[chunked_shape(nq, BF16), chunked_shape(nq, BF16),
                   jax.ShapeDtypeStruct((m, nq), BF16), jax.ShapeDtypeStruct((m, nv), BF16),
                   jax.ShapeDtypeStruct((m, nv), BF16), chunked_shape(_N_STATS * ng, F32)],
        compiler_params=_params(("parallel",), 40),
        name="mlstm_in",
    )(xa, xb, g, wqT, wkT, wk, wv, wo, wgT, bgT)


_N_STATS = 6


def _lane_scan(x, combine, identity, reverse):
    n = x.shape[-1]
    lane = lax.broadcasted_iota(jnp.int32, x.shape, 1)
    sh = 1
    while sh < n:
        if reverse:
            moved = jnp.where(lane < n - sh, pltpu.roll(x, n - sh, 1), identity)
        else:
            moved = jnp.where(lane >= sh, pltpu.roll(x, sh, 1), identity)
        x = combine(x, moved)
        sh *= 2
    return x


def _mlstm_gate_stats(gates):
    row = lax.broadcasted_iota(jnp.int32, gates.shape, 0)
    bwd_row = (row % 4) >= 2
    add = lambda x, y: x + y
    cum = jnp.where(bwd_row, _lane_scan(gates, add, 0.0, True), _lane_scan(gates, add, 0.0, False))
    b = pltpu.roll(cum, 7, 0)
    total = jnp.broadcast_to(jnp.sum(gates, axis=1, keepdims=True), gates.shape)
    g = pltpu.roll(total, 7, 0)
    r = gates - b
    pm = jnp.where(bwd_row, _lane_scan(r, jnp.maximum, -jnp.inf, True),
                   _lane_scan(r, jnp.maximum, -jnp.inf, False))
    a = g + r
    a_max = jnp.broadcast_to(jnp.max(a, axis=1, keepdims=True), gates.shape)
    return b, r, pm, g, a_max, jnp.exp(a - a_max)


def _mlstm_unit_pre(qT, kT, k, v2, stats, reverse):
    L = M_CHUNK
    s_idx = lax.broadcasted_iota(jnp.int32, (L, L), 0)
    t_idx = lax.broadcasted_iota(jnp.int32, (L, L), 1)
    keep = (s_idx >= t_idx) if reverse else (s_idx <= t_idx)
    head_a_lanes = lax.broadcasted_iota(jnp.int32, (L, LANES), 1) < M_QK
    head_a_rows = lax.broadcasted_iota(jnp.int32, (LANES, L), 0) < M_QK
    zeros_k = jnp.zeros_like(k)
    k2 = jnp.concatenate([jnp.where(head_a_lanes, k, zeros_k), jnp.where(head_a_lanes, zeros_k, k)],
                         axis=0)
    s2 = _dot(k2, qT)
    sT, den0 = [], []
    for hd in range(2):
        _, r, pm, _, _, _ = stats[hd]
        r_col = jnp.broadcast_to(r, (L, L)).T
        sT.append(s2[hd * L:(hd + 1) * L] * jnp.exp(jnp.where(keep, r_col - pm, -jnp.inf)))
        den0.append(jnp.sum(sT[hd], axis=0, keepdims=True))
    w = jnp.where(head_a_rows, stats[0][5], stats[1][5])
    ones = jnp.ones((L, M_V), BF16)
    kv = _dot((kT.astype(F32) * w).astype(BF16), jnp.concatenate([v2, ones], axis=1))
    col = lax.broadcasted_iota(jnp.int32, kv.shape, 1)
    row_a = lax.broadcasted_iota(jnp.int32, kv.shape, 0) < M_QK
    other_v = (row_a & (col >= M_V) & (col < 2 * M_V)) | (~row_a & (col < M_V))
    return sT, den0, jnp.where(other_v, 0.0, kv)


def _mlstm_unit_post(qT, v2, stats, pre, c_pair, m_rows):
    L = M_CHUNK
    sT, den0, kv = pre
    head_a_rows = lax.broadcasted_iota(jnp.int32, (LANES, L), 0) < M_QK
    row_a = lax.broadcasted_iota(jnp.int32, kv.shape, 0) < M_QK
    qT_f = qT.astype(F32)
    qn_all = qT_f * c_pair[:, 2 * M_V:]
    s_scale, q_scale, m_new, decay, scale = [], [], [], [], []
    for hd in range(2):
        b, _, pm, g, a_max, _ = stats[hd]
        m_row = m_rows[hd]
        u = jnp.maximum(pm, m_row)
        c1 = jnp.exp(pm - u)
        e = jnp.exp(m_row - u)
        qn = jnp.sum(qn_all[hd * M_QK:(hd + 1) * M_QK], axis=0, keepdims=True)
        den = c1 * den0[hd] + e * qn
        inv = 1.0 / jnp.maximum(jnp.abs(den), jnp.exp(-(b + u)))
        s_scale.append(c1 * inv)
        q_scale.append(e * inv)
        m_new.append(jnp.maximum(g + m_row, a_max))
        decay.append(jnp.exp(g + m_row - m_new[hd])[:, 0:1])
        scale.append(jnp.exp(a_max - m_new[hd])[:, 0:1])
    lhsT = jnp.concatenate([sT[0] * s_scale[0], sT[1] * s_scale[1],
                            qT_f * jnp.where(head_a_rows, q_scale[0], q_scale[1])],
                           axis=0).astype(BF16)
    zeros_v = jnp.zeros((L, M_V), BF16)
    rhs = jnp.concatenate([jnp.concatenate([v2[:, :M_V], zeros_v], axis=1),
                           jnp.concatenate([zeros_v, v2[:, M_V:]], axis=1),
                           c_pair[:, :2 * M_V].astype(BF16)], axis=0)
    h = lax.dot_general(lhsT, rhs, (((0,), (0,)), ((), ())), preferred_element_type=F32)
    c_new = jnp.where(row_a, decay[0], decay[1]) * c_pair + jnp.where(row_a, scale[0], scale[1]) * kv
    return h, c_new, m_new


def _mlstm_core_kernel(qTf_ref, kTf_ref, kf_ref, vf_ref, sf_ref, qTb_ref, kTb_ref, kb_ref, vb_ref,
                       sb_ref, hf_ref, hb_ref, c_ref, m_ref, *, nchunk, rb, m1, s1, s2):
    i = pl.program_id(1)
    nblk = pl.num_programs(1)
    pos_f, _ = _seq_pos(i * rb, m1, s1, s2)
    pos_b, len_b = _seq_pos((nblk - 1 - i) * rb, m1, s1, s2)

    def reset(dirn):
        c_ref[dirn] = jnp.zeros(c_ref.shape[1:], F32)
        for hd in range(2):
            m_ref[hd * 2 + dirn] = jnp.zeros(m_ref.shape[1:], F32)

    pl.when(pos_f == 0)(lambda: reset(0))
    pl.when(pos_b + rb == len_b)(lambda: reset(1))

    c_state = [c_ref[dirn] for dirn in range(2)]
    m_state = [[m_ref[hd * 2 + dirn][0:1] for hd in range(2)] for dirn in range(2)]
    refs = ((qTf_ref, kTf_ref, kf_ref, vf_ref, sf_ref, hf_ref),
            (qTb_ref, kTb_ref, kb_ref, vb_ref, sb_ref, hb_ref))

    def unit(step, dirn):
        qT_ref, kT_ref, k_ref, v_ref, st_ref, out_ref = refs[dirn]
        cc = nchunk - 1 - step if dirn else step
        rows = slice(cc * M_CHUNK, (cc + 1) * M_CHUNK)
        st = st_ref[cc]
        stats = [tuple(st[n * 8 + hd * 4 + dirn * 2:n * 8 + hd * 4 + dirn * 2 + 1]
                       for n in range(_N_STATS)) for hd in range(2)]
        return qT_ref[cc], kT_ref[cc], k_ref[rows, :], v_ref[rows, :], stats, out_ref, rows

    pre = {}
    for step in range(nchunk):
        for dirn in range(2):
            qT, kT, k, v2, stats, _, _ = unit(step, dirn)
            pre[step, dirn] = _mlstm_unit_pre(qT, kT, k, v2, stats, reverse=bool(dirn))
    for step in range(nchunk):
        for dirn in range(2):
            qT, _, _, v2, stats, out_ref, rows = unit(step, dirn)
            h, c_state[dirn], m_state[dirn] = _mlstm_unit_post(
                qT, v2, stats, pre[step, dirn], c_state[dirn], m_state[dirn])
            out_ref[rows, :] = h.astype(out_ref.dtype)

    for dirn in range(2):
        c_ref[dirn] = c_state[dirn]
        for hd in range(2):
            m_ref[hd * 2 + dirn] = jnp.broadcast_to(m_state[dirn][hd], m_ref.shape[1:])


def _mlstm_core(qT, kT, k, v, gT, rb, m1, s1, s2):
    m = k.shape[0]
    nblk = m // rb
    nchunk = rb // M_CHUNK
    npair = M_HEADS // 2
    fwd = lambda p, i: (i, p)
    bwd = lambda p, i: (nblk - 1 - i, p)
    fwd3 = lambda p, i: (i, p, 0)
    bwd3 = lambda p, i: (nblk - 1 - i, p, 0)

    def specs(im2, im3):
        return [
            pl.BlockSpec((nchunk, LANES, M_CHUNK), im3),
            pl.BlockSpec((nchunk, LANES, M_CHUNK), im3),
            pl.BlockSpec((rb, LANES), im2),
            pl.BlockSpec((rb, 2 * M_V), im2),
            pl.BlockSpec((nchunk, _N_STATS * 8, M_CHUNK), im3),
        ]

    kern = functools.partial(_mlstm_core_kernel, nchunk=nchunk, rb=rb, m1=m1, s1=s1, s2=s2)
    return pl.pallas_call(
        kern,
        grid=(npair, nblk),
        in_specs=specs(fwd, fwd3) + specs(bwd, bwd3),
        out_specs=[pl.BlockSpec((rb, 2 * M_V), fwd), pl.BlockSpec((rb, 2 * M_V), bwd)],
        out_shape=[jax.ShapeDtypeStruct((m, M_HEADS * M_V), BF16)] * 2,
        scratch_shapes=[pltpu.VMEM((2, LANES, 3 * M_V), F32), pltpu.VMEM((4, 8, M_CHUNK), F32)],
        compiler_params=_params(("parallel", "arbitrary"), 32),
        name="mlstm_core",
    )(qT, kT, k, v, gT, qT, kT, k, v, gT)


def _mlstm_out_kernel(hf_ref, hb_ref, o_ref, gain_ref, w_ref, xa_ref, xb_ref, out_ref, *, na):
    h = hf_ref[...].astype(F32) + hb_ref[...].astype(F32)
    parts = []
    for hd in range(M_HEADS):
        hh = h[:, hd * M_V:(hd + 1) * M_V]
        ms = jnp.mean(hh * hh, axis=-1, keepdims=True)
        parts.append(hh * lax.rsqrt(ms + EPS))
    hn = jnp.concatenate(parts, axis=1) * gain_ref[...]
    hg = (_sigmoid(o_ref[...].astype(F32)) * hn).astype(BF16)
    out_ref[...] = _pair_tile(xa_ref, xb_ref, na) + _dot(hg, w_ref[...])


def _mlstm_out(hf, hb, o, gain, w, x, tm):
    xa, xb, na = _as_pair(x, tm)
    m = hf.shape[0]
    row = lambda i: (i, 0)
    fix = lambda i: (0, 0)
    blk = pl.BlockSpec((tm, D_MODEL), row)
    return pl.pallas_call(
        functools.partial(_mlstm_out_kernel, na=na),
        grid=(m // tm,),
        in_specs=[blk, blk, blk, pl.BlockSpec((1, D_MODEL), fix),
                  pl.BlockSpec((D_MODEL, D_MODEL), fix)] + _pair_specs(tm, na),
        out_specs=blk,
        out_shape=jax.ShapeDtypeStruct((m, D_MODEL), F32),
        compiler_params=_params(("parallel",), 40),
        name="mlstm_out",
    )(hf, hb, o, gain, w, xa, xb)


def _ffn_kernel(x_ref, g_ref, wg_ref, wu_ref, wd_ref, out_ref, hn_ref, acc_ref):
    f = pl.program_id(1)

    @pl.when(f == 0)
    def _():
        hn_ref[...] = _rms(x_ref[...], g_ref[...]).astype(BF16)
        acc_ref[...] = jnp.zeros_like(acc_ref)

    hn = hn_ref[...]
    gate = _dot(hn, wg_ref[...])
    up = _dot(hn, wu_ref[...])
    act = (gate * _sigmoid(gate) * up).astype(BF16)
    acc_ref[...] += _dot(act, wd_ref[...])

    @pl.when(f == pl.num_programs(1) - 1)
    def _():
        out_ref[...] = x_ref[...] + acc_ref[...]


def _ffn(x, g, wg, wu, wd, tm, tf):
    m = x.shape[0]
    row = lambda i, f: (i, 0)
    return pl.pallas_call(
        _ffn_kernel,
        grid=(m // tm, D_FF // tf),
        in_specs=[
            pl.BlockSpec((tm, D_MODEL), row),
            pl.BlockSpec((1, D_MODEL), lambda i, f: (0, 0)),
            pl.BlockSpec((D_MODEL, tf), lambda i, f: (0, f)),
            pl.BlockSpec((D_MODEL, tf), lambda i, f: (0, f)),
            pl.BlockSpec((tf, D_MODEL), lambda i, f: (f, 0)),
        ],
        out_specs=pl.BlockSpec((tm, D_MODEL), row),
        out_shape=jax.ShapeDtypeStruct((m, D_MODEL), F32),
        scratch_shapes=[pltpu.VMEM((tm, D_MODEL), BF16), pltpu.VMEM((tm, D_MODEL), F32)],
        compiler_params=_params(("parallel", "arbitrary"), 48),
        name="ffn",
    )(x, g, wg, wu, wd)


def _rope_table_kernel(inv_ref, ma_ref, mb_ref, cos_ref, sa_ref, sb_ref):
    rows = cos_ref.shape[0]
    pos = pl.program_id(0) * rows + lax.broadcasted_iota(jnp.int32, (rows, LANES), 0)
    ang = pos.astype(F32) * inv_ref[...]
    sin = jnp.sin(ang)
    cos_ref[...] = jnp.cos(ang)
    sa_ref[...] = sin * ma_ref[...]
    sb_ref[...] = sin * mb_ref[...]


def _rope_tables(smax, rows):
    half = ROPE_DIM // 2
    d = jnp.arange(LANES) % A_HD
    inv = ROPE_THETA ** (-(jnp.arange(half, dtype=F32) * 2.0) / ROPE_DIM)
    inv_lane = jnp.where(d < ROPE_DIM, inv[d % half], 0.0).astype(F32)[None, :]
    ma = jnp.where(d < half, -1.0, 0.0).astype(F32)[None, :]
    mb = jnp.where((d >= half) & (d < ROPE_DIM), 1.0, 0.0).astype(F32)[None, :]
    fix = lambda i: (0, 0)
    vec = pl.BlockSpec((1, LANES), fix)
    tab = pl.BlockSpec((rows, LANES), lambda i: (i, 0))
    return pl.pallas_call(
        _rope_table_kernel,
        grid=(smax // rows,),
        in_specs=[vec, vec, vec],
        out_specs=[tab, tab, tab],
        out_shape=[jax.ShapeDtypeStruct((smax, LANES), F32)] * 3,
        compiler_params=_params(("parallel",), 16),
        name="rope_tables",
    )(inv_lane, ma, mb)


def _attn_in_kernel(x_ref, g_ref, wq_ref, wk_ref, wv_ref, qg_ref, kg_ref, gm_ref,
                    cos_ref, sa_ref, sb_ref, q_ref, k_ref, v_ref):
    hn = _rms(x_ref[...], g_ref[...]).astype(BF16)
    cos = cos_ref[...]
    sa = sa_ref[...]
    sb = sb_ref[...]
    half = ROPE_DIM // 2

    def tiles(x):
        return [x[:, j * LANES:(j + 1) * LANES] for j in range(x.shape[1] // LANES)]

    def norm_rope(xj, gain):
        ms = _dot((xj * xj).astype(BF16), gm_ref[...])
        y = xj * lax.rsqrt(ms + EPS) * gain
        return y * cos + pltpu.roll(y, LANES - half, 1) * sa + pltpu.roll(y, half, 1) * sb

    def store_lo_hi(pair_tiles, out_ref):
        low = lax.broadcasted_iota(jnp.int32, pair_tiles[0].shape, 1) < A_HD
        for j, y in enumerate(pair_tiles):
            for half_idx in range(2):
                mine = jnp.where(low if half_idx == 0 else jnp.logical_not(low), y, 0.0)
                other = pltpu.roll(mine, A_HD, 1)
                lo, hi = (mine, other) if half_idx == 0 else (other, mine)
                col = (2 * j + half_idx) * 2 * LANES
                out_ref[:, col:col + LANES] = lo.astype(BF16)
                out_ref[:, col + LANES:col + 2 * LANES] = hi.astype(BF16)

    for j, xj in enumerate(tiles(_dot(hn, wq_ref[...]))):
        q_ref[:, j * LANES:(j + 1) * LANES] = (norm_rope(xj, qg_ref[...]) * (A_HD ** -0.5)).astype(BF16)
    store_lo_hi([norm_rope(xj, kg_ref[...]) for xj in tiles(_dot(hn, wk_ref[...]))], k_ref)
    store_lo_hi(tiles(_dot(hn, wv_ref[...])), v_ref)


def _attn_in(x, g, wq, wk, wv, qg, kg, gm, tabs, tm, m1, s1, s2):
    m = x.shape[0]
    nq = A_Q_HEADS * A_HD
    nkv = A_KV_HEADS * A_HD
    row = lambda i: (i, 0)
    fix = lambda i: (0, 0)

    def tab_map(i):
        pos, _ = _seq_pos(i * tm, m1, s1, s2)
        return (pos // tm, 0)

    tab = pl.BlockSpec((tm, LANES), tab_map)
    vec = pl.BlockSpec((1, LANES), fix)
    return pl.pallas_call(
        _attn_in_kernel,
        grid=(m // tm,),
        in_specs=[
            pl.BlockSpec((tm, D_MODEL), row),
            pl.BlockSpec((1, D_MODEL), fix),
            pl.BlockSpec((D_MODEL, nq), fix),
            pl.BlockSpec((D_MODEL, nkv), fix),
            pl.BlockSpec((D_MODEL, nkv), fix),
            vec, vec,
            pl.BlockSpec((LANES, LANES), fix),
            tab, tab, tab,
        ],
        out_specs=[pl.BlockSpec((tm, nq), row), pl.BlockSpec((tm, _A_KV_COLS), row),
                   pl.BlockSpec((tm, _A_KV_COLS), row)],
        out_shape=[jax.ShapeDtypeStruct((m, nq), BF16), jax.ShapeDtypeStruct((m, _A_KV_COLS), BF16),
                   jax.ShapeDtypeStruct((m, _A_KV_COLS), BF16)],
        compiler_params=_params(("parallel",), 40),
        name="attn_in",
    )(x, g, wq, wk, wv, qg, kg, gm, *tabs)


def _attn_kernel(sink_ref, q_ref, kp_ref, kc_ref, kn_ref, vp_ref, vc_ref, vn_ref, out_ref,
                 *, m1, s1, s2):
    blk = A_BLOCK
    pos0, slen = _seq_pos(pl.program_id(0) * blk, m1, s1, s2)
    prev_ok = pos0 > 0
    next_ok = pos0 + blk < slen
    t = lax.broadcasted_iota(jnp.int32, (blk, 3 * blk), 0)
    c = lax.broadcasted_iota(jnp.int32, (blk, 3 * blk), 1)
    valid = (jnp.abs(c - blk - t) <= WINDOW) & ((c >= blk) | prev_ok) & ((c < 2 * blk) | next_ok)

    kcat = jnp.concatenate([kp_ref[...], kc_ref[...], kn_ref[...]], axis=0)
    vcat = jnp.concatenate([vp_ref[...], vc_ref[...], vn_ref[...]], axis=0)
    lane_q = lax.broadcasted_iota(jnp.int32, (blk, LANES), 1)

    def softmax_parts(s, sink):
        s = jnp.concatenate([jnp.where(valid[:, :blk], s[:, :blk], -jnp.inf), s[:, blk:2 * blk],
                             jnp.where(valid[:, 2 * blk:], s[:, 2 * blk:], -jnp.inf)], axis=1)
        m = jnp.maximum(jnp.max(s, axis=1, keepdims=True), sink)
        p = jnp.exp(s - m)
        denom = jnp.sum(p, axis=1, keepdims=True) + jnp.exp(sink - m)
        return p.astype(BF16), denom

    def lo_hi(cat, h):
        col = h * 2 * LANES
        return jnp.concatenate([cat[:, col:col + LANES], cat[:, col + LANES:col + 2 * LANES]], axis=0)

    npair = A_GROUP // 2
    kk = [lo_hi(kcat, h) for h in range(A_KV_HEADS)]
    vv = [lo_hi(vcat, h) for h in range(A_KV_HEADS)]
    def score(h):
        q_pairs = jnp.concatenate([q_ref[:, (h * npair + j) * LANES:(h * npair + j + 1) * LANES]
                                   for j in range(npair)], axis=0)
        return lax.dot_general(q_pairs, kk[h], _NT, preferred_element_type=F32)

    scores = {0: score(0)}
    for h in range(A_KV_HEADS):
        if h + 1 < A_KV_HEADS:
            scores[h + 1] = score(h + 1)
        p_rows, inv_rows = [], []
        for j in range(npair):
            parts = [softmax_parts(scores[h][j * blk:(j + 1) * blk, par * 3 * blk:(par + 1) * 3 * blk],
                                   sink_ref[(h * npair + j) * 2 + par]) for par in range(2)]
            p_rows.append(jnp.concatenate([parts[0][0], parts[1][0]], axis=1))
            inv_rows.append(jnp.where(lane_q < A_HD, 1.0 / parts[0][1], 1.0 / parts[1][1]))
        o = _dot(jnp.concatenate(p_rows, axis=0), vv[h])
        for j in range(npair):
            grp = h * npair + j
            out_ref[:, grp * LANES:(grp + 1) * LANES] = (o[j * blk:(j + 1) * blk] * inv_rows[j]).astype(BF16)


def _attn(sink, q, k, v, m1, s1, s2):
    m = q.shape[0]
    blk = A_BLOCK
    nblk = m // blk
    cur = lambda i: (i, 0)
    prev = lambda i: (jnp.maximum(i - 1, 0), 0)
    nxt = lambda i: (jnp.minimum(i + 1, nblk - 1), 0)
    kv = lambda im: pl.BlockSpec((blk, _A_KV_COLS), im)
    kern = functools.partial(_attn_kernel, m1=m1, s1=s1, s2=s2)
    return pl.pallas_call(
        kern,
        grid=(nblk,),
        in_specs=[pl.BlockSpec(memory_space=pltpu.SMEM),
                  pl.BlockSpec((blk, D_MODEL), cur), kv(prev), kv(cur), kv(nxt),
                  kv(prev), kv(cur), kv(nxt)],
        out_specs=pl.BlockSpec((blk, D_MODEL), cur),
        out_shape=jax.ShapeDtypeStruct((m, D_MODEL), BF16),
        compiler_params=_params(("parallel",), 32),
        name="attn",
    )(sink, q, k, k, k, v, v, v)


_R_E1, _R_E2, _R_P1, _R_P2, _R_RANK1, _R_RANK2 = range(6)


def _router_kernel(a_ref, w_ref, x_ref, g_ref, wr_ref, x2_ref, route_ref, route_t_ref, count_ref,
                   carry_ref):
    @pl.when(pl.program_id(0) == 0)
    def _():
        carry_ref[...] = jnp.zeros_like(carry_ref)

    x2 = x_ref[...] + _dot(a_ref[...], w_ref[...])
    x2_ref[...] = x2
    hn = _rms(x2, g_ref[...]).astype(BF16)
    logits = _dot(hn, wr_ref[...])
    tm = logits.shape[0]
    lane = lax.broadcasted_iota(jnp.int32, logits.shape, 1).astype(F32)
    logits = jnp.where(lane < N_EXPERTS, logits, -jnp.inf)
    v1 = jnp.max(logits, axis=1, keepdims=True)
    i1 = jnp.min(jnp.where(logits == v1, lane, float(LANES)), axis=1, keepdims=True)
    rest = jnp.where(lane == i1, -jnp.inf, logits)
    v2 = jnp.max(rest, axis=1, keepdims=True)
    i2 = jnp.min(jnp.where(rest == v2, lane, float(LANES)), axis=1, keepdims=True)
    e2 = jnp.exp(v2 - v1)
    p1 = 1.0 / (1.0 + e2)
    p2 = e2 * p1

    sel = jnp.where((lane == i1) | (lane == i2), 1.0, 0.0)
    t_row = lax.broadcasted_iota(jnp.int32, (tm, tm), 0)
    t_col = lax.broadcasted_iota(jnp.int32, (tm, tm), 1)
    earlier = jnp.where(t_col < t_row, 1.0, 0.0).astype(BF16)
    before = _dot(earlier, sel.astype(BF16)) + carry_ref[0:1, :]
    rank1 = jnp.sum(jnp.where(lane == i1, before, 0.0), axis=1, keepdims=True)
    rank2 = jnp.sum(jnp.where(lane == i2, before, 0.0), axis=1, keepdims=True)
    total = carry_ref[0:1, :] + jnp.sum(sel, axis=0, keepdims=True)
    carry_ref[...] = jnp.broadcast_to(total, carry_ref.shape)
    count_ref[...] = jnp.broadcast_to(total, count_ref.shape)

    route = jnp.zeros_like(logits)
    for col, val in ((_R_E1, i1), (_R_E2, i2), (_R_P1, p1), (_R_P2, p2),
                     (_R_RANK1, rank1), (_R_RANK2, rank2)):
        route = jnp.where(lane == float(col), val, route)
    route_ref[...] = route
    route_t_ref[...] = route.T[:8]


def _proj_route(a, w, x, g, wr, tm):
    m = x.shape[0]
    row = lambda i: (i, 0)
    fix = lambda i: (0, 0)
    blk = pl.BlockSpec((tm, D_MODEL), row)
    return pl.pallas_call(
        _router_kernel,
        grid=(m // tm,),
        in_specs=[blk, pl.BlockSpec((D_MODEL, D_MODEL), fix), blk, pl.BlockSpec((1, D_MODEL), fix),
                  pl.BlockSpec((D_MODEL, LANES), fix)],
        out_specs=[blk, pl.BlockSpec((tm, LANES), row), pl.BlockSpec((8, tm), lambda i: (0, i)),
                   pl.BlockSpec((8, LANES), fix)],
        out_shape=[jax.ShapeDtypeStruct((m, D_MODEL), F32), jax.ShapeDtypeStruct((m, LANES), F32),
                   jax.ShapeDtypeStruct((8, m), F32), jax.ShapeDtypeStruct((8, LANES), F32)],
        scratch_shapes=[pltpu.VMEM((8, LANES), F32)],
        compiler_params=_params(("arbitrary",), 40),
        name="attn_out_router",
    )(a, w, x, g, wr)


def _invert_kernel(pos_ref, tab_ref):
    i = pl.program_id(0)
    ta = pos_ref.shape[-1]

    @pl.when(i == 0)
    def _():
        tab_ref[...] = jnp.full(tab_ref.shape, -1, jnp.int32)

    lane = lax.broadcasted_iota(jnp.int32, (1, LANES), 1)
    unroll = 8

    def body(jj, carry):
        for u in range(unroll):
            j = jj * unroll + u
            p = pos_ref[0, 0, j]
            row = lax.shift_right_logical(p, 7)
            pltpu.store(tab_ref.at[pl.ds(row, 1), :], jnp.broadcast_to(i * ta + j, (1, LANES)),
                        mask=lane == (p & (LANES - 1)))
        return carry

    lax.fori_loop(0, ta // unroll, body, 0)


def _invert(pos, n_rows, ta):
    n = pos.shape[0]
    assert n % ta == 0 and n_rows % LANES == 0
    return pl.pallas_call(
        _invert_kernel,
        grid=(n // ta,),
        in_specs=[pl.BlockSpec((1, 1, ta), lambda i: (i, 0, 0), memory_space=pltpu.SMEM)],
        out_specs=pl.BlockSpec((n_rows // LANES, LANES), lambda i: (0, 0)),
        out_shape=jax.ShapeDtypeStruct((n_rows // LANES, LANES), jnp.int32),
        compiler_params=_params(("arbitrary",), 16),
        name="moe_invert",
    )(pos.reshape(n // ta, 1, ta)).reshape(n_rows)


def _experts_kernel(te_ref, nused_ref, src0_ref, src_next_ref, dst_prev_ref, x_hbm, g_ref,
                    wg_hbm, wu_hbm, wd_hbm, y_hbm, xbuf, acc_ref, hn_ref, wg_buf, wu_buf, wd_buf,
                    gsem, ssem, zsem, wsem, *, n_real, n_trash):
    t = pl.program_id(0)
    nused = nused_ref[0]
    nf, _, tf = wg_buf.shape
    tm = hn_ref.shape[0]
    chunk = tm // nf
    slot = t % 2
    other = 1 - slot

    def gather_row(idx_ref, j, s):
        return pltpu.make_async_copy(x_hbm.at[pl.ds(idx_ref[0, 0, j], 1)],
                                     xbuf.at[s, pl.ds(j, 1)], gsem.at[s])

    def scatter_row(j, s):
        return pltpu.make_async_copy(acc_ref.at[s, pl.ds(j, 1)],
                                     y_hbm.at[pl.ds(dst_prev_ref[0, 0, j], 1)], ssem.at[s])

    def issue_gather(row0):
        for j in range(chunk):
            gather_row(src_next_ref, row0 + j, other).start(priority=j % 2)

    def issue_scatter(row0):
        for j in range(chunk):
            scatter_row(row0 + j, other).start(priority=j % 2)

    def compute(f):
        hn = hn_ref[...]
        gate = _dot(hn, wg_buf[f])
        up = _dot(hn, wu_buf[f])
        act = (gate * _sigmoid(gate) * up).astype(BF16)
        acc_ref[slot] += _dot(act, wd_buf[f])

    def chunks(body):
        def step(f, carry):
            body(f, f * chunk)
            return carry
        lax.fori_loop(0, nf, step, 0)

    @pl.when(t == 0)
    def _():
        acc_ref[1] = jnp.zeros(acc_ref.shape[1:], F32)
        fills = [pltpu.make_async_copy(acc_ref.at[1], y_hbm.at[pl.ds(n_real + k * tm, tm)], zsem)
                 for k in range(n_trash // tm)]
        for c in fills:
            c.start()
        for c in fills:
            c.wait()

        def body(jj, carry):
            for u in range(8):
                gather_row(src0_ref, jj * 8 + u, 0).start()
            return carry

        lax.fori_loop(0, tm // 8, body, 0)

    expert = te_ref[t]

    @pl.when((t < nused) & ((t == 0) | (expert != te_ref[jnp.maximum(t - 1, 0)])))
    def _():
        loads = []
        for f in range(nf):
            cols = pl.ds(f * tf, tf)
            loads += [pltpu.make_async_copy(wg_hbm.at[expert, :, cols], wg_buf.at[f], wsem),
                      pltpu.make_async_copy(wu_hbm.at[expert, :, cols], wu_buf.at[f], wsem),
                      pltpu.make_async_copy(wd_hbm.at[expert, cols, :], wd_buf.at[f], wsem)]
        for c in loads:
            c.start()
        for c in loads:
            c.wait()

    @pl.when(t <= nused)
    def _():
        pltpu.make_async_copy(x_hbm.at[pl.ds(0, tm)], xbuf.at[slot], gsem.at[slot]).wait()

    @pl.when((t >= 2) & (t - 2 < nused))
    def _():
        pltpu.make_async_copy(acc_ref.at[slot], y_hbm.at[pl.ds(0, tm)], ssem.at[slot]).wait()

    @pl.when(t < nused)
    def _():
        hn_ref[...] = _rms(xbuf[slot], g_ref[...]).astype(BF16)
        acc_ref[slot] = jnp.zeros(acc_ref.shape[1:], F32)

    @pl.when(t == 0)
    def _():
        def body(f, row0):
            issue_gather(row0)
            compute(f)
        chunks(body)

    @pl.when((t >= 1) & (t < nused))
    def _():
        def body(f, row0):
            issue_gather(row0)
            issue_scatter(row0)
            compute(f)
        chunks(body)

    @pl.when(t == nused)
    def _():
        chunks(lambda f, row0: issue_scatter(row0))


def _experts(tile_expert, nused, src, dst, x, g, wg, wu, wd, tm, tf):
    m = x.shape[0]
    ntile = src.shape[0]
    n_trash = N_EXPERTS * tm
    nf = D_FF // tf
    assert tm % nf == 0 and D_FF % tf == 0
    idx = lambda im: pl.BlockSpec((1, 1, tm), im, memory_space=pltpu.SMEM)
    any_spec = pl.BlockSpec(memory_space=pl.ANY)
    kern = functools.partial(_experts_kernel, n_real=2 * m, n_trash=n_trash)
    return pl.pallas_call(
        kern,
        grid_spec=pltpu.PrefetchScalarGridSpec(
            num_scalar_prefetch=2,
            grid=(ntile,),
            in_specs=[
                idx(lambda t, te, nu: (0, 0, 0)),
                idx(lambda t, te, nu: (jnp.minimum(t + 1, ntile - 1), 0, 0)),
                idx(lambda t, te, nu: (jnp.maximum(t - 1, 0), 0, 0)),
                any_spec,
                pl.BlockSpec((1, D_MODEL), lambda t, te, nu: (0, 0)),
                any_spec, any_spec, any_spec,
            ],
            out_specs=any_spec,
            scratch_shapes=[pltpu.VMEM((2, tm, D_MODEL), F32), pltpu.VMEM((2, tm, D_MODEL), F32),
                            pltpu.VMEM((tm, D_MODEL), BF16),
                            pltpu.VMEM((nf, D_MODEL, tf), BF16), pltpu.VMEM((nf, D_MODEL, tf), BF16),
                            pltpu.VMEM((nf, tf, D_MODEL), BF16),
                            pltpu.SemaphoreType.DMA((2,)), pltpu.SemaphoreType.DMA((2,)),
                            pltpu.SemaphoreType.DMA(()), pltpu.SemaphoreType.DMA(())],
        ),
        out_shape=jax.ShapeDtypeStruct((2 * m + n_trash, D_MODEL), F32),
        compiler_params=_params(("arbitrary",), 56),
        name="moe_experts",
    )(tile_expert, nused, src, src, dst, x, g, wg, wu, wd)


def _combine_kernel(x_ref, route_ref, y1_ref, y2_ref, outa_ref, outb_ref, *, na):
    route = route_ref[...]
    p1 = route[:, _R_P1:_R_P1 + 1]
    p2 = route[:, _R_P2:_R_P2 + 1]
    out = x_ref[...] + (p1 * y1_ref[...] + p2 * y2_ref[...])
    first = pl.program_id(0) < na

    @pl.when(first)
    def _():
        outa_ref[...] = out

    @pl.when(jnp.logical_not(first))
    def _():
        outb_ref[...] = out


def _combine(x, route, y, tc, m1):
    m = x.shape[0]
    na = m1 // tc
    row = lambda i: (i, 0)
    blk = pl.BlockSpec((tc, D_MODEL), row)
    return pl.pallas_call(
        functools.partial(_combine_kernel, na=na),
        grid=(m // tc,),
        in_specs=[blk, pl.BlockSpec((tc, LANES), row), blk,
                  pl.BlockSpec((tc, D_MODEL), lambda i: (i + m // tc, 0))],
        out_specs=_pair_specs(tc, na),
        out_shape=[jax.ShapeDtypeStruct((m1, D_MODEL), F32),
                   jax.ShapeDtypeStruct((m - m1, D_MODEL), F32)],
        compiler_params=_params(("arbitrary",), 32),
        name="moe_combine",
    )(x, route, y, y)


def _moe(att, w_out, x, g, wr, wg, wu, wd, tm_route, tm, tf, m1):
    m = x.shape[0]
    x, route, route_t, counts = _proj_route(att, w_out, x, g, wr, tm_route)
    e1 = route_t[_R_E1].astype(jnp.int32)
    e2 = route_t[_R_E2].astype(jnp.int32)
    n_e = counts[0, :N_EXPERTS].astype(jnp.int32)
    tiles_e = (n_e + tm - 1) // tm
    tile_end = jnp.cumsum(tiles_e)
    row_start = (tile_end - tiles_e) * tm
    pos = jnp.concatenate([row_start[e1] + route_t[_R_RANK1].astype(jnp.int32),
                           row_start[e2] + route_t[_R_RANK2].astype(jnp.int32)])
    ntile = pl.cdiv(2 * m, tm) + N_EXPERTS + 2
    nused = tile_end[-1:]
    tile_ids = jnp.minimum(jnp.arange(ntile, dtype=jnp.int32), nused - 1)
    tile_expert = jnp.sum(tile_ids[:, None] >= tile_end[None, :], axis=1).astype(jnp.int32)
    rows = jnp.arange(ntile * tm, dtype=jnp.int32)
    spare = 2 * m + rows - jnp.cumsum(n_e)[jnp.repeat(tile_expert, tm)]
    spare = jnp.clip(spare, 2 * m, 2 * m + N_EXPERTS * tm - 1)
    assign = _invert(pos, ntile * tm, tm_route)
    src = jnp.where(assign >= m, assign - m, jnp.maximum(assign, 0))
    dst = jnp.where(assign >= 0, assign, spare)
    y = _experts(tile_expert, nused.astype(jnp.int32), src.reshape(ntile, 1, tm),
                 dst.reshape(ntile, 1, tm), x, g, wg, wu, wd, tm, tf)
    return tuple(_combine(x, route, y, tm_route, m1))


def _mlstm_gate_perm():
    perm = []
    for pair in range(M_HEADS // 2):
        for hh in range(2):
            for gtype in range(4):
                perm.append(gtype * M_HEADS + 2 * pair + hh)
    return jnp.array(perm, jnp.int32)


def kernel(x_prompt, x_sample, norm_mix, norm_ffn, mlstm_w_in, mlstm_gate_bias, mlstm_head_gain,
           mlstm_w_out, attn_w_in, attn_q_gain, attn_k_gain, attn_sink, attn_w_out, ffn_w_gate,
           ffn_w_up, ffn_w_down, moe_w_router, moe_w_gate, moe_w_up, moe_w_down):
    b1, s1, _ = x_prompt.shape
    b2, s2, _ = x_sample.shape
    m1 = b1 * s1
    x = (x_prompt.reshape(m1, D_MODEL), x_sample.reshape(b2 * s2, D_MODEL))
    m = m1 + b2 * s2
    seq_gcd = math.gcd(s1, s2)
    tm = _tile(seq_gcd, 512)
    rb = _tile(seq_gcd, 1024)
    tm_ffn = _tile(m, 1024)
    tf = 512
    tm_moe = LANES * (D_FF // tf)
    depth = norm_mix.shape[0]
    vec = lambda a: a.astype(F32).reshape(1, -1)
    nq = M_HEADS * M_QK
    nv = M_HEADS * M_V

    tabs = _rope_tables(max(s1, s2), tm)
    group_mean = jnp.where((jnp.arange(LANES)[:, None] // A_HD) == (jnp.arange(LANES)[None, :] // A_HD),
                           1.0 / A_HD, 0.0).astype(BF16)
    gate_perm = _mlstm_gate_perm()

    for i in range(depth):
        j = i // 2
        if i % 2 == 0:
            w_in = mlstm_w_in[j]
            wqT = w_in[:, :nq].T.astype(BF16)
            wk = w_in[:, nq:2 * nq].astype(BF16)
            wv = w_in[:, 2 * nq:2 * nq + nv].astype(BF16)
            wo = w_in[:, 2 * nq + nv:2 * nq + 2 * nv].astype(BF16)
            wgT = w_in[:, 2 * nq + 2 * nv:][:, gate_perm].T.astype(BF16)
            bgT = mlstm_gate_bias[j].astype(F32)[gate_perm].reshape(-1, 1)
            qT, kT, k, v, o, gT = _mlstm_in(x, vec(norm_mix[i]), wqT, wk.T, wk, wv, wo, wgT, bgT, tm)
            hf, hb = _mlstm_core(qT, kT, k, v, gT, rb, m1, s1, s2)
            x = _mlstm_out(hf, hb, o, vec(mlstm_head_gain[j]), mlstm_w_out[j].astype(BF16), x, tm)
            x = _ffn(x, vec(norm_ffn[i]), ffn_w_gate[j].astype(BF16), ffn_w_up[j].astype(BF16),
                     ffn_w_down[j].astype(BF16), tm_ffn, tf)
        else:
            w_in = attn_w_in[j]
            a_q = A_Q_HEADS * A_HD
            a_kv = A_KV_HEADS * A_HD
            qg = jnp.tile(attn_q_gain[j].astype(F32), LANES // A_HD).reshape(1, LANES)
            kg = jnp.tile(attn_k_gain[j].astype(F32), LANES // A_HD).reshape(1, LANES)
            q, k, v = _attn_in(x, vec(norm_mix[i]), w_in[:, :a_q].astype(BF16),
                               w_in[:, a_q:a_q + a_kv].astype(BF16), w_in[:, a_q + a_kv:].astype(BF16),
                               qg, kg, group_mean, tabs, tm, m1, s1, s2)
            att = _attn(attn_sink[j].astype(F32), q, k, v, m1, s1, s2)
            wr = jnp.pad(moe_w_router[j], ((0, 0), (0, LANES - N_EXPERTS))).astype(BF16)
            x = _moe(att, attn_w_out[j].astype(BF16), x, vec(norm_ffn[i]), wr,
                     moe_w_gate[j].astype(BF16), moe_w_up[j].astype(BF16),
                     moe_w_down[j].astype(BF16), tm, tm_moe, tf, m1)

    xa, xb = x if isinstance(x, tuple) else (x[:m1], x[m1:])
    return xa.reshape(b1, s1, D_MODEL), xb.reshape(b2, s2, D_MODEL)
```

```python
import functools
import math

import jax
import jax.numpy as jnp
from jax import lax
from jax.experimental import pallas as pl
from jax.experimental.pallas import tpu as pltpu

F32 = jnp.float32
BF16 = jnp.bfloat16

D_MODEL = 1024
EPS = 1e-6
M_HEADS = 8
M_QK = 64
M_V = 128
M_CHUNK = 128
A_Q_HEADS = 16
A_KV_HEADS = 4
A_HD = 64
A_GROUP = A_Q_HEADS // A_KV_HEADS
A_BLOCK = 128
WINDOW = 128
ROPE_THETA = 500000.0
ROPE_DIM = 16
D_FF = 3584
N_EXPERTS = 8

LANES = 128
_A_KV_COLS = A_KV_HEADS * 2 * LANES
VMEM_BYTES_V7X = 64 * 1024 * 1024

_NT = (((1,), (1,)), ((), ()))


def _params(semantics, vmem_mb):
    assert vmem_mb * 1024 * 1024 < VMEM_BYTES_V7X
    return pltpu.CompilerParams(dimension_semantics=semantics,
                                vmem_limit_bytes=vmem_mb * 1024 * 1024)


def _tile(total, pref):
    t = min(pref, total)
    t -= t % LANES
    while total % t:
        t -= LANES
    return t


def _rms(x, g):
    ms = jnp.mean(x * x, axis=-1, keepdims=True)
    return x * lax.rsqrt(ms + EPS) * g


def _sigmoid(x):
    return 1.0 / (1.0 + jnp.exp(-x))


def _dot(a, b):
    return jnp.dot(a, b, preferred_element_type=F32)


def _seq_pos(row0, m1, s1, s2):
    first = row0 < m1
    pos = jnp.where(first, row0 % s1, (row0 - m1) % s2)
    return pos, jnp.where(first, s1, s2)


def _pair_specs(tm, na):
    return [pl.BlockSpec((tm, D_MODEL), lambda i: (jnp.minimum(i, na - 1), 0)),
            pl.BlockSpec((tm, D_MODEL), lambda i: (jnp.maximum(i - na, 0), 0))]


def _pair_tile(xa_ref, xb_ref, na):
    return jnp.where(pl.program_id(0) < na, xa_ref[...], xb_ref[...])


def _as_pair(x, tm):
    if isinstance(x, tuple):
        return x[0], x[1], x[0].shape[0] // tm
    return x, x, x.shape[0] // tm


def _mlstm_in_kernel(xa_ref, xb_ref, g_ref, wqT_ref, wkT_ref, wk_ref, wv_ref, wo_ref, wgT_ref, bgT_ref,
                     qT_ref, kT_ref, k_ref, v_ref, o_ref, st_ref, *, na):
    hn = _rms(_pair_tile(xa_ref, xb_ref, na), g_ref[...]).astype(BF16)
    gT = lax.dot_general(wgT_ref[...], hn, _NT, preferred_element_type=F32) + bgT_ref[...]
    log_sig = jnp.minimum(gT, 0.0) - jnp.log1p(jnp.exp(-jnp.abs(gT)))
    row = lax.broadcasted_iota(jnp.int32, gT.shape, 0)
    gT = jnp.where(row % 2 == 1, log_sig, gT)
    nchunk = kT_ref.shape[0]
    for j in range(nchunk):
        cols = slice(j * M_CHUNK, (j + 1) * M_CHUNK)
        for pair in range(M_HEADS // 2):
            stats = _mlstm_gate_stats(gT[pair * 8:(pair + 1) * 8, cols])
            st_ref[j, pair * _N_STATS * 8:(pair + 1) * _N_STATS * 8, :] = jnp.concatenate(stats, axis=0)
    k_ref[...] = _dot(hn, wk_ref[...]).astype(BF16)
    v_ref[...] = _dot(hn, wv_ref[...]).astype(BF16)
    o_ref[...] = _dot(hn, wo_ref[...]).astype(BF16)
    qT = lax.dot_general(wqT_ref[...], hn, _NT, preferred_element_type=F32) * (M_QK ** -0.5)
    qT = qT.astype(BF16)
    kT = lax.dot_general(wkT_ref[...], hn, _NT, preferred_element_type=F32).astype(BF16)
    for j in range(nchunk):
        cols = slice(j * M_CHUNK, (j + 1) * M_CHUNK)
        qT_ref[j] = qT[:, cols]
        kT_ref[j] = kT[:, cols]


def _mlstm_in(x, g, wqT, wkT, wk, wv, wo, wgT, bgT, tm):
    xa, xb, na = _as_pair(x, tm)
    m = xa.shape[0] + xb.shape[0] if isinstance(x, tuple) else xa.shape[0]
    nq = M_HEADS * M_QK
    nv = M_HEADS * M_V
    ng = 4 * M_HEADS
    row = lambda i: (i, 0)
    fix = lambda i: (0, 0)
    chunked = lambda n: pl.BlockSpec((tm // M_CHUNK, n, M_CHUNK), lambda i: (i, 0, 0))
    chunked_shape = lambda n, dt: jax.ShapeDtypeStruct((m // M_CHUNK, n, M_CHUNK), dt)
    return pl.pallas_call(
        functools.partial(_mlstm_in_kernel, na=na),
        grid=(m // tm,),
        in_specs=_pair_specs(tm, na) + [
            pl.BlockSpec((1, D_MODEL), fix),
            pl.BlockSpec((nq, D_MODEL), fix),
            pl.BlockSpec((nq, D_MODEL), fix),
            pl.BlockSpec((D_MODEL, nq), fix),
            pl.BlockSpec((D_MODEL, nv), fix),
            pl.BlockSpec((D_MODEL, nv), fix),
            pl.BlockSpec((ng, D_MODEL), fix),
            pl.BlockSpec((ng, 1), fix),
        ],
        out_specs=[chunked(nq), chunked(nq), pl.BlockSpec((tm, nq), row),
                   pl.BlockSpec((tm, nv), row), pl.BlockSpec((tm, nv), row), chunked(_N_STATS * ng)],
        out_shape=[chunked_shape(nq, BF16), chunked_shape(nq, BF16),
                   jax.ShapeDtypeStruct((m, nq), BF16), jax.ShapeDtypeStruct((m, nv), BF16),
                   jax.ShapeDtypeStruct((m, nv), BF16), chunked_shape(_N_STATS * ng, F32)],
        compiler_params=_params(("parallel",), 40),
        name="mlstm_in",
    )(xa, xb, g, wqT, wkT, wk, wv, wo, wgT, bgT)


_N_STATS = 6


def _lane_scan(x, combine, identity, reverse):
    n = x.shape[-1]
    lane = lax.broadcasted_iota(jnp.int32, x.shape, 1)
    sh = 1
    while sh < n:
        if reverse:
            moved = jnp.where(lane < n - sh, pltpu.roll(x, n - sh, 1), identity)
        else:
            moved = jnp.where(lane >= sh, pltpu.roll(x, sh, 1), identity)
        x = combine(x, moved)
        sh *= 2
    return x


def _mlstm_gate_stats(gates):
    row = lax.broadcasted_iota(jnp.int32, gates.shape, 0)
    bwd_row = (row % 4) >= 2
    add = lambda x, y: x + y
    cum = jnp.where(bwd_row, _lane_scan(gates, add, 0.0, True), _lane_scan(gates, add, 0.0, False))
    b = pltpu.roll(cum, 7, 0)
    total = jnp.broadcast_to(jnp.sum(gates, axis=1, keepdims=True), gates.shape)
    g = pltpu.roll(total, 7, 0)
    r = gates - b
    pm = jnp.where(bwd_row, _lane_scan(r, jnp.maximum, -jnp.inf, True),
                   _lane_scan(r, jnp.maximum, -jnp.inf, False))
    a = g + r
    a_max = jnp.broadcast_to(jnp.max(a, axis=1, keepdims=True), gates.shape)
    return b, r, pm, g, a_max, jnp.exp(a - a_max)


def _mlstm_unit_pre(qT, kT, k, v2, stats, reverse):
    L = M_CHUNK
    s_idx = lax.broadcasted_iota(jnp.int32, (L, L), 0)
    t_idx = lax.broadcasted_iota(jnp.int32, (L, L), 1)
    keep = (s_idx >= t_idx) if reverse else (s_idx <= t_idx)
    head_a_lanes = lax.broadcasted_iota(jnp.int32, (L, LANES), 1) < M_QK
    head_a_rows = lax.broadcasted_iota(jnp.int32, (LANES, L), 0) < M_QK
    zeros_k = jnp.zeros_like(k)
    k2 = jnp.concatenate([jnp.where(head_a_lanes, k, zeros_k), jnp.where(head_a_lanes, zeros_k, k)],
                         axis=0)
    s2 = _dot(k2, qT)
    sT, den0 = [], []
    for hd in range(2):
        _, r, pm, _, _, _ = stats[hd]
        r_col = jnp.broadcast_to(r, (L, L)).T
        sT.append(s2[hd * L:(hd + 1) * L] * jnp.exp(jnp.where(keep, r_col - pm, -jnp.inf)))
        den0.append(jnp.sum(sT[hd], axis=0, keepdims=True))
    w = jnp.where(head_a_rows, stats[0][5], stats[1][5])
    ones = jnp.ones((L, M_V), BF16)
    kv = _dot((kT.astype(F32) * w).astype(BF16), jnp.concatenate([v2, ones], axis=1))
    col = lax.broadcasted_iota(jnp.int32, kv.shape, 1)
    row_a = lax.broadcasted_iota(jnp.int32, kv.shape, 0) < M_QK
    other_v = (row_a & (col >= M_V) & (col < 2 * M_V)) | (~row_a & (col < M_V))
    return sT, den0, jnp.where(other_v, 0.0, kv)


def _mlstm_unit_post(qT, v2, stats, pre, c_pair, m_rows):
    L = M_CHUNK
    sT, den0, kv = pre
    head_a_rows = lax.broadcasted_iota(jnp.int32, (LANES, L), 0) < M_QK
    row_a = lax.broadcasted_iota(jnp.int32, kv.shape, 0) < M_QK
    qT_f = qT.astype(F32)
    qn_all = qT_f * c_pair[:, 2 * M_V:]
    s_scale, q_scale, m_new, decay, scale = [], [], [], [], []
    for hd in range(2):
        b, _, pm, g, a_max, _ = stats[hd]
        m_row = m_rows[hd]
        u = jnp.maximum(pm, m_row)
        c1 = jnp.exp(pm - u)
        e = jnp.exp(m_row - u)
        qn = jnp.sum(qn_all[hd * M_QK:(hd + 1) * M_QK], axis=0, keepdims=True)
        den = c1 * den0[hd] + e * qn
        inv = 1.0 / jnp.maximum(jnp.abs(den), jnp.exp(-(b + u)))
        s_scale.append(c1 * inv)
        q_scale.append(e * inv)
        m_new.append(jnp.maximum(g + m_row, a_max))
        decay.append(jnp.exp(g + m_row - m_new[hd])[:, 0:1])
        scale.append(jnp.exp(a_max - m_new[hd])[:, 0:1])
    lhsT = jnp.concatenate([sT[0] * s_scale[0], sT[1] * s_scale[1],
                            qT_f * jnp.where(head_a_rows, q_scale[0], q_scale[1])],
                           axis=0).astype(BF16)
    zeros_v = jnp.zeros((L, M_V), BF16)
    rhs = jnp.concatenate([jnp.concatenate([v2[:, :M_V], zeros_v], axis=1),
                           jnp.concatenate([zeros_v, v2[:, M_V:]], axis=1),
                           c_pair[:, :2 * M_V].astype(BF16)], axis=0)
    h = lax.dot_general(lhsT, rhs, (((0,), (0,)), ((), ())), preferred_element_type=F32)
    c_new = jnp.where(row_a, decay[0], decay[1]) * c_pair + jnp.where(row_a, scale[0], scale[1]) * kv
    return h, c_new, m_new


def _mlstm_core_kernel(qTf_ref, kTf_ref, kf_ref, vf_ref, sf_ref, qTb_ref, kTb_ref, kb_ref, vb_ref,
                       sb_ref, hf_ref, hb_ref, c_ref, m_ref, *, nchunk, rb, m1, s1, s2):
    i = pl.program_id(1)
    nblk = pl.num_programs(1)
    pos_f, _ = _seq_pos(i * rb, m1, s1, s2)
    pos_b, len_b = _seq_pos((nblk - 1 - i) * rb, m1, s1, s2)

    def reset(dirn):
        c_ref[dirn] = jnp.zeros(c_ref.shape[1:], F32)
        for hd in range(2):
            m_ref[hd * 2 + dirn] = jnp.zeros(m_ref.shape[1:], F32)

    pl.when(pos_f == 0)(lambda: reset(0))
    pl.when(pos_b + rb == len_b)(lambda: reset(1))

    c_state = [c_ref[dirn] for dirn in range(2)]
    m_state = [[m_ref[hd * 2 + dirn][0:1] for hd in range(2)] for dirn in range(2)]
    refs = ((qTf_ref, kTf_ref, kf_ref, vf_ref, sf_ref, hf_ref),
            (qTb_ref, kTb_ref, kb_ref, vb_ref, sb_ref, hb_ref))

    def unit(step, dirn):
        qT_ref, kT_ref, k_ref, v_ref, st_ref, out_ref = refs[dirn]
        cc = nchunk - 1 - step if dirn else step
        rows = slice(cc * M_CHUNK, (cc + 1) * M_CHUNK)
        st = st_ref[cc]
        stats = [tuple(st[n * 8 + hd * 4 + dirn * 2:n * 8 + hd * 4 + dirn * 2 + 1]
                       for n in range(_N_STATS)) for hd in range(2)]
        return qT_ref[cc], kT_ref[cc], k_ref[rows, :], v_ref[rows, :], stats, out_ref, rows

    pre = {}
    for step in range(nchunk):
        for dirn in range(2):
            qT, kT, k, v2, stats, _, _ = unit(step, dirn)
            pre[step, dirn] = _mlstm_unit_pre(qT, kT, k, v2, stats, reverse=bool(dirn))
    for step in range(nchunk):
        for dirn in range(2):
            qT, _, _, v2, stats, out_ref, rows = unit(step, dirn)
            h, c_state[dirn], m_state[dirn] = _mlstm_unit_post(
                qT, v2, stats, pre[step, dirn], c_state[dirn], m_state[dirn])
            out_ref[rows, :] = h.astype(out_ref.dtype)

    for dirn in range(2):
        c_ref[dirn] = c_state[dirn]
        for hd in range(2):
            m_ref[hd * 2 + dirn] = jnp.broadcast_to(m_state[dirn][hd], m_ref.shape[1:])


def _mlstm_core(qT, kT, k, v, gT, rb, m1, s1, s2):
    m = k.shape[0]
    nblk = m // rb
    nchunk = rb // M_CHUNK
    npair = M_HEADS // 2
    fwd = lambda p, i: (i, p)
    bwd = lambda p, i: (nblk - 1 - i, p)
    fwd3 = lambda p, i: (i, p, 0)
    bwd3 = lambda p, i: (nblk - 1 - i, p, 0)

    def specs(im2, im3):
        return [
            pl.BlockSpec((nchunk, LANES, M_CHUNK), im3),
            pl.BlockSpec((nchunk, LANES, M_CHUNK), im3),
            pl.BlockSpec((rb, LANES), im2),
            pl.BlockSpec((rb, 2 * M_V), im2),
            pl.BlockSpec((nchunk, _N_STATS * 8, M_CHUNK), im3),
        ]

    kern = functools.partial(_mlstm_core_kernel, nchunk=nchunk, rb=rb, m1=m1, s1=s1, s2=s2)
    return pl.pallas_call(
        kern,
        grid=(npair, nblk),
        in_specs=specs(fwd, fwd3) + specs(bwd, bwd3),
        out_specs=[pl.BlockSpec((rb, 2 * M_V), fwd), pl.BlockSpec((rb, 2 * M_V), bwd)],
        out_shape=[jax.ShapeDtypeStruct((m, M_HEADS * M_V), BF16)] * 2,
        scratch_shapes=[pltpu.VMEM((2, LANES, 3 * M_V), F32), pltpu.VMEM((4, 8, M_CHUNK), F32)],
        compiler_params=_params(("parallel", "arbitrary"), 32),
        name="mlstm_core",
    )(qT, kT, k, v, gT, qT, kT, k, v, gT)


def _mlstm_out_kernel(hf_ref, hb_ref, o_ref, gain_ref, w_ref, xa_ref, xb_ref, out_ref, *, na):
    h = hf_ref[...].astype(F32) + hb_ref[...].astype(F32)
    parts = []
    for hd in range(M_HEADS):
        hh = h[:, hd * M_V:(hd + 1) * M_V]
        ms = jnp.mean(hh * hh, axis=-1, keepdims=True)
        parts.append(hh * lax.rsqrt(ms + EPS))
    hn = jnp.concatenate(parts, axis=1) * gain_ref[...]
    hg = (_sigmoid(o_ref[...].astype(F32)) * hn).astype(BF16)
    out_ref[...] = _pair_tile(xa_ref, xb_ref, na) + _dot(hg, w_ref[...])


def _mlstm_out(hf, hb, o, gain, w, x, tm):
    xa, xb, na = _as_pair(x, tm)
    m = hf.shape[0]
    row = lambda i: (i, 0)
    fix = lambda i: (0, 0)
    blk = pl.BlockSpec((tm, D_MODEL), row)
    return pl.pallas_call(
        functools.partial(_mlstm_out_kernel, na=na),
        grid=(m // tm,),
        in_specs=[blk, blk, blk, pl.BlockSpec((1, D_MODEL), fix),
                  pl.BlockSpec((D_MODEL, D_MODEL), fix)] + _pair_specs(tm, na),
        out_specs=blk,
        out_shape=jax.ShapeDtypeStruct((m, D_MODEL), F32),
        compiler_params=_params(("parallel",), 40),
        name="mlstm_out",
    )(hf, hb, o, gain, w, xa, xb)


def _ffn_kernel(x_ref, g_ref, wg_ref, wu_ref, wd_ref, out_ref, hn_ref, acc_ref):
    f = pl.program_id(1)

    @pl.when(f == 0)
    def _():
        hn_ref[...] = _rms(x_ref[...], g_ref[...]).astype(BF16)
        acc_ref[...] = jnp.zeros_like(acc_ref)

    hn = hn_ref[...]
    gate = _dot(hn, wg_ref[...])
    up = _dot(hn, wu_ref[...])
    act = (gate * _sigmoid(gate) * up).astype(BF16)
    acc_ref[...] += _dot(act, wd_ref[...])

    @pl.when(f == pl.num_programs(1) - 1)
    def _():
        out_ref[...] = x_ref[...] + acc_ref[...]


def _ffn(x, g, wg, wu, wd, tm, tf):
    m = x.shape[0]
    row = lambda i, f: (i, 0)
    return pl.pallas_call(
        _ffn_kernel,
        grid=(m // tm, D_FF // tf),
        in_specs=[
            pl.BlockSpec((tm, D_MODEL), row),
            pl.BlockSpec((1, D_MODEL), lambda i, f: (0, 0)),
            pl.BlockSpec((D_MODEL, tf), lambda i, f: (0, f)),
            pl.BlockSpec((D_MODEL, tf), lambda i, f: (0, f)),
            pl.BlockSpec((tf, D_MODEL), lambda i, f: (f, 0)),
        ],
        out_specs=pl.BlockSpec((tm, D_MODEL), row),
        out_shape=jax.ShapeDtypeStruct((m, D_MODEL), F32),
        scratch_shapes=[pltpu.VMEM((tm, D_MODEL), BF16), pltpu.VMEM((tm, D_MODEL), F32)],
        compiler_params=_params(("parallel", "arbitrary"), 48),
        name="ffn",
    )(x, g, wg, wu, wd)


def _rope_table_kernel(inv_ref, ma_ref, mb_ref, cos_ref, sa_ref, sb_ref):
    rows = cos_ref.shape[0]
    pos = pl.program_id(0) * rows + lax.broadcasted_iota(jnp.int32, (rows, LANES), 0)
    ang = pos.astype(F32) * inv_ref[...]
    sin = jnp.sin(ang)
    cos_ref[...] = jnp.cos(ang)
    sa_ref[...] = sin * ma_ref[...]
    sb_ref[...] = sin * mb_ref[...]


def _rope_tables(smax, rows):
    half = ROPE_DIM // 2
    d = jnp.arange(LANES) % A_HD
    inv = ROPE_THETA ** (-(jnp.arange(half, dtype=F32) * 2.0) / ROPE_DIM)
    inv_lane = jnp.where(d < ROPE_DIM, inv[d % half], 0.0).astype(F32)[None, :]
    ma = jnp.where(d < half, -1.0, 0.0).astype(F32)[None, :]
    mb = jnp.where((d >= half) & (d < ROPE_DIM), 1.0, 0.0).astype(F32)[None, :]
    fix = lambda i: (0, 0)
    vec = pl.BlockSpec((1, LANES), fix)
    tab = pl.BlockSpec((rows, LANES), lambda i: (i, 0))
    return pl.pallas_call(
        _rope_table_kernel,
        grid=(smax // rows,),
        in_specs=[vec, vec, vec],
        out_specs=[tab, tab, tab],
        out_shape=[jax.ShapeDtypeStruct((smax, LANES), F32)] * 3,
        compiler_params=_params(("parallel",), 16),
        name="rope_tables",
    )(inv_lane, ma, mb)


def _attn_in_kernel(x_ref, g_ref, wq_ref, wk_ref, wv_ref, qg_ref, kg_ref, gm_ref,
                    cos_ref, sa_ref, sb_ref, q_ref, k_ref, v_ref):
    hn = _rms(x_ref[...], g_ref[...]).astype(BF16)
    cos = cos_ref[...]
    sa = sa_ref[...]
    sb = sb_ref[...]
    half = ROPE_DIM // 2

    def tiles(x):
        return [x[:, j * LANES:(j + 1) * LANES] for j in range(x.shape[1] // LANES)]

    def norm_rope(xj, gain):
        ms = _dot((xj * xj).astype(BF16), gm_ref[...])
        y = xj * lax.rsqrt(ms + EPS) * gain
        return y * cos + pltpu.roll(y, LANES - half, 1) * sa + pltpu.roll(y, half, 1) * sb

    def store_lo_hi(pair_tiles, out_ref):
        low = lax.broadcasted_iota(jnp.int32, pair_tiles[0].shape, 1) < A_HD
        for j, y in enumerate(pair_tiles):
            for half_idx in range(2):
                mine = jnp.where(low if half_idx == 0 else jnp.logical_not(low), y, 0.0)
                other = pltpu.roll(mine, A_HD, 1)
                lo, hi = (mine, other) if half_idx == 0 else (other, mine)
                col = (2 * j + half_idx) * 2 * LANES
                out_ref[:, col:col + LANES] = lo.astype(BF16)
                out_ref[:, col + LANES:col + 2 * LANES] = hi.astype(BF16)

    for j, xj in enumerate(tiles(_dot(hn, wq_ref[...]))):
        q_ref[:, j * LANES:(j + 1) * LANES] = (norm_rope(xj, qg_ref[...]) * (A_HD ** -0.5)).astype(BF16)
    store_lo_hi([norm_rope(xj, kg_ref[...]) for xj in tiles(_dot(hn, wk_ref[...]))], k_ref)
    store_lo_hi(tiles(_dot(hn, wv_ref[...])), v_ref)


def _attn_in(x, g, wq, wk, wv, qg, kg, gm, tabs, tm, m1, s1, s2):
    m = x.shape[0]
    nq = A_Q_HEADS * A_HD
    nkv = A_KV_HEADS * A_HD
    row = lambda i: (i, 0)
    fix = lambda i: (0, 0)

    def tab_map(i):
        pos, _ = _seq_pos(i * tm, m1, s1, s2)
        return (pos // tm, 0)

    tab = pl.BlockSpec((tm, LANES), tab_map)
    vec = pl.BlockSpec((1, LANES), fix)
    return pl.pallas_call(
        _attn_in_kernel,
        grid=(m // tm,),
        in_specs=[
            pl.BlockSpec((tm, D_MODEL), row),
            pl.BlockSpec((1, D_MODEL), fix),
            pl.BlockSpec((D_MODEL, nq), fix),
            pl.BlockSpec((D_MODEL, nkv), fix),
            pl.BlockSpec((D_MODEL, nkv), fix),
            vec, vec,
            pl.BlockSpec((LANES, LANES), fix),
            tab, tab, tab,
        ],
        out_specs=[pl.BlockSpec((tm, nq), row), pl.BlockSpec((tm, _A_KV_COLS), row),
                   pl.BlockSpec((tm, _A_KV_COLS), row)],
        out_shape=[jax.ShapeDtypeStruct((m, nq), BF16), jax.ShapeDtypeStruct((m, _A_KV_COLS), BF16),
                   jax.ShapeDtypeStruct((m, _A_KV_COLS), BF16)],
        compiler_params=_params(("parallel",), 40),
        name="attn_in",
    )(x, g, wq, wk, wv, qg, kg, gm, *tabs)


def _attn_kernel(sink_ref, q_ref, kp_ref, kc_ref, kn_ref, vp_ref, vc_ref, vn_ref, out_ref,
                 *, m1, s1, s2):
    blk = A_BLOCK
    pos0, slen = _seq_pos(pl.program_id(0) * blk, m1, s1, s2)
    prev_ok = pos0 > 0
    next_ok = pos0 + blk < slen
    t = lax.broadcasted_iota(jnp.int32, (blk, 3 * blk), 0)
    c = lax.broadcasted_iota(jnp.int32, (blk, 3 * blk), 1)
    valid = (jnp.abs(c - blk - t) <= WINDOW) & ((c >= blk) | prev_ok) & ((c < 2 * blk) | next_ok)

    kcat = jnp.concatenate([kp_ref[...], kc_ref[...], kn_ref[...]], axis=0)
    vcat = jnp.concatenate([vp_ref[...], vc_ref[...], vn_ref[...]], axis=0)
    lane_q = lax.broadcasted_iota(jnp.int32, (blk, LANES), 1)

    def softmax_parts(s, sink):
        s = jnp.concatenate([jnp.where(valid[:, :blk], s[:, :blk], -jnp.inf), s[:, blk:2 * blk],
                             jnp.where(valid[:, 2 * blk:], s[:, 2 * blk:], -jnp.inf)], axis=1)
        m = jnp.maximum(jnp.max(s, axis=1, keepdims=True), sink)
        p = jnp.exp(s - m)
        denom = jnp.sum(p, axis=1, keepdims=True) + jnp.exp(sink - m)
        return p.astype(BF16), denom

    def lo_hi(cat, h):
        col = h * 2 * LANES
        return jnp.concatenate([cat[:, col:col + LANES], cat[:, col + LANES:col + 2 * LANES]], axis=0)

    npair = A_GROUP // 2
    kk = [lo_hi(kcat, h) for h in range(A_KV_HEADS)]
    vv = [lo_hi(vcat, h) for h in range(A_KV_HEADS)]
    def score(h):
        q_pairs = jnp.concatenate([q_ref[:, (h * npair + j) * LANES:(h * npair + j + 1) * LANES]
                                   for j in range(npair)], axis=0)
        return lax.dot_general(q_pairs, kk[h], _NT, preferred_element_type=F32)

    scores = {0: score(0)}
    for h in range(A_KV_HEADS):
        if h + 1 < A_KV_HEADS:
            scores[h + 1] = score(h + 1)
        p_rows, inv_rows = [], []
        for j in range(npair):
            parts = [softmax_parts(scores[h][j * blk:(j + 1) * blk, par * 3 * blk:(par + 1) * 3 * blk],
                                   sink_ref[(h * npair + j) * 2 + par]) for par in range(2)]
            p_rows.append(jnp.concatenate([parts[0][0], parts[1][0]], axis=1))
            inv_rows.append(jnp.where(lane_q < A_HD, 1.0 / parts[0][1], 1.0 / parts[1][1]))
        o = _dot(jnp.concatenate(p_rows, axis=0), vv[h])
        for j in range(npair):
            grp = h * npair + j
            out_ref[:, grp * LANES:(grp + 1) * LANES] = (o[j * blk:(j + 1) * blk] * inv_rows[j]).astype(BF16)


def _attn(sink, q, k, v, m1, s1, s2):
    m = q.shape[0]
    blk = A_BLOCK
    nblk = m // blk
    cur = lambda i: (i, 0)
    prev = lambda i: (jnp.maximum(i - 1, 0), 0)
    nxt = lambda i: (jnp.minimum(i + 1, nblk - 1), 0)
    kv = lambda im: pl.BlockSpec((blk, _A_KV_COLS), im)
    kern = functools.partial(_attn_kernel, m1=m1, s1=s1, s2=s2)
    return pl.pallas_call(
        kern,
        grid=(nblk,),
        in_specs=[pl.BlockSpec(memory_space=pltpu.SMEM),
                  pl.BlockSpec((blk, D_MODEL), cur), kv(prev), kv(cur), kv(nxt),
                  kv(prev), kv(cur), kv(nxt)],
        out_specs=pl.BlockSpec((blk, D_MODEL), cur),
        out_shape=jax.ShapeDtypeStruct((m, D_MODEL), BF16),
        compiler_params=_params(("parallel",), 32),
        name="attn",
    )(sink, q, k, k, k, v, v, v)


_R_E1, _R_E2, _R_P1, _R_P2, _R_RANK1, _R_RANK2 = range(6)


def _router_kernel(a_ref, w_ref, x_ref, g_ref, wr_ref, x2_ref, route_ref, route_t_ref, count_ref,
                   carry_ref):
    @pl.when(pl.program_id(0) == 0)
    def _():
        carry_ref[...] = jnp.zeros_like(carry_ref)

    x2 = x_ref[...] + _dot(a_ref[...], w_ref[...])
    x2_ref[...] = x2
    hn = _rms(x2, g_ref[...]).astype(BF16)
    logits = _dot(hn, wr_ref[...])
    tm = logits.shape[0]
    lane = lax.broadcasted_iota(jnp.int32, logits.shape, 1).astype(F32)
    logits = jnp.where(lane < N_EXPERTS, logits, -jnp.inf)
    v1 = jnp.max(logits, axis=1, keepdims=True)
    i1 = jnp.min(jnp.where(logits == v1, lane, float(LANES)), axis=1, keepdims=True)
    rest = jnp.where(lane == i1, -jnp.inf, logits)
    v2 = jnp.max(rest, axis=1, keepdims=True)
    i2 = jnp.min(jnp.where(rest == v2, lane, float(LANES)), axis=1, keepdims=True)
    e2 = jnp.exp(v2 - v1)
    p1 = 1.0 / (1.0 + e2)
    p2 = e2 * p1

    sel = jnp.where((lane == i1) | (lane == i2), 1.0, 0.0)
    t_row = lax.broadcasted_iota(jnp.int32, (tm, tm), 0)
    t_col = lax.broadcasted_iota(jnp.int32, (tm, tm), 1)
    earlier = jnp.where(t_col < t_row, 1.0, 0.0).astype(BF16)
    before = _dot(earlier, sel.astype(BF16)) + carry_ref[0:1, :]
    rank1 = jnp.sum(jnp.where(lane == i1, before, 0.0), axis=1, keepdims=True)
    rank2 = jnp.sum(jnp.where(lane == i2, before, 0.0), axis=1, keepdims=True)
    total = carry_ref[0:1, :] + jnp.sum(sel, axis=0, keepdims=True)
    carry_ref[...] = jnp.broadcast_to(total, carry_ref.shape)
    count_ref[...] = jnp.broadcast_to(total, count_ref.shape)

    route = jnp.zeros_like(logits)
    for col, val in ((_R_E1, i1), (_R_E2, i2), (_R_P1, p1), (_R_P2, p2),
                     (_R_RANK1, rank1), (_R_RANK2, rank2)):
        route = jnp.where(lane == float(col), val, route)
    route_ref[...] = route
    route_t_ref[...] = route.T[:8]


def _proj_route(a, w, x, g, wr, tm):
    m = x.shape[0]
    row = lambda i: (i, 0)
    fix = lambda i: (0, 0)
    blk = pl.BlockSpec((tm, D_MODEL), row)
    return pl.pallas_call(
        _router_kernel,
        grid=(m // tm,),
        in_specs=[blk, pl.BlockSpec((D_MODEL, D_MODEL), fix), blk, pl.BlockSpec((1, D_MODEL), fix),
                  pl.BlockSpec((D_MODEL, LANES), fix)],
        out_specs=[blk, pl.BlockSpec((tm, LANES), row), pl.BlockSpec((8, tm), lambda i: (0, i)),
                   pl.BlockSpec((8, LANES), fix)],
        out_shape=[jax.ShapeDtypeStruct((m, D_MODEL), F32), jax.ShapeDtypeStruct((m, LANES), F32),
                   jax.ShapeDtypeStruct((8, m), F32), jax.ShapeDtypeStruct((8, LANES), F32)],
        scratch_shapes=[pltpu.VMEM((8, LANES), F32)],
        compiler_params=_params(("arbitrary",), 40),
        name="attn_out_router",
    )(a, w, x, g, wr)


def _invert_kernel(pos_ref, tab_ref):
    i = pl.program_id(0)
    ta = pos_ref.shape[-1]

    @pl.when(i == 0)
    def _():
        tab_ref[...] = jnp.full(tab_ref.shape, -1, jnp.int32)

    lane = lax.broadcasted_iota(jnp.int32, (1, LANES), 1)
    unroll = 8

    def body(jj, carry):
        for u in range(unroll):
            j = jj * unroll + u
            p = pos_ref[0, 0, j]
            row = lax.shift_right_logical(p, 7)
            pltpu.store(tab_ref.at[pl.ds(row, 1), :], jnp.broadcast_to(i * ta + j, (1, LANES)),
                        mask=lane == (p & (LANES - 1)))
        return carry

    lax.fori_loop(0, ta // unroll, body, 0)


def _invert(pos, n_rows, ta):
    n = pos.shape[0]
    assert n % ta == 0 and n_rows % LANES == 0
    return pl.pallas_call(
        _invert_kernel,
        grid=(n // ta,),
        in_specs=[pl.BlockSpec((1, 1, ta), lambda i: (i, 0, 0), memory_space=pltpu.SMEM)],
        out_specs=pl.BlockSpec((n_rows // LANES, LANES), lambda i: (0, 0)),
        out_shape=jax.ShapeDtypeStruct((n_rows // LANES, LANES), jnp.int32),
        compiler_params=_params(("arbitrary",), 16),
        name="moe_invert",
    )(pos.reshape(n // ta, 1, ta)).reshape(n_rows)


def _experts_kernel(te_ref, nused_ref, src0_ref, src_next_ref, dst_prev_ref, x_hbm, g_ref,
                    wg_hbm, wu_hbm, wd_hbm, y_hbm, xbuf, acc_ref, hn_ref, wg_buf, wu_buf, wd_buf,
                    gsem, ssem, zsem, wsem, *, n_real, n_trash):
    t = pl.program_id(0)
    nused = nused_ref[0]
    nf, _, tf = wg_buf.shape
    tm = hn_ref.shape[0]
    chunk = tm // nf
    slot = t % 2
    other = 1 - slot

    def gather_row(idx_ref, j, s):
        return pltpu.make_async_copy(x_hbm.at[pl.ds(idx_ref[0, 0, j], 1)],
                                     xbuf.at[s, pl.ds(j, 1)], gsem.at[s])

    def scatter_row(j, s):
        return pltpu.make_async_copy(acc_ref.at[s, pl.ds(j, 1)],
                                     y_hbm.at[pl.ds(dst_prev_ref[0, 0, j], 1)], ssem.at[s])

    def issue_gather(row0):
        for j in range(chunk):
            gather_row(src_next_ref, row0 + j, other).start(priority=j % 2)

    def issue_scatter(row0):
        for j in range(chunk):
            scatter_row(row0 + j, other).start(priority=j % 2)

    def compute(f):
        hn = hn_ref[...]
        gate = _dot(hn, wg_buf[f])
        up = _dot(hn, wu_buf[f])
        act = (gate * _sigmoid(gate) * up).astype(BF16)
        acc_ref[slot] += _dot(act, wd_buf[f])

    def chunks(body):
        def step(f, carry):
            body(f, f * chunk)
            return carry
        lax.fori_loop(0, nf, step, 0)

    @pl.when(t == 0)
    def _():
        acc_ref[1] = jnp.zeros(acc_ref.shape[1:], F32)
        fills = [pltpu.make_async_copy(acc_ref.at[1], y_hbm.at[pl.ds(n_real + k * tm, tm)], zsem)
                 for k in range(n_trash // tm)]
        for c in fills:
            c.start()
        for c in fills:
            c.wait()

        def body(jj, carry):
            for u in range(8):
                gather_row(src0_ref, jj * 8 + u, 0).start()
            return carry

        lax.fori_loop(0, tm // 8, body, 0)

    expert = te_ref[t]

    @pl.when((t < nused) & ((t == 0) | (expert != te_ref[jnp.maximum(t - 1, 0)])))
    def _():
        loads = []
        for f in range(nf):
            cols = pl.ds(f * tf, tf)
            loads += [pltpu.make_async_copy(wg_hbm.at[expert, :, cols], wg_buf.at[f], wsem),
                      pltpu.make_async_copy(wu_hbm.at[expert, :, cols], wu_buf.at[f], wsem),
                      pltpu.make_async_copy(wd_hbm.at[expert, cols, :], wd_buf.at[f], wsem)]
        for c in loads:
            c.start()
        for c in loads:
            c.wait()

    @pl.when(t <= nused)
    def _():
        pltpu.make_async_copy(x_hbm.at[pl.ds(0, tm)], xbuf.at[slot], gsem.at[slot]).wait()

    @pl.when((t >= 2) & (t - 2 < nused))
    def _():
        pltpu.make_async_copy(acc_ref.at[slot], y_hbm.at[pl.ds(0, tm)], ssem.at[slot]).wait()

    @pl.when(t < nused)
    def _():
        hn_ref[...] = _rms(xbuf[slot], g_ref[...]).astype(BF16)
        acc_ref[slot] = jnp.zeros(acc_ref.shape[1:], F32)

    @pl.when(t == 0)
    def _():
        def body(f, row0):
            issue_gather(row0)
            compute(f)
        chunks(body)

    @pl.when((t >= 1) & (t < nused))
    def _():
        def body(f, row0):
            issue_gather(row0)
            issue_scatter(row0)
            compute(f)
        chunks(body)

    @pl.when(t == nused)
    def _():
        chunks(lambda f, row0: issue_scatter(row0))


def _experts(tile_expert, nused, src, dst, x, g, wg, wu, wd, tm, tf):
    m = x.shape[0]
    ntile = src.shape[0]
    n_trash = N_EXPERTS * tm
    nf = D_FF // tf
    assert tm % nf == 0 and D_FF % tf == 0
    idx = lambda im: pl.BlockSpec((1, 1, tm), im, memory_space=pltpu.SMEM)
    any_spec = pl.BlockSpec(memory_space=pl.ANY)
    kern = functools.partial(_experts_kernel, n_real=2 * m, n_trash=n_trash)
    return pl.pallas_call(
        kern,
        grid_spec=pltpu.PrefetchScalarGridSpec(
            num_scalar_prefetch=2,
            grid=(ntile,),
            in_specs=[
                idx(lambda t, te, nu: (0, 0, 0)),
                idx(lambda t, te, nu: (jnp.minimum(t + 1, ntile - 1), 0, 0)),
                idx(lambda t, te, nu: (jnp.maximum(t - 1, 0), 0, 0)),
                any_spec,
                pl.BlockSpec((1, D_MODEL), lambda t, te, nu: (0, 0)),
                any_spec, any_spec, any_spec,
            ],
            out_specs=any_spec,
            scratch_shapes=[pltpu.VMEM((2, tm, D_MODEL), F32), pltpu.VMEM((2, tm, D_MODEL), F32),
                            pltpu.VMEM((tm, D_MODEL), BF16),
                            pltpu.VMEM((nf, D_MODEL, tf), BF16), pltpu.VMEM((nf, D_MODEL, tf), BF16),
                            pltpu.VMEM((nf, tf, D_MODEL), BF16),
                            pltpu.SemaphoreType.DMA((2,)), pltpu.SemaphoreType.DMA((2,)),
                            pltpu.SemaphoreType.DMA(()), pltpu.SemaphoreType.DMA(())],
        ),
        out_shape=jax.ShapeDtypeStruct((2 * m + n_trash, D_MODEL), F32),
        compiler_params=_params(("arbitrary",), 56),
        name="moe_experts",
    )(tile_expert, nused, src, src, dst, x, g, wg, wu, wd)


def _combine_kernel(x_ref, route_ref, y1_ref, y2_ref, outa_ref, outb_ref, *, na):
    route = route_ref[...]
    p1 = route[:, _R_P1:_R_P1 + 1]
    p2 = route[:, _R_P2:_R_P2 + 1]
    out = x_ref[...] + (p1 * y1_ref[...] + p2 * y2_ref[...])
    first = pl.program_id(0) < na

    @pl.when(first)
    def _():
        outa_ref[...] = out

    @pl.when(jnp.logical_not(first))
    def _():
        outb_ref[...] = out


def _combine(x, route, y, tc, m1):
    m = x.shape[0]
    na = m1 // tc
    row = lambda i: (i, 0)
    blk = pl.BlockSpec((tc, D_MODEL), row)
    return pl.pallas_call(
        functools.partial(_combine_kernel, na=na),
        grid=(m // tc,),
        in_specs=[blk, pl.BlockSpec((tc, LANES), row), blk,
                  pl.BlockSpec((tc, D_MODEL), lambda i: (i + m // tc, 0))],
        out_specs=_pair_specs(tc, na),
        out_shape=[jax.ShapeDtypeStruct((m1, D_MODEL), F32),
                   jax.ShapeDtypeStruct((m - m1, D_MODEL), F32)],
        compiler_params=_params(("arbitrary",), 32),
        name="moe_combine",
    )(x, route, y, y)


def _moe(att, w_out, x, g, wr, wg, wu, wd, tm_route, tm, tf, m1):
    m = x.shape[0]
    x, route, route_t, counts = _proj_route(att, w_out, x, g, wr, tm_route)
    e1 = route_t[_R_E1].astype(jnp.int32)
    e2 = route_t[_R_E2].astype(jnp.int32)
    n_e = counts[0, :N_EXPERTS].astype(jnp.int32)
    tiles_e = (n_e + tm - 1) // tm
    tile_end = jnp.cumsum(tiles_e)
    row_start = (tile_end - tiles_e) * tm
    pos = jnp.concatenate([row_start[e1] + route_t[_R_RANK1].astype(jnp.int32),
                           row_start[e2] + route_t[_R_RANK2].astype(jnp.int32)])
    ntile = pl.cdiv(2 * m, tm) + N_EXPERTS + 2
    nused = tile_end[-1:]
    tile_ids = jnp.minimum(jnp.arange(ntile, dtype=jnp.int32), nused - 1)
    tile_expert = jnp.sum(tile_ids[:, None] >= tile_end[None, :], axis=1).astype(jnp.int32)
    rows = jnp.arange(ntile * tm, dtype=jnp.int32)
    spare = 2 * m + rows - jnp.cumsum(n_e)[jnp.repeat(tile_expert, tm)]
    spare = jnp.clip(spare, 2 * m, 2 * m + N_EXPERTS * tm - 1)
    assign = _invert(pos, ntile * tm, tm_route)
    src = jnp.where(assign >= m, assign - m, jnp.maximum(assign, 0))
    dst = jnp.where(assign >= 0, assign, spare)
    y = _experts(tile_expert, nused.astype(jnp.int32), src.reshape(ntile, 1, tm),
                 dst.reshape(ntile, 1, tm), x, g, wg, wu, wd, tm, tf)
    return tuple(_combine(x, route, y, tm_route, m1))


def _mlstm_gate_perm():
    perm = []
    for pair in range(M_HEADS // 2):
        for hh in range(2):
            for gtype in range(4):
                perm.append(gtype * M_HEADS + 2 * pair + hh)
    return jnp.array(perm, jnp.int32)


def kernel(x_prompt, x_sample, norm_mix, norm_ffn, mlstm_w_in, mlstm_gate_bias, mlstm_head_gain,
           mlstm_w_out, attn_w_in, attn_q_gain, attn_k_gain, attn_sink, attn_w_out, ffn_w_gate,
           ffn_w_up, ffn_w_down, moe_w_router, moe_w_gate, moe_w_up, moe_w_down):
    b1, s1, _ = x_prompt.shape
    b2, s2, _ = x_sample.shape
    m1 = b1 * s1
    x = (x_prompt.reshape(m1, D_MODEL), x_sample.reshape(b2 * s2, D_MODEL))
    m = m1 + b2 * s2
    seq_gcd = math.gcd(s1, s2)
    tm = _tile(seq_gcd, 512)
    rb = _tile(seq_gcd, 2048)
    tm_ffn = _tile(m, 1024)
    tf = 512
    tm_moe = LANES * (D_FF // tf)
    depth = norm_mix.shape[0]
    vec = lambda a: a.astype(F32).reshape(1, -1)
    nq = M_HEADS * M_QK
    nv = M_HEADS * M_V

    tabs = _rope_tables(max(s1, s2), tm)
    group_mean = jnp.where((jnp.arange(LANES)[:, None] // A_HD) == (jnp.arange(LANES)[None, :] // A_HD),
                           1.0 / A_HD, 0.0).astype(BF16)
    gate_perm = _mlstm_gate_perm()

    for i in range(depth):
        j = i // 2
        if i % 2 == 0:
            w_in = mlstm_w_in[j]
            wqT = w_in[:, :nq].T.astype(BF16)
            wk = w_in[:, nq:2 * nq].astype(BF16)
            wv = w_in[:, 2 * nq:2 * nq + nv].astype(BF16)
            wo = w_in[:, 2 * nq + nv:2 * nq + 2 * nv].astype(BF16)
            wgT = w_in[:, 2 * nq + 2 * nv:][:, gate_perm].T.astype(BF16)
            bgT = mlstm_gate_bias[j].astype(F32)[gate_perm].reshape(-1, 1)
            qT, kT, k, v, o, gT = _mlstm_in(x, vec(norm_mix[i]), wqT, wk.T, wk, wv, wo, wgT, bgT, tm)
            hf, hb = _mlstm_core(qT, kT, k, v, gT, rb, m1, s1, s2)
            x = _mlstm_out(hf, hb, o, vec(mlstm_head_gain[j]), mlstm_w_out[j].astype(BF16), x, tm)
            x = _ffn(x, vec(norm_ffn[i]), ffn_w_gate[j].astype(BF16), ffn_w_up[j].astype(BF16),
                     ffn_w_down[j].astype(BF16), tm_ffn, tf)
        else:
            w_in = attn_w_in[j]
            a_q = A_Q_HEADS * A_HD
            a_kv = A_KV_HEADS * A_HD
            qg = jnp.tile(attn_q_gain[j].astype(F32), LANES // A_HD).reshape(1, LANES)
            kg = jnp.tile(attn_k_gain[j].astype(F32), LANES // A_HD).reshape(1, LANES)
            q, k, v = _attn_in(x, vec(norm_mix[i]), w_in[:, :a_q].astype(BF16),
                               w_in[:, a_q:a_q + a_kv].astype(BF16), w_in[:, a_q + a_kv:].astype(BF16),
                               qg, kg, group_mean, tabs, tm, m1, s1, s2)
            att = _attn(attn_sink[j].astype(F32), q, k, v, m1, s1, s2)
            wr = jnp.pad(moe_w_router[j], ((0, 0), (0, LANES - N_EXPERTS))).astype(BF16)
            x = _moe(att, attn_w_out[j].astype(BF16), x, vec(norm_ffn[i]), wr,
                     moe_w_gate[j].astype(BF16), moe_w_up[j].astype(BF16),
                     moe_w_down[j].astype(BF16), tm, tm_moe, tf, m1)

    xa, xb = x if isinstance(x, tuple) else (x[:m1], x[m1:])
    return xa.reshape(b1, s1, D_MODEL), xb.reshape(b2, s2, D_MODEL)
```
